```python
import jax
import jax.numpy as jnp
from jax import lax
import numpy as np

D_MODEL = 1024
BATCH = 8
SEQ = 4096
DEPTH = 2

GRID_W = 64
CTX_LEN = 256
NORM_EPS = 1e-6
N_MOD = 6

POOL_WINDOWS = (2, 4, 8, 16)
POOL_WIDTH = D_MODEL // 4
POOL_GROUP = POOL_WIDTH // len(POOL_WINDOWS)

MLA_HEADS = 8
MLA_NOPE = 64
MLA_ROPE = 32
MLA_V = 64
MLA_Q_RANK = 384
MLA_KV_RANK = 256
MLA_WIDTH = MLA_HEADS * MLA_V
Q_BLOCK = 128
ROPE_BASE = 10000.0

RWKV_HEAD = 64
RWKV_WIDTH = D_MODEL // 4
RWKV_HEADS = RWKV_WIDTH // RWKV_HEAD
DECAY_RANK = 64
AAA_RANK = 64
GATE_RANK = 128
RWKV_GN_EPS = 64e-5
RWKV_SIZES = (RWKV_WIDTH, RWKV_WIDTH, RWKV_WIDTH, DECAY_RANK, DECAY_RANK, AAA_RANK, AAA_RANK, GATE_RANK)
RWKV_IN = 3 * RWKV_WIDTH + 2 * DECAY_RANK + 2 * AAA_RANK + GATE_RANK

N_BRANCH = 3
IN_SIZES = (POOL_WIDTH, MLA_Q_RANK, MLA_KV_RANK, MLA_ROPE, RWKV_IN, N_BRANCH * D_MODEL)
IN_COLS = POOL_WIDTH + MLA_Q_RANK + MLA_KV_RANK + MLA_ROPE + RWKV_IN + N_BRANCH * D_MODEL
D_FF = 4 * D_MODEL

kernel_name = 'hybrid_pool_mla_rwkv7_dit_block'


def split_cols(z, sizes):
    idx = []
    acc = 0
    for s in sizes[:-1]:
        acc += s
        idx.append(acc)
    return jnp.split(z, idx, axis=-1)


def rms_norm(x, g, eps=NORM_EPS):
    xf = x.astype(jnp.float32)
    y = xf * lax.rsqrt(jnp.mean(xf * xf, axis=-1, keepdims=True) + eps)
    return (y * g).astype(x.dtype)


def modulate(h, shift, scale):
    return h * (1.0 + scale) + shift


def axial_rope_tables(n_tokens):
    rows = n_tokens // GRID_W
    row = jnp.repeat(jnp.arange(rows), GRID_W).astype(jnp.float32)
    col = jnp.tile(jnp.arange(GRID_W), rows).astype(jnp.float32)
    n_freq = MLA_ROPE // 4
    inv_freq = jnp.power(ROPE_BASE, -jnp.arange(n_freq, dtype=jnp.float32) / n_freq)
    ang = jnp.concatenate([row[:, None] * inv_freq, col[:, None] * inv_freq], axis=-1)
    return jnp.cos(ang), jnp.sin(ang)


def apply_rope(x, cos, sin):
    half = MLA_ROPE // 2
    cos = cos.astype(x.dtype)
    sin = sin.astype(x.dtype)
    x1, x2 = x[..., :half], x[..., half:]
    return jnp.concatenate([x1 * cos - x2 * sin, x1 * sin + x2 * cos], axis=-1)


def pool_mixer(u, pool_w, pool_scale):
    B, L, _ = u.shape
    uf = u.astype(jnp.float32)
    csum = jnp.concatenate([jnp.zeros((B, 1, POOL_WIDTH), jnp.float32), jnp.cumsum(uf, axis=1)], axis=1)
    t = jnp.arange(L)
    groups = []
    for gi, win in enumerate(POOL_WINDOWS):
        sl = slice(gi * POOL_GROUP, (gi + 1) * POOL_GROUP)
        lo = jnp.clip(t - win // 2, 0, L)
        hi = jnp.clip(t + win // 2, 0, L)
        total = jnp.take(csum[..., sl], hi, axis=1) - jnp.take(csum[..., sl], lo, axis=1)
        mean = total / (hi - lo).astype(jnp.float32)[:, None]
        groups.append(mean - uf[..., sl])
    pooled = jnp.stack(groups, axis=2).astype(u.dtype)
    y = jnp.einsum('blgc,gcd->blgd', pooled, pool_w).reshape(B, L, POOL_WIDTH)
    return y * pool_scale


def mla_queries(q_c, p):
    B, L, _ = q_c.shape
    q = (rms_norm(q_c, p['mla_q_norm']) @ p['mla_w_uq']).reshape(B, L, MLA_HEADS, MLA_NOPE + MLA_ROPE)
    q_nope = rms_norm(q[..., :MLA_NOPE], p['qk_gain_q'][:MLA_NOPE])
    q_rope = rms_norm(q[..., MLA_NOPE:], p['qk_gain_q'][MLA_NOPE:])
    return q_nope, q_rope


def mla_keys(kv_c, k_r, p):
    B, L, _ = kv_c.shape
    kv = (rms_norm(kv_c, p['mla_kv_norm']) @ p['mla_w_ukv']).reshape(B, L, MLA_HEADS, MLA_NOPE + MLA_V)
    k_nope = rms_norm(kv[..., :MLA_NOPE], p['qk_gain_k'][:MLA_NOPE])
    v = kv[..., MLA_NOPE:]
    k_rope = rms_norm(k_r, p['qk_gain_k'][MLA_NOPE:])
    return k_nope, k_rope, v


def attend(qn, qr, kn, kr, v):
    s = jnp.einsum('bqhd,bkhd->bhqk', qn, kn) + jnp.einsum('bqhd,bkd->bhqk', qr, kr)
    s = s.astype(jnp.float32) * ((MLA_NOPE + MLA_ROPE) ** -0.5)
    prob = jax.nn.softmax(s, axis=-1)
    return jnp.einsum('bhqk,bkhd->bqhd', prob.astype(v.dtype), v)


def latent_attention(qn, qr, kn, kr, v):
    B, L, H, _ = qn.shape
    nb = L // Q_BLOCK

    def blocks(a):
        return jnp.moveaxis(a.reshape((B, nb, Q_BLOCK) + a.shape[2:]), 1, 0)

    out = lax.map(lambda q: attend(q[0], q[1], kn, kr, v), (blocks(qn), blocks(qr)))
    return jnp.moveaxis(out, 0, 1).reshape(B, L, H * MLA_V)


def bidir_shift(z, mu_prev, mu_next):
    zp = jnp.pad(z, ((0, 0), (1, 0), (0, 0)))[:, :-1]
    zn = jnp.pad(z, ((0, 0), (0, 1), (0, 0)))[:, 1:]
    return z + mu_prev * (zp - z) + mu_next * (zn - z)


def rwkv_prepare(z, p):
    B, L, _ = z.shape
    z = bidir_shift(z.astype(jnp.float32), p['rwkv_mu'][0], p['rwkv_mu'][1])
    r, k, v, wd_f, wd_b, ad_f, ad_b, gd = split_cols(z, RWKV_SIZES)

    def heads(t):
        return t.reshape(B, L, RWKV_HEADS, RWKV_HEAD).astype(jnp.float32)

    kk = heads(k * p['rwkv_kk'])
    kk = kk * lax.rsqrt(jnp.maximum(jnp.sum(kk * kk, axis=-1, keepdims=True), 1e-24))
    dirs = []
    for d, (wd, ad) in enumerate(((wd_f, ad_f), (wd_b, ad_b))):
        w_log = -jax.nn.softplus(-(p['rwkv_w0'][d] + jnp.tanh(wd) @ p['rwkv_w2'][d])) - 0.5
        decay = jnp.exp(-jnp.exp(w_log.astype(jnp.float32)))
        a = jax.nn.sigmoid(p['rwkv_a0'][d] + ad @ p['rwkv_a2'][d])
        k_d = k * (1.0 + (a - 1.0) * p['rwkv_ka'][d])
        a_h = heads(a)
        dirs.append((heads(decay), heads(k_d), -kk, kk * a_h))
    return heads(r), heads(v), gd, dirs


def rwkv7_scan(s0, r, decay, k, a_vec, b_vec, v, reverse, emit):
    def step(S, inp):
        r_t, w_t, k_t, a_t, b_t, v_t = inp
        sa = jnp.einsum('bhvk,bhk->bhv', S, a_t)
        S = S * w_t[:, :, None, :] + sa[..., None] * b_t[:, :, None, :] + v_t[..., None] * k_t[:, :, None, :]
        y = jnp.einsum('bhvk,bhk->bhv', S, r_t) if emit else None
        return S, y

    xs = tuple(jnp.moveaxis(t, 1, 0) for t in (r, decay, k, a_vec, b_vec, v))
    S, ys = lax.scan(step, s0, xs, reverse=reverse)
    return S, (jnp.moveaxis(ys, 0, 1) if emit else None)


def rwkv_readout(y, r, v, k_f, k_b, gd, p):
    B, L = y.shape[:2]
    mu = jnp.mean(y, axis=-1, keepdims=True)
    var = jnp.mean(jnp.square(y - mu), axis=-1, keepdims=True)
    yn = ((y - mu) * lax.rsqrt(var + RWKV_GN_EPS)).reshape(B, L, RWKV_WIDTH) * p['rwkv_ln_w'] + p['rwkv_ln_b']
    bonus = jnp.sum(r * (0.5 * (k_f + k_b)) * p['rwkv_rk'], axis=-1, keepdims=True) * v
    g = jax.nn.sigmoid(gd) @ p['rwkv_g2']
    return (yn + bonus.reshape(B, L, RWKV_WIDTH)) * g


def merge_branches(o_pool, o_mla, o_rwkv, gate_cols, p):
    g_pool, g_mla, g_rwkv = jnp.split(jax.nn.sigmoid(gate_cols), N_BRANCH, axis=-1)
    m = (g_pool * (o_pool @ p['w_br_pool'])
         + g_mla * (o_mla @ p['w_br_mla'])
         + g_rwkv * (o_rwkv @ p['w_br_rwkv']))
    return m @ p['w_o']


def mixer_sublayer(h_l, h_c, p, cos, sin, need_ctx):
    B, L, _ = h_l.shape
    Lc = h_c.shape[1]
    pool_l, qc_l, kvc_l, kr_l, rw_l, gate_l = split_cols(h_l @ p['w_in'], IN_SIZES)
    pool_c, qc_c, kvc_c, kr_c, rw_c, gate_c = split_cols(h_c @ p['w_in'], IN_SIZES)

    o_pool_l = pool_mixer(pool_l, p['pool_w'], p['pool_scale'])

    kn_c, krn_c, v_c = mla_keys(kvc_c, kr_c, p)
    kn_l, krn_l, v_l = mla_keys(kvc_l, kr_l, p)
    krn_l = apply_rope(krn_l, cos, sin)
    qn_l, qr_l = mla_queries(qc_l, p)
    qr_l = apply_rope(qr_l, cos[:, None, :], sin[:, None, :])
    o_mla_l = latent_attention(qn_l, qr_l,
                               jnp.concatenate([kn_l, kn_c], axis=1),
                               jnp.concatenate([krn_l, krn_c], axis=1),
                               jnp.concatenate([v_l, v_c], axis=1))

    r_c, vr_c, gd_c, dirs_c = rwkv_prepare(rw_c, p)
    r_l, vr_l, gd_l, dirs_l = rwkv_prepare(rw_l, p)
    s0 = jnp.zeros((B, RWKV_HEADS, RWKV_HEAD, RWKV_HEAD), jnp.float32)
    ys_l = []
    ys_c = []
    for d in range(2):
        rev = d == 1
        s_ctx, yc = rwkv7_scan(s0, r_c, *dirs_c[d], vr_c, rev, need_ctx)
        _, yl = rwkv7_scan(s_ctx, r_l, *dirs_l[d], vr_l, rev, True)
        ys_l.append(yl)
        ys_c.append(yc)
    o_rwkv_l = rwkv_readout(ys_l[0] + ys_l[1], r_l, vr_l, dirs_l[0][1], dirs_l[1][1], gd_l, p)

    out_l = merge_branches(o_pool_l, o_mla_l, o_rwkv_l, gate_l, p)
    if not need_ctx:
        return out_l, None

    o_pool_c = pool_mixer(pool_c, p['pool_w'], p['pool_scale'])
    qn_c, qr_c = mla_queries(qc_c, p)
    o_mla_c = attend(qn_c, qr_c, kn_c, krn_c, v_c).reshape(B, Lc, MLA_WIDTH)
    o_rwkv_c = rwkv_readout(ys_c[0] + ys_c[1], r_c, vr_c, dirs_c[0][1], dirs_c[1][1], gd_c, p)
    out_c = merge_branches(o_pool_c, o_mla_c, o_rwkv_c, gate_c, p)
    return out_l, out_c


def sq_relu_mlp(h, w1, w2):
    return jnp.square(jax.nn.relu(h @ w1)) @ w2


def setup_inputs(seed: int = 0) -> dict:
    key = jax.random.key(seed)
    keys = iter(jax.random.split(key, 48))

    def nrm(shape, scale):
        return scale * jax.random.normal(next(keys), shape, jnp.float32)

    D = D_MODEL
    return {
        'x': nrm((BATCH, SEQ, D), 1.0),
        'c': nrm((BATCH, D), 1.0),
        'ctx': nrm((BATCH, CTX_LEN, D), 1.0),
        'c_ctx': nrm((D,), 1.0),
        'norm1_g': 1.0 + nrm((DEPTH, D), 0.05),
        'norm2_g': 1.0 + nrm((DEPTH, D), 0.05),
        'w_ada': nrm((DEPTH, D, N_MOD * D), D ** -0.5),
        'b_ada': nrm((DEPTH, N_MOD * D), 0.02),
        'w_in': nrm((DEPTH, D, IN_COLS), D ** -0.5),
        'pool_w': nrm((DEPTH, len(POOL_WINDOWS), POOL_GROUP, POOL_GROUP), POOL_GROUP ** -0.5),
        'pool_scale': 1.0 + nrm((DEPTH, POOL_WIDTH), 0.1),
        'mla_q_norm': 1.0 + nrm((DEPTH, MLA_Q_RANK), 0.05),
        'mla_w_uq': nrm((DEPTH, MLA_Q_RANK, MLA_HEADS * (MLA_NOPE + MLA_ROPE)), MLA_Q_RANK ** -0.5),
        'mla_kv_norm': 1.0 + nrm((DEPTH, MLA_KV_RANK), 0.05),
        'mla_w_ukv': nrm((DEPTH, MLA_KV_RANK, MLA_HEADS * (MLA_NOPE + MLA_V)), MLA_KV_RANK ** -0.5),
        'qk_gain_q': 1.0 + nrm((DEPTH, MLA_NOPE + MLA_ROPE), 0.05),
        'qk_gain_k': 1.0 + nrm((DEPTH, MLA_NOPE + MLA_ROPE), 0.05),
        'rwkv_mu': jax.random.uniform(next(keys), (DEPTH, 2, RWKV_IN), jnp.float32, 0.0, 0.5),
        'rwkv_w0': jnp.linspace(-6.0, -1.0, RWKV_WIDTH, dtype=jnp.float32) + nrm((DEPTH, 2, RWKV_WIDTH), 0.3),
        'rwkv_w2': nrm((DEPTH, 2, DECAY_RANK, RWKV_WIDTH), 0.1 * DECAY_RANK ** -0.5),
        'rwkv_a0': nrm((DEPTH, 2, RWKV_WIDTH), 0.1),
        'rwkv_a2': nrm((DEPTH, 2, AAA_RANK, RWKV_WIDTH), 0.5 * AAA_RANK ** -0.5),
        'rwkv_ka': 1.0 + nrm((DEPTH, 2, RWKV_WIDTH), 0.05),
        'rwkv_kk': 0.85 + nrm((DEPTH, RWKV_WIDTH), 0.05),
        'rwkv_rk': nrm((DEPTH, RWKV_HEADS, RWKV_HEAD), 0.1),
        'rwkv_g2': nrm((DEPTH, GATE_RANK, RWKV_WIDTH), GATE_RANK ** -0.5),
        'rwkv_ln_w': 1.0 + nrm((DEPTH, RWKV_WIDTH), 0.05),
        'rwkv_ln_b': nrm((DEPTH, RWKV_WIDTH), 0.02),
        'w_br_pool': nrm((DEPTH, POOL_WIDTH, D), POOL_WIDTH ** -0.5),
        'w_br_mla': nrm((DEPTH, MLA_WIDTH, D), MLA_WIDTH ** -0.5),
        'w_br_rwkv': nrm((DEPTH, RWKV_WIDTH, D), RWKV_WIDTH ** -0.5),
        'w_o': nrm((DEPTH, D, D), D ** -0.5),
        'mlp_w1': nrm((DEPTH, D, D_FF), D ** -0.5),
        'mlp_w2': nrm((DEPTH, D_FF, D), D_FF ** -0.5),
    }


def reference(x, c, ctx, c_ctx, norm1_g, norm2_g, w_ada, b_ada, w_in, pool_w, pool_scale,
              mla_q_norm, mla_w_uq, mla_kv_norm, mla_w_ukv, qk_gain_q, qk_gain_k,
              rwkv_mu, rwkv_w0, rwkv_w2, rwkv_a0, rwkv_a2, rwkv_ka, rwkv_kk, rwkv_rk, rwkv_g2,
              rwkv_ln_w, rwkv_ln_b, w_br_pool, w_br_mla, w_br_rwkv, w_o, mlp_w1, mlp_w2):
    B, L, D = x.shape
    cos, sin = axial_rope_tables(L)
    silu_c = jax.nn.silu(c)
    silu_cc = jax.nn.silu(c_ctx)
    for i in range(DEPTH):
        need_ctx = i < DEPTH - 1
        p = {
            'w_in': w_in[i], 'pool_w': pool_w[i], 'pool_scale': pool_scale[i],
            'mla_q_norm': mla_q_norm[i], 'mla_w_uq': mla_w_uq[i],
            'mla_kv_norm': mla_kv_norm[i], 'mla_w_ukv': mla_w_ukv[i],
            'qk_gain_q': qk_gain_q[i], 'qk_gain_k': qk_gain_k[i],
            'rwkv_mu': rwkv_mu[i], 'rwkv_w0': rwkv_w0[i], 'rwkv_w2': rwkv_w2[i],
            'rwkv_a0': rwkv_a0[i], 'rwkv_a2': rwkv_a2[i], 'rwkv_ka': rwkv_ka[i],
            'rwkv_kk': rwkv_kk[i], 'rwkv_rk': rwkv_rk[i], 'rwkv_g2': rwkv_g2[i],
            'rwkv_ln_w': rwkv_ln_w[i], 'rwkv_ln_b': rwkv_ln_b[i],
            'w_br_pool': w_br_pool[i], 'w_br_mla': w_br_mla[i], 'w_br_rwkv': w_br_rwkv[i],
            'w_o': w_o[i],
        }
        mod_l = (silu_c @ w_ada[i] + b_ada[i]).reshape(B, 1, N_MOD, D)
        sh1, sc1, g1, sh2, sc2, g2 = (mod_l[:, :, j] for j in range(N_MOD))
        mod_c = (silu_cc @ w_ada[i] + b_ada[i]).reshape(N_MOD, D)
        csh1, csc1, cg1, csh2, csc2, cg2 = (mod_c[j] for j in range(N_MOD))

        h_l = modulate(rms_norm(x, norm1_g[i]), sh1, sc1)
        h_c = modulate(rms_norm(ctx, norm1_g[i]), csh1, csc1)
        o_l, o_c = mixer_sublayer(h_l, h_c, p, cos, sin, need_ctx)
        x = x + g1 * o_l
        x = x + g2 * sq_relu_mlp(modulate(rms_norm(x, norm2_g[i]), sh2, sc2), mlp_w1[i], mlp_w2[i])
        if need_ctx:
            ctx = ctx + cg1 * o_c
            ctx = ctx + cg2 * sq_relu_mlp(modulate(rms_norm(ctx, norm2_g[i]), csh2, csc2), mlp_w1[i], mlp_w2[i])
    return x
```

```python
import functools
import math

import jax
import jax.numpy as jnp
from jax import lax
from jax.experimental import pallas as pl
from jax.experimental.pallas import tpu as pltpu

F32 = jnp.float32
BF16 = jnp.bfloat16

NORM_EPS = 1e-6
RWKV_GN_EPS = 64e-5
GRID_W = 64
ROPE_BASE = 10000.0
POOL_HALF_WINDOWS = (1, 2, 4, 8)
N_MOD = 6
MLA_HEADS = 8
MLA_NOPE = 64
MLA_ROPE = 32
MLA_V = 64
RWKV_HEAD = 64
DECAY_RANK = 64
AAA_RANK = 64
GATE_RANK = 128

LANES = 128
SUBLANES = 8
HEAD_SLOT = LANES
HALO = SUBLANES
VMEM_LIMIT = 56 * 1024 * 1024

LOG2E = 1.4426950408889634


def _dot(a, b):
    return jnp.dot(a, b, preferred_element_type=F32)


def _sigmoid(x):
    return 1.0 / (1.0 + jnp.exp(-x))


def _rms(x, width):
    return lax.rsqrt(jnp.sum(x * x, axis=-1, keepdims=True) * (1.0 / width) + NORM_EPS)


def _norm_mod(x, g, shift, scale):
    return (x * _rms(x, x.shape[-1]) * g) * (1.0 + scale) + shift


def _lane(shape):
    return lax.broadcasted_iota(jnp.int32, shape, 1)


def _seg64_sum(x):
    cols = []
    for c in range(x.shape[1] // LANES):
        xc = x[:, c * LANES:(c + 1) * LANES]
        lo_m = _lane(xc.shape) < 64
        lo = jnp.sum(jnp.where(lo_m, xc, 0.0), axis=-1, keepdims=True)
        hi = jnp.sum(jnp.where(lo_m, 0.0, xc), axis=-1, keepdims=True)
        cols.append(jnp.where(lo_m, lo, hi))
    return cols[0] if len(cols) == 1 else jnp.concatenate(cols, axis=1)


def _params(sem):
    return pltpu.CompilerParams(dimension_semantics=sem, vmem_limit_bytes=VMEM_LIMIT)


class _Tiles:
    def __init__(self, batch, lc, l, d):
        self.batch, self.lc, self.l, self.d = batch, lc, l, d
        self.s = lc + l
        self.tm = 256 if (lc % 256 == 0 and l % 256 == 0) else 128
        assert lc % self.tm == 0 and l % self.tm == 0
        self.nct = lc // self.tm
        self.ns = self.s // self.tm
        self.ctx_row = batch

    def grid(self, off):
        return (self.batch, self.ns - off)

    def tok(self, width, off):
        return pl.BlockSpec((None, self.tm, width), lambda b, s: (b, s + off, 0))

    def halo_prev(self, width, off):
        r = self.tm // HALO
        return pl.BlockSpec((None, HALO, width), lambda b, s: (b, jnp.maximum((s + off) * r - 1, 0), 0))

    def halo_next(self, width, off):
        r = self.tm // HALO
        last = self.s // HALO - 1
        return pl.BlockSpec((None, HALO, width), lambda b, s: (b, jnp.minimum((s + off + 1) * r, last), 0))

    def mod(self, j, off):
        nct, ctx_row = self.nct, self.ctx_row
        return pl.BlockSpec((None, None, 1, self.d),
                            lambda b, s: (jnp.where(s + off < nct, ctx_row, b), j, 0, 0))

    def const(self, shape):
        nd = len(shape)
        return pl.BlockSpec(shape, lambda b, s: (0,) * nd)


def _ada_kernel(c_ref, w_ref, b_ref, o_ref):
    c = c_ref[...]
    s = (c * _sigmoid(c)).astype(BF16)
    o_ref[...] = _dot(s, w_ref[...].astype(BF16)) + b_ref[...]


def _ada_mod(c_all, w_ada, b_ada):
    depth, d, n = w_ada.shape
    rows = c_all.shape[0]
    tn = 1024
    return pl.pallas_call(
        _ada_kernel,
        grid=(depth, n // tn),
        in_specs=[pl.BlockSpec((rows, d), lambda i, j: (0, 0)),
                  pl.BlockSpec((None, d, tn), lambda i, j: (i, 0, j)),
                  pl.BlockSpec((None, 1, tn), lambda i, j: (i, 0, j))],
        out_specs=pl.BlockSpec((None, rows, tn), lambda i, j: (i, 0, j)),
        out_shape=jax.ShapeDtypeStruct((depth, rows, n), F32),
        compiler_params=_params(("parallel", "parallel")),
        name="ada_mod",
    )(c_all, w_ada, b_ada.reshape(depth, 1, n))


def _in_proj_kernel(x_ref, g_ref, sh_ref, sc_ref, wp_ref, wq_ref, wkv_ref, wrw_ref, wg_ref,
                    zp_ref, zq_ref, zkv_ref, zrw_ref, gate_ref):
    h = _norm_mod(x_ref[...], g_ref[...], sh_ref[...], sc_ref[...]).astype(BF16)
    zp_ref[...] = _dot(h, wp_ref[...])
    zq_ref[...] = _dot(h, wq_ref[...])
    zkv_ref[...] = _dot(h, wkv_ref[...])
    zrw_ref[...] = _dot(h, wrw_ref[...])
    d = x_ref.shape[-1]
    for c in range(wg_ref.shape[1] // d):
        gate_ref[:, c * d:(c + 1) * d] = _sigmoid(_dot(h, wg_ref[:, c * d:(c + 1) * d])).astype(BF16)


def _in_proj(tl, xc, g, mod, wp, wq, wkv, wrw, wg):
    b, s, d = xc.shape
    widths = (wp.shape[1], wq.shape[1], wkv.shape[1], wrw.shape[1], wg.shape[1])
    dts = (F32, F32, F32, F32, BF16)
    return pl.pallas_call(
        _in_proj_kernel,
        grid=tl.grid(0),
        in_specs=[tl.tok(d, 0), tl.const((1, d)), tl.mod(0, 0), tl.mod(1, 0)]
        + [tl.const(w.shape) for w in (wp, wq, wkv, wrw, wg)],
        out_specs=[tl.tok(w, 0) for w in widths],
        out_shape=[jax.ShapeDtypeStruct((b, s, w), dt) for w, dt in zip(widths, dts)],
        compiler_params=_params(("parallel", "parallel")),
        name="in_proj",
    )(xc, g, mod, mod, wp, wq, wkv, wrw, wg)


def _seq_flags(tl, off):
    s_abs = pl.program_id(1) + off
    is_ctx = s_abs < tl.nct
    first = jnp.logical_or(s_abs == 0, s_abs == tl.nct)
    last = jnp.logical_or(s_abs == tl.nct - 1, s_abs == tl.ns - 1)
    seq_len = jnp.where(is_ctx, tl.lc, tl.l)
    tile_in_seq = jnp.where(is_ctx, s_abs, s_abs - tl.nct)
    return first, last, seq_len, tile_in_seq


def _pool_kernel(u_ref, up_ref, un_ref, pw_ref, ps_ref, o_ref, *, tl, off):
    first, last, seq_len, tile_in_seq = _seq_flags(tl, off)
    tm = tl.tm
    u = u_ref[...]
    prev = jnp.where(first, 0.0, up_ref[...])
    nxt = jnp.where(last, 0.0, un_ref[...])
    e = jnp.concatenate([prev, u, nxt], axis=0)
    n = tm + 2 * HALO
    w2 = e + pltpu.roll(e, 1, 0)
    w4 = pltpu.roll(w2, n - 1, 0) + pltpu.roll(w2, 1, 0)
    w8 = pltpu.roll(w4, n - 2, 0) + pltpu.roll(w4, 2, 0)
    w16 = pltpu.roll(w8, n - 4, 0) + pltpu.roll(w8, 4, 0)
    sums = [w[HALO:HALO + tm] for w in (w2, w4, w8, w16)]
    width = u.shape[1]
    group = width // len(POOL_HALF_WINDOWS)
    lane = _lane((tm, width))
    pos = tile_in_seq * tm + lax.broadcasted_iota(jnp.int32, (tm, width), 0)
    total = sums[-1]
    half = jnp.full((tm, width), POOL_HALF_WINDOWS[-1], jnp.int32)
    for gi in range(len(POOL_HALF_WINDOWS) - 2, -1, -1):
        sel = lane < (gi + 1) * group
        total = jnp.where(sel, sums[gi], total)
        half = jnp.where(sel, POOL_HALF_WINDOWS[gi], half)
    cnt = jnp.minimum(pos + half, seq_len) - jnp.maximum(pos - half, 0)
    pooled = total / cnt.astype(F32) - u
    o_ref[...] = (_dot(pooled.astype(BF16), pw_ref[...]) * ps_ref[...]).astype(BF16)


def _pool(tl, zp, pw_bd, ps, off):
    b, s, w = zp.shape
    return pl.pallas_call(
        functools.partial(_pool_kernel, tl=tl, off=off),
        grid=tl.grid(off),
        in_specs=[tl.tok(w, off), tl.halo_prev(w, off), tl.halo_next(w, off),
                  tl.const(pw_bd.shape), tl.const(ps.shape)],
        out_specs=tl.tok(w, off),
        out_shape=jax.ShapeDtypeStruct((b, s, w), BF16),
        compiler_params=_params(("parallel", "parallel")),
        name="pool_mixer",
    )(zp, zp, zp, pw_bd, ps)


def _rope(x, tc, ts1, ts2):
    return x * tc + pltpu.roll(x, LANES - MLA_ROPE // 2, 1) * ts1 + pltpu.roll(x, MLA_ROPE // 2, 1) * ts2


def _qkv_kernel(zq_ref, zkv_ref, tc_ref, ts1_ref, ts2_ref, qng_ref, kvng_ref, gq_ref, gk_ref, gkr_ref,
                wuq_ref, wuk_ref, wuv_ref, q_ref, k_ref, v_ref):
    tc, ts1, ts2 = tc_ref[...], ts1_ref[...], ts2_ref[...]
    lane = _lane(tc.shape)
    nope_m = lane < MLA_NOPE
    rope_m = jnp.logical_and(lane >= MLA_NOPE, lane < MLA_NOPE + MLA_ROPE)

    zq = zq_ref[...]
    qc = (zq * _rms(zq, zq.shape[-1]) * qng_ref[...]).astype(BF16)
    q = _dot(qc, wuq_ref[...])
    q_scale = LOG2E * (MLA_NOPE + MLA_ROPE) ** -0.5
    for h in range(MLA_HEADS):
        qh = q[:, h * HEAD_SLOT:(h + 1) * HEAD_SLOT]
        sq = qh * qh
        ss_n = jnp.sum(jnp.where(nope_m, sq, 0.0), axis=-1, keepdims=True) * (1.0 / MLA_NOPE)
        ss_r = jnp.sum(jnp.where(rope_m, sq, 0.0), axis=-1, keepdims=True) * (1.0 / MLA_ROPE)
        inv = jnp.where(nope_m, lax.rsqrt(ss_n + NORM_EPS), lax.rsqrt(ss_r + NORM_EPS))
        qh = _rope(qh * inv * gq_ref[...], tc, ts1, ts2)
        q_ref[:, h * HEAD_SLOT:(h + 1) * HEAD_SLOT] = (qh * q_scale).astype(BF16)

    zkv = zkv_ref[...]
    kv_w = kvng_ref.shape[-1]
    kvc = zkv[:, :kv_w]
    kvn = (kvc * _rms(kvc, kv_w) * kvng_ref[...]).astype(BF16)
    kr = zkv[:, kv_w:kv_w + LANES]
    kr = kr * lax.rsqrt(jnp.sum(kr * kr, axis=-1, keepdims=True) * (1.0 / MLA_ROPE) + NORM_EPS) * gkr_ref[...]
    kr = _rope(pltpu.roll(kr, MLA_NOPE, 1), tc, ts1, ts2)
    kn = _dot(kvn, wuk_ref[...])
    for h in range(MLA_HEADS):
        kh = kn[:, h * HEAD_SLOT:(h + 1) * HEAD_SLOT]
        inv = lax.rsqrt(jnp.sum(kh * kh, axis=-1, keepdims=True) * (1.0 / MLA_NOPE) + NORM_EPS)
        k_ref[:, h * HEAD_SLOT:(h + 1) * HEAD_SLOT] = (kh * inv * gk_ref[...] + kr).astype(BF16)
    v_ref[...] = _dot(kvn, wuv_ref[...]).astype(BF16)


def _qkv_up(tl, zq, zkv, tabs, smalls, wuq, wuk, wuv):
    b, s, _ = zq.shape
    tc, ts1, ts2 = tabs
    tab_spec = pl.BlockSpec((tl.tm, LANES), lambda bb, ss: (ss, 0))
    widths = (wuq.shape[1], wuk.shape[1], wuv.shape[1])
    return pl.pallas_call(
        _qkv_kernel,
        grid=tl.grid(0),
        in_specs=[tl.tok(zq.shape[-1], 0), tl.tok(zkv.shape[-1], 0), tab_spec, tab_spec, tab_spec]
        + [tl.const(a.shape) for a in smalls] + [tl.const(w.shape) for w in (wuq, wuk, wuv)],
        out_specs=[tl.tok(w, 0) for w in widths],
        out_shape=[jax.ShapeDtypeStruct((b, s, w), BF16) for w in widths],
        compiler_params=_params(("parallel", "parallel")),
        name="qkv_up",
    )(zq, zkv, tc, ts1, ts2, *smalls, wuq, wuk, wuv)


def _attn_kernel(q_ref, k_ref, v_ref, o_ref):
    v = v_ref[...]
    outs = []
    for hh in range(2):
        q = q_ref[:, hh * HEAD_SLOT:(hh + 1) * HEAD_SLOT]
        k = k_ref[:, hh * HEAD_SLOT:(hh + 1) * HEAD_SLOT]
        s = lax.dot_general(q, k, (((1,), (1,)), ((), ())), preferred_element_type=F32)
        p = jnp.exp2(s - jnp.max(s, axis=-1, keepdims=True))
        denom = jnp.sum(p, axis=-1, keepdims=True)
        outs.append(_dot(p.astype(BF16), v) / denom)
    o_ref[...] = jnp.where(_lane(outs[0].shape) < MLA_V, outs[0], outs[1]).astype(BF16)


def _attention(q, k, v, *, tq, q_off_tiles, n_q_tiles, n_keys):
    b, _, qw = q.shape
    pairs = qw // (2 * HEAD_SLOT)
    return pl.pallas_call(
        _attn_kernel,
        grid=(b, pairs, n_q_tiles),
        in_specs=[pl.BlockSpec((None, tq, 2 * HEAD_SLOT), lambda bb, hp, i: (bb, i + q_off_tiles, hp)),
                  pl.BlockSpec((None, n_keys, 2 * HEAD_SLOT), lambda bb, hp, i: (bb, 0, hp)),
                  pl.BlockSpec((None, n_keys, 2 * MLA_V), lambda bb, hp, i: (bb, 0, hp))],
        out_specs=pl.BlockSpec((None, tq, 2 * MLA_V), lambda bb, hp, i: (bb, i, hp)),
        out_shape=jax.ShapeDtypeStruct((b, n_q_tiles * tq, pairs * 2 * MLA_V), BF16),
        compiler_params=_params(("parallel", "parallel", "arbitrary")),
        name="attention",
    )(q, k, v)


def _rwkv_prep_kernel(z_ref, zp_ref, zn_ref, mu_ref, kkw_ref, w0_ref, w2_ref, a0_ref, a2_ref, ka_ref,
                      rk_ref, g2_ref,
                      r_o, v_o, nkk_o, gg_o, bonus_o, wf_o, kf_o, bf_o, wb_o, kb_o, bb_o, *, tl, off):
    first, last, _, _ = _seq_flags(tl, off)
    tm = tl.tm
    z = z_ref[...]
    row = lax.broadcasted_iota(jnp.int32, z.shape, 0)
    prev_row = jnp.where(first, 0.0, zp_ref[HALO - 1:HALO, :])
    next_row = jnp.where(last, 0.0, zn_ref[0:1, :])
    z_prev = jnp.where(row == 0, prev_row, pltpu.roll(z, 1, 0))
    z_next = jnp.where(row == tm - 1, next_row, pltpu.roll(z, tm - 1, 0))
    zs = z + mu_ref[0:1, :] * (z_prev - z) + mu_ref[1:2, :] * (z_next - z)

    w = kkw_ref.shape[-1]
    r, k, v = zs[:, 0:w], zs[:, w:2 * w], zs[:, 2 * w:3 * w]
    o = 3 * w
    wd = zs[:, o:o + 2 * DECAY_RANK]
    ad = zs[:, o + 2 * DECAY_RANK:o + 2 * DECAY_RANK + 2 * AAA_RANK]
    gd = zs[:, o + 2 * DECAY_RANK + 2 * AAA_RANK:]

    kk = k * kkw_ref[...]
    kk = kk * lax.rsqrt(jnp.maximum(_seg64_sum(kk * kk), 1e-24))
    u = w0_ref[...] + _dot(jnp.tanh(wd).astype(BF16), w2_ref[...])
    decay = jnp.exp(-math.exp(-0.5) * _sigmoid(u))
    a = _sigmoid(a0_ref[...] + _dot(ad.astype(BF16), a2_ref[...]))
    ka = ka_ref[...]
    k_sum = jnp.zeros_like(k)
    for d, (w_o, k_o, b_o) in enumerate(((wf_o, kf_o, bf_o), (wb_o, kb_o, bb_o))):
        a_d = a[:, d * w:(d + 1) * w]
        k_d = k * (1.0 + (a_d - 1.0) * ka[:, d * w:(d + 1) * w])
        w_o[...] = decay[:, d * w:(d + 1) * w]
        k_o[...] = k_d
        b_o[...] = kk * a_d
        k_sum = k_sum + k_d
    r_o[...] = r
    v_o[...] = v
    nkk_o[...] = -kk
    gg_o[...] = _dot(_sigmoid(gd).astype(BF16), g2_ref[...])
    bonus_o[...] = _seg64_sum(r * (0.5 * k_sum) * rk_ref[...]) * v


def _rwkv_prep(tl, zrw, smalls):
    b, s, win = zrw.shape
    w = smalls[1].shape[-1]
    n_out = 11
    return pl.pallas_call(
        functools.partial(_rwkv_prep_kernel, tl=tl, off=0),
        grid=tl.grid(0),
        in_specs=[tl.tok(win, 0), tl.halo_prev(win, 0), tl.halo_next(win, 0)]
        + [tl.const(a.shape) for a in smalls],
        out_specs=[tl.tok(w, 0)] * n_out,
        out_shape=[jax.ShapeDtypeStruct((b, s, w), F32)] * n_out,
        compiler_params=_params(("parallel", "parallel")),
        name="rwkv_prep",
    )(zrw, zrw, zrw, *smalls)


def _scan_kernel(a_ref, w_ref, b_ref, k_ref, r_ref, v_ref, y_ref, s_ref):
    @pl.when(pl.program_id(1) == 0)
    def _():
        s_ref[...] = jnp.zeros_like(s_ref)

    n_k = s_ref.shape[0]
    steps = v_ref.shape[0]

    def step(t, carry):
        v = v_ref[t]
        acc = [jnp.zeros_like(v), jnp.zeros_like(v)]
        for kk in range(n_k):
            acc[kk % 2] = acc[kk % 2] + s_ref[kk] * a_ref[t, pl.ds(kk, 1), :]
        sa = acc[0] + acc[1]
        yacc = [jnp.zeros_like(v), jnp.zeros_like(v)]
        for kk in range(n_k):
            sn = (s_ref[kk] * w_ref[t, pl.ds(kk, 1), :] + sa * b_ref[t, pl.ds(kk, 1), :]
                  + v * k_ref[t, pl.ds(kk, 1), :])
            s_ref[kk] = sn
            yacc[kk % 2] = yacc[kk % 2] + sn * r_ref[t, pl.ds(kk, 1), :]
        y_ref[t] = yacc[0] + yacc[1]
        return carry

    lax.fori_loop(0, steps, step, 0)


def _scan(ops_k, op_v, chunk):
    steps, n_k, nl = ops_k[0].shape
    nv = op_v.shape[1]
    kspec = pl.BlockSpec((chunk, n_k, LANES), lambda lb, c: (c, 0, lb))
    vspec = pl.BlockSpec((chunk, nv, LANES), lambda lb, c: (c, 0, lb))
    return pl.pallas_call(
        _scan_kernel,
        grid=(nl // LANES, steps // chunk),
        in_specs=[kspec] * 5 + [vspec],
        out_specs=vspec,
        out_shape=jax.ShapeDtypeStruct((steps, nv, nl), F32),
        scratch_shapes=[pltpu.VMEM((n_k, nv, LANES), F32)],
        compiler_params=_params(("parallel", "arbitrary")),
        name="rwkv_scan",
    )(*ops_k, op_v)


def _flip_segments(x, lc):
    return jnp.concatenate([x[:, :lc][:, ::-1], x[:, lc:][:, ::-1]], axis=1)


def _rwkv_scan(tl, r, v, nkk, dirs):
    b, s, width = r.shape
    heads = width // RWKV_HEAD
    half = RWKV_HEAD // 2
    nl = 2 * 2 * b * heads
    nl_pad = -(-nl // LANES) * LANES

    def pad(x):
        return x if nl_pad == nl else jnp.pad(x, ((0, 0), (0, 0), (0, nl_pad - nl)))

    def k_layout(x_f, x_b):
        z = jnp.stack([x_f, _flip_segments(x_b, tl.lc)]).reshape(2, b, s, heads, RWKV_HEAD)
        z = z.transpose(2, 4, 0, 1, 3).reshape(s, RWKV_HEAD, 2 * b * heads)
        return pad(jnp.concatenate([z, z], axis=-1))

    def v_layout(x):
        z = jnp.stack([x, _flip_segments(x, tl.lc)]).reshape(2, b, s, heads, 2, half)
        return pad(z.transpose(2, 5, 4, 0, 1, 3).reshape(s, half, nl))

    (w_f, k_f, b_f), (w_b, k_b, b_b) = dirs
    ops_k = [k_layout(nkk, nkk), k_layout(w_f, w_b), k_layout(b_f, b_b), k_layout(k_f, k_b), k_layout(r, r)]
    chunk = 32 if s % 32 == 0 else 16
    y = _scan(ops_k, v_layout(v), chunk)[:, :, :nl]
    y = y.reshape(s, half, 2, 2, b, heads).transpose(3, 4, 0, 5, 2, 1).reshape(2, b, s, width)
    return y[0] + _flip_segments(y[1], tl.lc)


def _merge_kernel(y_ref, bonus_ref, gg_ref, op_ref, om_ref, gate_ref, x_ref, g1_ref, lnw_ref, lnb_ref,
                  wbp_ref, wbm_ref, wbr_ref, wo_ref, o_ref):
    y = y_ref[...]
    mu = _seg64_sum(y) * (1.0 / RWKV_HEAD)
    yc = y - mu
    var = _seg64_sum(yc * yc) * (1.0 / RWKV_HEAD)
    yn = yc * lax.rsqrt(var + RWKV_GN_EPS) * lnw_ref[...] + lnb_ref[...]
    o_rw = ((yn + bonus_ref[...]) * gg_ref[...]).astype(BF16)
    d = x_ref.shape[-1]
    m = (gate_ref[:, 0:d].astype(F32) * _dot(op_ref[...], wbp_ref[...])
         + gate_ref[:, d:2 * d].astype(F32) * _dot(om_ref[...], wbm_ref[...])
         + gate_ref[:, 2 * d:3 * d].astype(F32) * _dot(o_rw, wbr_ref[...]))
    o_ref[...] = x_ref[...] + g1_ref[...] * _dot(m.astype(BF16), wo_ref[...])


def _merge(tl, off, y, bonus, gg, o_pool, o_mla, gate, xc, mod, lnw, lnb, wbp, wbm, wbr, wo):
    b, _, d = xc.shape
    n_tiles = tl.ns - off
    out_spec = pl.BlockSpec((None, tl.tm, d), lambda bb, ss: (bb, ss, 0))
    mla_spec = pl.BlockSpec((None, tl.tm, o_mla.shape[-1]), lambda bb, ss: (bb, ss, 0))
    rw = y.shape[-1]
    return pl.pallas_call(
        _merge_kernel,
        grid=tl.grid(off),
        in_specs=[tl.tok(rw, off), tl.tok(rw, off), tl.tok(rw, off), tl.tok(o_pool.shape[-1], off), mla_spec,
                  tl.tok(gate.shape[-1], off), tl.tok(d, off), tl.mod(2, off),
                  tl.const(lnw.shape), tl.const(lnb.shape)] + [tl.const(w.shape) for w in (wbp, wbm, wbr, wo)],
        out_specs=out_spec,
        out_shape=jax.ShapeDtypeStruct((b, n_tiles * tl.tm, d), F32),
        compiler_params=_params(("parallel", "parallel")),
        name="merge",
    )(y, bonus, gg, o_pool, o_mla, gate, xc, mod, lnw, lnb, wbp, wbm, wbr, wo)


def _mlp_kernel(x_ref, g_ref, sh_ref, sc_ref, g2_ref, w1_ref, w2_ref, o_ref, h_ref, acc_ref):
    j = pl.program_id(2)

    @pl.when(j == 0)
    def _():
        h_ref[...] = _norm_mod(x_ref[...], g_ref[...], sh_ref[...], sc_ref[...]).astype(BF16)
        acc_ref[...] = jnp.zeros_like(acc_ref)

    a = jnp.maximum(_dot(h_ref[...], w1_ref[...]), 0.0)
    acc_ref[...] += _dot((a * a).astype(BF16), w2_ref[...])

    @pl.when(j == pl.num_programs(2) - 1)
    def _():
        o_ref[...] = x_ref[...] + g2_ref[...] * acc_ref[...]


def _mlp(tl, x1, in_off, mod_off, g, mod, w1, w2):
    b, s1, d = x1.shape
    dff = w1.shape[1]
    tf = 1024 if dff % 1024 == 0 else dff
    n_tiles = s1 // tl.tm - in_off
    nct, ctx_row = tl.nct, tl.ctx_row

    def mod_spec(j):
        return pl.BlockSpec((None, None, 1, d),
                            lambda bb, ss, jj: (jnp.where(ss + mod_off < nct, ctx_row, bb), j, 0, 0))

    x_spec = pl.BlockSpec((None, tl.tm, d), lambda bb, ss, jj: (bb, ss + in_off, 0))
    return pl.pallas_call(
        _mlp_kernel,
        grid=(b, n_tiles, dff // tf),
        in_specs=[x_spec, pl.BlockSpec((1, d), lambda bb, ss, jj: (0, 0)), mod_spec(3), mod_spec(4), mod_spec(5),
                  pl.BlockSpec((d, tf), lambda bb, ss, jj: (0, jj)),
                  pl.BlockSpec((tf, d), lambda bb, ss, jj: (jj, 0))],
        out_specs=pl.BlockSpec((None, tl.tm, d), lambda bb, ss, jj: (bb, ss, 0)),
        out_shape=jax.ShapeDtypeStruct((b, n_tiles * tl.tm, d), F32),
        scratch_shapes=[pltpu.VMEM((tl.tm, d), BF16), pltpu.VMEM((tl.tm, d), F32)],
        compiler_params=_params(("parallel", "parallel", "arbitrary")),
        name="mlp",
    )(x1, g, mod, mod, mod, w1, w2)


def _rope_tables(lc, l):
    rows = l // GRID_W
    row = jnp.repeat(jnp.arange(rows), GRID_W).astype(F32)
    col = jnp.tile(jnp.arange(GRID_W), rows).astype(F32)
    n_freq = MLA_ROPE // 4
    inv_freq = jnp.power(ROPE_BASE, -jnp.arange(n_freq, dtype=F32) / n_freq)
    ang = jnp.concatenate([row[:, None] * inv_freq, col[:, None] * inv_freq], axis=-1)
    cos = jnp.concatenate([jnp.ones((lc, MLA_ROPE // 2), F32), jnp.cos(ang)], axis=0)
    sin = jnp.concatenate([jnp.zeros((lc, MLA_ROPE // 2), F32), jnp.sin(ang)], axis=0)
    s = lc + l
    pad = jnp.zeros((s, LANES - MLA_NOPE - MLA_ROPE), F32)
    z16 = jnp.zeros((s, MLA_ROPE // 2), F32)
    zn = jnp.zeros((s, MLA_NOPE), F32)
    tc = jnp.concatenate([jnp.ones((s, MLA_NOPE), F32), cos, cos, pad], axis=1)
    ts1 = jnp.concatenate([zn, -sin, z16, pad], axis=1)
    ts2 = jnp.concatenate([zn, z16, sin, pad], axis=1)
    return tc, ts1, ts2


def _slot_cols(w, per_head):
    k = w.shape[0]
    w = w.reshape(k, MLA_HEADS, per_head)
    return jnp.pad(w, ((0, 0), (0, 0), (0, HEAD_SLOT - per_head))).reshape(k, MLA_HEADS * HEAD_SLOT)


def _block_diag(blocks):
    n = len(blocks)
    rows = []
    for i, blk in enumerate(blocks):
        rows.append(jnp.concatenate(
            [blk if j == i else jnp.zeros((blk.shape[0], blocks[j].shape[1]), blk.dtype) for j in range(n)], axis=1))
    return jnp.concatenate(rows, axis=0)


def _row(x):
    return x.reshape(1, -1).astype(F32)


def kernel(x, c, ctx, c_ctx, norm1_g, norm2_g, w_ada, b_ada, w_in, pool_w, pool_scale, mla_q_norm, mla_w_uq, mla_kv_norm, mla_w_ukv, qk_gain_q, qk_gain_k, rwkv_mu, rwkv_w0, rwkv_w2, rwkv_a0, rwkv_a2, rwkv_ka, rwkv_kk, rwkv_rk, rwkv_g2, rwkv_ln_w, rwkv_ln_b, w_br_pool, w_br_mla, w_br_rwkv, w_o, mlp_w1, mlp_w2):
    b, l, d = x.shape
    lc = ctx.shape[1]
    depth = w_in.shape[0]
    tl = _Tiles(b, lc, l, d)

    pool_width = pool_scale.shape[-1]
    q_rank = mla_q_norm.shape[-1]
    kv_rank = mla_kv_norm.shape[-1]
    rw_width = rwkv_kk.shape[-1]
    rw_in = rwkv_mu.shape[-1]

    rows = -(-(b + 1) // SUBLANES) * SUBLANES
    c_all = jnp.concatenate([c, c_ctx[None, :], jnp.zeros((rows - b - 1, d), F32)], axis=0)
    mod_all = _ada_mod(c_all, w_ada, b_ada).reshape(depth, rows, N_MOD, 1, d)
    tabs = _rope_tables(lc, l)

    xc = jnp.concatenate([ctx, x], axis=1)
    out = None
    for i in range(depth):
        need_ctx = i < depth - 1
        off = 0 if need_ctx else tl.nct
        mod = mod_all[i]

        wi = w_in[i].astype(BF16)
        o0 = 0
        wp = wi[:, o0:o0 + pool_width]; o0 += pool_width
        wq = wi[:, o0:o0 + q_rank]; o0 += q_rank
        wkv = wi[:, o0:o0 + kv_rank + MLA_ROPE]; o0 += kv_rank + MLA_ROPE
        wkv = jnp.pad(wkv, ((0, 0), (0, LANES - MLA_ROPE)))
        wrw = wi[:, o0:o0 + rw_in]; o0 += rw_in
        wg = wi[:, o0:]

        zp, zq, zkv, zrw, gate = _in_proj(tl, xc, _row(norm1_g[i]), mod, wp, wq, wkv, wrw, wg)

        pw_bd = _block_diag([pool_w[i, g] for g in range(pool_w.shape[1])]).astype(BF16)
        o_pool = _pool(tl, zp, pw_bd, _row(pool_scale[i]), off)

        wuq = _slot_cols(mla_w_uq[i], MLA_NOPE + MLA_ROPE).astype(BF16)
        wukv = mla_w_ukv[i].reshape(kv_rank, MLA_HEADS, MLA_NOPE + MLA_V)
        wuk = _slot_cols(wukv[:, :, :MLA_NOPE].reshape(kv_rank, -1), MLA_NOPE).astype(BF16)
        wuv = wukv[:, :, MLA_NOPE:].reshape(kv_rank, MLA_HEADS * MLA_V).astype(BF16)
        zpad = jnp.zeros((LANES - MLA_NOPE - MLA_ROPE,), F32)
        gq = _row(jnp.concatenate([qk_gain_q[i], zpad]))
        gk = _row(jnp.concatenate([qk_gain_k[i, :MLA_NOPE], jnp.zeros((LANES - MLA_NOPE,), F32)]))
        gkr = _row(jnp.concatenate([qk_gain_k[i, MLA_NOPE:], jnp.zeros((LANES - MLA_ROPE,), F32)]))
        smalls = (_row(mla_q_norm[i]), _row(mla_kv_norm[i]), gq, gk, gkr)
        q, k, v = _qkv_up(tl, zq, zkv, tabs, smalls, wuq, wuk, wuv)
        o_mla_l = _attention(q, k, v, tq=tl.tm, q_off_tiles=tl.nct, n_q_tiles=tl.ns - tl.nct, n_keys=tl.s)
        if need_ctx:
            o_mla_c = _attention(q, k, v, tq=tl.tm, q_off_tiles=0, n_q_tiles=tl.nct, n_keys=lc)
            o_mla = jnp.concatenate([o_mla_c, o_mla_l], axis=1)
        else:
            o_mla = o_mla_l

        w2cat = _block_diag([rwkv_w2[i, 0], rwkv_w2[i, 1]]).astype(BF16)
        a2cat = _block_diag([rwkv_a2[i, 0], rwkv_a2[i, 1]]).astype(BF16)
        rsmalls = (rwkv_mu[i].astype(F32), _row(rwkv_kk[i]), _row(rwkv_w0[i]), w2cat, _row(rwkv_a0[i]), a2cat,
                   _row(rwkv_ka[i]), _row(rwkv_rk[i]), rwkv_g2[i].astype(BF16))
        r, vv, nkk, gg, bonus, wf, kf, bf, wb, kb, bb = _rwkv_prep(tl, zrw, rsmalls)
        y = _rwkv_scan(tl, r, vv, nkk, ((wf, kf, bf), (wb, kb, bb)))

        x1 = _merge(tl, off, y, bonus, gg, o_pool, o_mla, gate, xc, mod,
                    _row(rwkv_ln_w[i]), _row(rwkv_ln_b[i]),
                    w_br_pool[i].astype(BF16), w_br_mla[i].astype(BF16), w_br_rwkv[i].astype(BF16),
                    w_o[i].astype(BF16))
        xc_next = _mlp(tl, x1, 0, off, _row(norm2_g[i]), mod, mlp_w1[i].astype(BF16), mlp_w2[i].astype(BF16))
        if need_ctx:
            xc = xc_next
        else:
            out = xc_next
    return out
```

```python
import functools
import math

import jax
import jax.numpy as jnp
from jax import lax
from jax.experimental import pallas as pl
from jax.experimental.pallas import tpu as pltpu

F32 = jnp.float32
BF16 = jnp.bfloat16

NORM_EPS = 1e-6
RWKV_GN_EPS = 64e-5
GRID_W = 64
ROPE_BASE = 10000.0
POOL_HALF_WINDOWS = (1, 2, 4, 8)
N_MOD = 6
MLA_HEADS = 8
MLA_NOPE = 64
MLA_ROPE = 32
MLA_V = 64
RWKV_HEAD = 64
DECAY_RANK = 64
AAA_RANK = 64
GATE_RANK = 128

LANES = 128
SUBLANES = 8
HEAD_SLOT = LANES
HALO = SUBLANES
VMEM_LIMIT = 56 * 1024 * 1024

LOG2E = 1.4426950408889634


def _dot(a, b):
    return jnp.dot(a, b, preferred_element_type=F32)


def _sigmoid(x):
    return 1.0 / (1.0 + jnp.exp(-x))


def _rms(x, width):
    return lax.rsqrt(jnp.sum(x * x, axis=-1, keepdims=True) * (1.0 / width) + NORM_EPS)


def _norm_mod(x, g, shift, scale):
    return (x * _rms(x, x.shape[-1]) * g) * (1.0 + scale) + shift


def _lane(shape):
    return lax.broadcasted_iota(jnp.int32, shape, 1)


def _seg64_sum(x):
    cols = []
    for c in range(x.shape[1] // LANES):
        xc = x[:, c * LANES:(c + 1) * LANES]
        lo_m = _lane(xc.shape) < 64
        lo = jnp.sum(jnp.where(lo_m, xc, 0.0), axis=-1, keepdims=True)
        hi = jnp.sum(jnp.where(lo_m, 0.0, xc), axis=-1, keepdims=True)
        cols.append(jnp.where(lo_m, lo, hi))
    return cols[0] if len(cols) == 1 else jnp.concatenate(cols, axis=1)


def _params(sem):
    return pltpu.CompilerParams(dimension_semantics=sem, vmem_limit_bytes=VMEM_LIMIT)


class _Tiles:
    def __init__(self, batch, lc, l, d):
        self.batch, self.lc, self.l, self.d = batch, lc, l, d
        self.s = lc + l
        self.tm = 256 if (lc % 256 == 0 and l % 256 == 0) else 128
        assert lc % self.tm == 0 and l % self.tm == 0
        self.nct = lc // self.tm
        self.ns = self.s // self.tm
        self.ctx_row = batch

    def grid(self, off):
        return (self.batch, self.ns - off)

    def tok(self, width, off):
        return pl.BlockSpec((None, self.tm, width), lambda b, s: (b, s + off, 0))

    def halo_prev(self, width, off):
        r = self.tm // HALO
        return pl.BlockSpec((None, HALO, width), lambda b, s: (b, jnp.maximum((s + off) * r - 1, 0), 0))

    def halo_next(self, width, off):
        r = self.tm // HALO
        last = self.s // HALO - 1
        return pl.BlockSpec((None, HALO, width), lambda b, s: (b, jnp.minimum((s + off + 1) * r, last), 0))

    def mod(self, j, off):
        nct, ctx_row = self.nct, self.ctx_row
        return pl.BlockSpec((None, None, 1, self.d),
                            lambda b, s: (jnp.where(s + off < nct, ctx_row, b), j, 0, 0))

    def const(self, shape):
        nd = len(shape)
        return pl.BlockSpec(shape, lambda b, s: (0,) * nd)


def _ada_kernel(c_ref, w_ref, b_ref, o_ref):
    c = c_ref[...]
    s = (c * _sigmoid(c)).astype(BF16)
    o_ref[...] = _dot(s, w_ref[...].astype(BF16)) + b_ref[...]


def _ada_mod(c_all, w_ada, b_ada):
    depth, d, n = w_ada.shape
    rows = c_all.shape[0]
    tn = 1024
    return pl.pallas_call(
        _ada_kernel,
        grid=(depth, n // tn),
        in_specs=[pl.BlockSpec((rows, d), lambda i, j: (0, 0)),
                  pl.BlockSpec((None, d, tn), lambda i, j: (i, 0, j)),
                  pl.BlockSpec((None, 1, tn), lambda i, j: (i, 0, j))],
        out_specs=pl.BlockSpec((None, rows, tn), lambda i, j: (i, 0, j)),
        out_shape=jax.ShapeDtypeStruct((depth, rows, n), F32),
        compiler_params=_params(("parallel", "parallel")),
        name="ada_mod",
    )(c_all, w_ada, b_ada.reshape(depth, 1, n))


def _in_proj_kernel(x_ref, g_ref, sh_ref, sc_ref, wp_ref, wq_ref, wkv_ref, wrw_ref, wg_ref,
                    zp_ref, zq_ref, zkv_ref, zrw_ref, gate_ref):
    h = _norm_mod(x_ref[...], g_ref[...], sh_ref[...], sc_ref[...]).astype(BF16)
    zp_ref[...] = _dot(h, wp_ref[...])
    zq_ref[...] = _dot(h, wq_ref[...])
    zkv_ref[...] = _dot(h, wkv_ref[...])
    zrw_ref[...] = _dot(h, wrw_ref[...])
    d = x_ref.shape[-1]
    for c in range(wg_ref.shape[1] // d):
        gate_ref[:, c * d:(c + 1) * d] = _sigmoid(_dot(h, wg_ref[:, c * d:(c + 1) * d])).astype(BF16)


def _in_proj(tl, xc, g, mod, wp, wq, wkv, wrw, wg):
    b, s, d = xc.shape
    widths = (wp.shape[1], wq.shape[1], wkv.shape[1], wrw.shape[1], wg.shape[1])
    dts = (F32, F32, F32, F32, BF16)
    return pl.pallas_call(
        _in_proj_kernel,
        grid=tl.grid(0),
        in_specs=[tl.tok(d, 0), tl.const((1, d)), tl.mod(0, 0), tl.mod(1, 0)]
        + [tl.const(w.shape) for w in (wp, wq, wkv, wrw, wg)],
        out_specs=[tl.tok(w, 0) for w in widths],
        out_shape=[jax.ShapeDtypeStruct((b, s, w), dt) for w, dt in zip(widths, dts)],
        compiler_params=_params(("parallel", "parallel")),
        name="in_proj",
    )(xc, g, mod, mod, wp, wq, wkv, wrw, wg)


def _seq_flags(tl, off):
    s_abs = pl.program_id(1) + off
    is_ctx = s_abs < tl.nct
    first = jnp.logical_or(s_abs == 0, s_abs == tl.nct)
    last = jnp.logical_or(s_abs == tl.nct - 1, s_abs == tl.ns - 1)
    seq_len = jnp.where(is_ctx, tl.lc, tl.l)
    tile_in_seq = jnp.where(is_ctx, s_abs, s_abs - tl.nct)
    return first, last, seq_len, tile_in_seq


def _pool_kernel(u_ref, up_ref, un_ref, pw_ref, ps_ref, o_ref, *, tl, off):
    first, last, seq_len, tile_in_seq = _seq_flags(tl, off)
    tm = tl.tm
    u = u_ref[...]
    prev = jnp.where(first, 0.0, up_ref[...])
    nxt = jnp.where(last, 0.0, un_ref[...])
    e = jnp.concatenate([prev, u, nxt], axis=0)
    n = tm + 2 * HALO
    w2 = e + pltpu.roll(e, 1, 0)
    w4 = pltpu.roll(w2, n - 1, 0) + pltpu.roll(w2, 1, 0)
    w8 = pltpu.roll(w4, n - 2, 0) + pltpu.roll(w4, 2, 0)
    w16 = pltpu.roll(w8, n - 4, 0) + pltpu.roll(w8, 4, 0)
    sums = [w[HALO:HALO + tm] for w in (w2, w4, w8, w16)]
    width = u.shape[1]
    group = width // len(POOL_HALF_WINDOWS)
    lane = _lane((tm, width))
    pos = tile_in_seq * tm + lax.broadcasted_iota(jnp.int32, (tm, width), 0)
    total = sums[-1]
    half = jnp.full((tm, width), POOL_HALF_WINDOWS[-1], jnp.int32)
    for gi in range(len(POOL_HALF_WINDOWS) - 2, -1, -1):
        sel = lane < (gi + 1) * group
        total = jnp.where(sel, sums[gi], total)
        half = jnp.where(sel, POOL_HALF_WINDOWS[gi], half)
    cnt = jnp.minimum(pos + half, seq_len) - jnp.maximum(pos - half, 0)
    pooled = total / cnt.astype(F32) - u
    o_ref[...] = (_dot(pooled.astype(BF16), pw_ref[...]) * ps_ref[...]).astype(BF16)


def _pool(tl, zp, pw_bd, ps, off):
    b, s, w = zp.shape
    return pl.pallas_call(
        functools.partial(_pool_kernel, tl=tl, off=off),
        grid=tl.grid(off),
        in_specs=[tl.tok(w, off), tl.halo_prev(w, off), tl.halo_next(w, off),
                  tl.const(pw_bd.shape), tl.const(ps.shape)],
        out_specs=tl.tok(w, off),
        out_shape=jax.ShapeDtypeStruct((b, s, w), BF16),
        compiler_params=_params(("parallel", "parallel")),
        name="pool_mixer",
    )(zp, zp, zp, pw_bd, ps)


def _rope(x, tc, ts1, ts2):
    return x * tc + pltpu.roll(x, LANES - MLA_ROPE // 2, 1) * ts1 + pltpu.roll(x, MLA_ROPE // 2, 1) * ts2


def _qkv_kernel(zq_ref, zkv_ref, tc_ref, ts1_ref, ts2_ref, qng_ref, kvng_ref, gq_ref, gk_ref, gkr_ref,
                wuq_ref, wuk_ref, wuv_ref, q_ref, k_ref, v_ref):
    tc, ts1, ts2 = tc_ref[...], ts1_ref[...], ts2_ref[...]
    lane = _lane(tc.shape)
    nope_m = lane < MLA_NOPE
    rope_m = jnp.logical_and(lane >= MLA_NOPE, lane < MLA_NOPE + MLA_ROPE)

    zq = zq_ref[...]
    qc = (zq * _rms(zq, zq.shape[-1]) * qng_ref[...]).astype(BF16)
    q = _dot(qc, wuq_ref[...])
    q_scale = LOG2E * (MLA_NOPE + MLA_ROPE) ** -0.5
    for h in range(MLA_HEADS):
        qh = q[:, h * HEAD_SLOT:(h + 1) * HEAD_SLOT]
        sq = qh * qh
        ss_n = jnp.sum(jnp.where(nope_m, sq, 0.0), axis=-1, keepdims=True) * (1.0 / MLA_NOPE)
        ss_r = jnp.sum(jnp.where(rope_m, sq, 0.0), axis=-1, keepdims=True) * (1.0 / MLA_ROPE)
        inv = jnp.where(nope_m, lax.rsqrt(ss_n + NORM_EPS), lax.rsqrt(ss_r + NORM_EPS))
        qh = _rope(qh * inv * gq_ref[...], tc, ts1, ts2)
        q_ref[:, h * HEAD_SLOT:(h + 1) * HEAD_SLOT] = (qh * q_scale).astype(BF16)

    zkv = zkv_ref[...]
    kv_w = kvng_ref.shape[-1]
    kvc = zkv[:, :kv_w]
    kvn = (kvc * _rms(kvc, kv_w) * kvng_ref[...]).astype(BF16)
    kr = zkv[:, kv_w:kv_w + LANES]
    kr = kr * lax.rsqrt(jnp.sum(kr * kr, axis=-1, keepdims=True) * (1.0 / MLA_ROPE) + NORM_EPS) * gkr_ref[...]
    kr = _rope(pltpu.roll(kr, MLA_NOPE, 1), tc, ts1, ts2)
    kn = _dot(kvn, wuk_ref[...])
    for h in range(MLA_HEADS):
        kh = kn[:, h * HEAD_SLOT:(h + 1) * HEAD_SLOT]
        inv = lax.rsqrt(jnp.sum(kh * kh, axis=-1, keepdims=True) * (1.0 / MLA_NOPE) + NORM_EPS)
        k_ref[:, h * HEAD_SLOT:(h + 1) * HEAD_SLOT] = (kh * inv * gk_ref[...] + kr).astype(BF16)
    v_ref[...] = _dot(kvn, wuv_ref[...]).astype(BF16)


def _qkv_up(tl, zq, zkv, tabs, smalls, wuq, wuk, wuv):
    b, s, _ = zq.shape
    tc, ts1, ts2 = tabs
    tab_spec = pl.BlockSpec((tl.tm, LANES), lambda bb, ss: (ss, 0))
    widths = (wuq.shape[1], wuk.shape[1], wuv.shape[1])
    return pl.pallas_call(
        _qkv_kernel,
        grid=tl.grid(0),
        in_specs=[tl.tok(zq.shape[-1], 0), tl.tok(zkv.shape[-1], 0), tab_spec, tab_spec, tab_spec]
        + [tl.const(a.shape) for a in smalls] + [tl.const(w.shape) for w in (wuq, wuk, wuv)],
        out_specs=[tl.tok(w, 0) for w in widths],
        out_shape=[jax.ShapeDtypeStruct((b, s, w), BF16) for w in widths],
        compiler_params=_params(("parallel", "parallel")),
        name="qkv_up",
    )(zq, zkv, tc, ts1, ts2, *smalls, wuq, wuk, wuv)


def _attn_kernel(q_ref, k_ref, v_ref, o_ref):
    v = v_ref[...]
    outs = []
    for hh in range(2):
        q = q_ref[:, hh * HEAD_SLOT:(hh + 1) * HEAD_SLOT]
        k = k_ref[:, hh * HEAD_SLOT:(hh + 1) * HEAD_SLOT]
        s = lax.dot_general(q, k, (((1,), (1,)), ((), ())), preferred_element_type=F32)
        p = jnp.exp2(s - jnp.max(s, axis=-1, keepdims=True))
        denom = jnp.sum(p, axis=-1, keepdims=True)
        outs.append(_dot(p.astype(BF16), v) / denom)
    o_ref[...] = jnp.where(_lane(outs[0].shape) < MLA_V, outs[0], outs[1]).astype(BF16)


def _attention(q, k, v, *, tq, q_off_tiles, n_q_tiles, n_keys):
    b, _, qw = q.shape
    pairs = qw // (2 * HEAD_SLOT)
    return pl.pallas_call(
        _attn_kernel,
        grid=(b, pairs, n_q_tiles),
        in_specs=[pl.BlockSpec((None, tq, 2 * HEAD_SLOT), lambda bb, hp, i: (bb, i + q_off_tiles, hp)),
                  pl.BlockSpec((None, n_keys, 2 * HEAD_SLOT), lambda bb, hp, i: (bb, 0, hp)),
                  pl.BlockSpec((None, n_keys, 2 * MLA_V), lambda bb, hp, i: (bb, 0, hp))],
        out_specs=pl.BlockSpec((None, tq, 2 * MLA_V), lambda bb, hp, i: (bb, i, hp)),
        out_shape=jax.ShapeDtypeStruct((b, n_q_tiles * tq, pairs * 2 * MLA_V), BF16),
        compiler_params=_params(("parallel", "parallel", "arbitrary")),
        name="attention",
    )(q, k, v)


def _rev_rows(x):
    n = x.shape[0]
    row = lax.broadcasted_iota(jnp.int32, x.shape, 0)
    for sh in (1, 2, 4):
        x = jnp.where((row & sh) == 0, pltpu.roll(x, n - sh, 0), pltpu.roll(x, sh, 0))
    groups = [x[g * SUBLANES:(g + 1) * SUBLANES] for g in range(n // SUBLANES)]
    return jnp.concatenate(groups[::-1], axis=0)


def _rev_tile(tl, s_abs):
    return jnp.where(s_abs < tl.nct, tl.nct - 1 - s_abs, tl.ns - 1 + tl.nct - s_abs)


def _rwkv_prep_kernel(z_ref, zp_ref, zn_ref, mu_ref, kkw_ref, w0_ref, w2_ref, a0_ref, a2_ref, ka_ref,
                      rk_ref, g2_ref,
                      r_o, v_o, nkk_o, gg_o, bonus_o, wf_o, kf_o, bf_o, rb_o, vb_o, nkkb_o, wb_o, kb_o, bb_o,
                      *, tl, off):
    first, last, _, _ = _seq_flags(tl, off)
    tm = tl.tm
    z = z_ref[...]
    row = lax.broadcasted_iota(jnp.int32, z.shape, 0)
    prev_row = jnp.where(first, 0.0, zp_ref[HALO - 1:HALO, :])
    next_row = jnp.where(last, 0.0, zn_ref[0:1, :])
    z_prev = jnp.where(row == 0, prev_row, pltpu.roll(z, 1, 0))
    z_next = jnp.where(row == tm - 1, next_row, pltpu.roll(z, tm - 1, 0))
    zs = z + mu_ref[0:1, :] * (z_prev - z) + mu_ref[1:2, :] * (z_next - z)

    w = kkw_ref.shape[-1]
    r, k, v = zs[:, 0:w], zs[:, w:2 * w], zs[:, 2 * w:3 * w]
    o = 3 * w
    wd = zs[:, o:o + 2 * DECAY_RANK]
    ad = zs[:, o + 2 * DECAY_RANK:o + 2 * DECAY_RANK + 2 * AAA_RANK]
    gd = zs[:, o + 2 * DECAY_RANK + 2 * AAA_RANK:]

    kk = k * kkw_ref[...]
    kk = kk * lax.rsqrt(jnp.maximum(_seg64_sum(kk * kk), 1e-24))
    u = w0_ref[...] + _dot(jnp.tanh(wd).astype(BF16), w2_ref[...])
    decay = jnp.exp(-math.exp(-0.5) * _sigmoid(u))
    a = _sigmoid(a0_ref[...] + _dot(ad.astype(BF16), a2_ref[...]))
    ka = ka_ref[...]
    k_sum = jnp.zeros_like(k)
    for d, (w_o, k_o, b_o, order) in enumerate(((wf_o, kf_o, bf_o, lambda t: t), (wb_o, kb_o, bb_o, _rev_rows))):
        a_d = a[:, d * w:(d + 1) * w]
        k_d = k * (1.0 + (a_d - 1.0) * ka[:, d * w:(d + 1) * w])
        w_o[...] = order(decay[:, d * w:(d + 1) * w])
        k_o[...] = order(k_d)
        b_o[...] = order(kk * a_d)
        k_sum = k_sum + k_d
    r_o[...] = r
    v_o[...] = v
    nkk_o[...] = -kk
    rb_o[...] = _rev_rows(r)
    vb_o[...] = _rev_rows(v)
    nkkb_o[...] = _rev_rows(-kk)
    gg_o[...] = _dot(_sigmoid(gd).astype(BF16), g2_ref[...])
    bonus_o[...] = _seg64_sum(r * (0.5 * k_sum) * rk_ref[...]) * v


def _rwkv_prep(tl, zrw, smalls):
    b, s, win = zrw.shape
    w = smalls[1].shape[-1]
    n_fwd, n_bwd = 8, 6
    bwd_spec = pl.BlockSpec((None, tl.tm, w), lambda bb, ss: (bb, _rev_tile(tl, ss), 0))
    return pl.pallas_call(
        functools.partial(_rwkv_prep_kernel, tl=tl, off=0),
        grid=tl.grid(0),
        in_specs=[tl.tok(win, 0), tl.halo_prev(win, 0), tl.halo_next(win, 0)]
        + [tl.const(a.shape) for a in smalls],
        out_specs=[tl.tok(w, 0)] * n_fwd + [bwd_spec] * n_bwd,
        out_shape=[jax.ShapeDtypeStruct((b, s, w), F32)] * (n_fwd + n_bwd),
        compiler_params=_params(("parallel", "parallel")),
        name="rwkv_prep",
    )(zrw, zrw, zrw, *smalls)


def _scan_kernel(a_ref, w_ref, b_ref, k_ref, r_ref, v_ref, y_ref, s_ref):
    @pl.when(pl.program_id(1) == 0)
    def _():
        s_ref[...] = jnp.zeros_like(s_ref)

    n_k = s_ref.shape[0]
    steps = v_ref.shape[0]

    def step(t, carry):
        v = v_ref[t]
        acc = [jnp.zeros_like(v), jnp.zeros_like(v)]
        for kk in range(n_k):
            acc[kk % 2] = acc[kk % 2] + s_ref[kk] * a_ref[t, pl.ds(kk, 1), :]
        sa = acc[0] + acc[1]
        yacc = [jnp.zeros_like(v), jnp.zeros_like(v)]
        for kk in range(n_k):
            sn = (s_ref[kk] * w_ref[t, pl.ds(kk, 1), :] + sa * b_ref[t, pl.ds(kk, 1), :]
                  + v * k_ref[t, pl.ds(kk, 1), :])
            s_ref[kk] = sn
            yacc[kk % 2] = yacc[kk % 2] + sn * r_ref[t, pl.ds(kk, 1), :]
        y_ref[t] = yacc[0] + yacc[1]
        return carry

    lax.fori_loop(0, steps, step, 0)


def _scan(ops_k, op_v, chunk):
    steps, n_k, nl = ops_k[0].shape
    nv = op_v.shape[1]
    kspec = pl.BlockSpec((chunk, n_k, LANES), lambda lb, c: (c, 0, lb))
    vspec = pl.BlockSpec((chunk, nv, LANES), lambda lb, c: (c, 0, lb))
    return pl.pallas_call(
        _scan_kernel,
        grid=(nl // LANES, steps // chunk),
        in_specs=[kspec] * 5 + [vspec],
        out_specs=vspec,
        out_shape=jax.ShapeDtypeStruct((steps, nv, nl), F32),
        scratch_shapes=[pltpu.VMEM((n_k, nv, LANES), F32)],
        compiler_params=_params(("parallel", "arbitrary")),
        name="rwkv_scan",
    )(*ops_k, op_v)


def _rwkv_scan(tl, fwd, bwd):
    r, v, nkk, w_f, k_f, b_f = fwd
    r_b, v_b, nkk_b, w_b, k_b, b_b = bwd
    b, s, width = r.shape
    heads = width // RWKV_HEAD
    half = RWKV_HEAD // 2
    nl = 2 * 2 * b * heads
    nl_pad = -(-nl // LANES) * LANES

    def pad(x):
        return x if nl_pad == nl else jnp.pad(x, ((0, 0), (0, 0), (0, nl_pad - nl)))

    def k_layout(x_f, x_b):
        z = jnp.stack([x_f, x_b]).reshape(2, b, s, heads, RWKV_HEAD)
        z = z.transpose(2, 4, 0, 1, 3).reshape(s, RWKV_HEAD, 2 * b * heads)
        return pad(jnp.concatenate([z, z], axis=-1))

    def v_layout(x_f, x_b):
        z = jnp.stack([x_f, x_b]).reshape(2, b, s, heads, 2, half)
        return pad(z.transpose(2, 5, 4, 0, 1, 3).reshape(s, half, nl))

    ops_k = [k_layout(nkk, nkk_b), k_layout(w_f, w_b), k_layout(b_f, b_b), k_layout(k_f, k_b), k_layout(r, r_b)]
    chunk = 32 if s % 32 == 0 else 16
    y = _scan(ops_k, v_layout(v, v_b), chunk)
    if nl_pad != nl:
        y = y[:, :, :nl]
    return y.reshape(s, half, 2, 2, b, heads).transpose(3, 4, 0, 5, 2, 1).reshape(2, b, s, width)


def _merge_kernel(yf_ref, yb_ref, bonus_ref, gg_ref, op_ref, om_ref, gate_ref, x_ref, g1_ref, lnw_ref, lnb_ref,
                  wbp_ref, wbm_ref, wbr_ref, wo_ref, o_ref):
    y = yf_ref[...] + _rev_rows(yb_ref[...])
    mu = _seg64_sum(y) * (1.0 / RWKV_HEAD)
    yc = y - mu
    var = _seg64_sum(yc * yc) * (1.0 / RWKV_HEAD)
    yn = yc * lax.rsqrt(var + RWKV_GN_EPS) * lnw_ref[...] + lnb_ref[...]
    o_rw = ((yn + bonus_ref[...]) * gg_ref[...]).astype(BF16)
    d = x_ref.shape[-1]
    m = (gate_ref[:, 0:d].astype(F32) * _dot(op_ref[...], wbp_ref[...])
         + gate_ref[:, d:2 * d].astype(F32) * _dot(om_ref[...], wbm_ref[...])
         + gate_ref[:, 2 * d:3 * d].astype(F32) * _dot(o_rw, wbr_ref[...]))
    o_ref[...] = x_ref[...] + g1_ref[...] * _dot(m.astype(BF16), wo_ref[...])


def _merge(tl, off, y, bonus, gg, o_pool, o_mla, gate, xc, mod, lnw, lnb, wbp, wbm, wbr, wo):
    b, _, d = xc.shape
    n_tiles = tl.ns - off
    out_spec = pl.BlockSpec((None, tl.tm, d), lambda bb, ss: (bb, ss, 0))
    mla_spec = pl.BlockSpec((None, tl.tm, o_mla.shape[-1]), lambda bb, ss: (bb, ss, 0))
    rw = y.shape[-1]
    yf_spec = pl.BlockSpec((None, None, tl.tm, rw), lambda bb, ss: (0, bb, ss + off, 0))
    yb_spec = pl.BlockSpec((None, None, tl.tm, rw), lambda bb, ss: (1, bb, _rev_tile(tl, ss + off), 0))
    return pl.pallas_call(
        _merge_kernel,
        grid=tl.grid(off),
        in_specs=[yf_spec, yb_spec, tl.tok(rw, off), tl.tok(rw, off), tl.tok(o_pool.shape[-1], off), mla_spec,
                  tl.tok(gate.shape[-1], off), tl.tok(d, off), tl.mod(2, off),
                  tl.const(lnw.shape), tl.const(lnb.shape)] + [tl.const(w.shape) for w in (wbp, wbm, wbr, wo)],
        out_specs=out_spec,
        out_shape=jax.ShapeDtypeStruct((b, n_tiles * tl.tm, d), F32),
        compiler_params=_params(("parallel", "parallel")),
        name="merge",
    )(y, y, bonus, gg, o_pool, o_mla, gate, xc, mod, lnw, lnb, wbp, wbm, wbr, wo)


def _mlp_kernel(x_ref, g_ref, *rest, parts, tm):
    mods, (w1_ref, w2_ref, o_ref, h_ref, acc_ref) = rest[:3 * parts], rest[3 * parts:]
    j = pl.program_id(1)

    @pl.when(j == 0)
    def _():
        for p in range(parts):
            rows = slice(p * tm, (p + 1) * tm)
            h_ref[rows, :] = _norm_mod(x_ref[rows, :], g_ref[...], mods[3 * p][...], mods[3 * p + 1][...]).astype(BF16)
        acc_ref[...] = jnp.zeros_like(acc_ref)

    a = jnp.maximum(_dot(h_ref[...], w1_ref[...]), 0.0)
    acc_ref[...] += _dot((a * a).astype(BF16), w2_ref[...])

    @pl.when(j == pl.num_programs(1) - 1)
    def _():
        for p in range(parts):
            rows = slice(p * tm, (p + 1) * tm)
            o_ref[rows, :] = x_ref[rows, :] + mods[3 * p + 2][...] * acc_ref[rows, :]


def _mlp(tl, x1, mod_off, g, mod, w1, w2):
    b, s1, d = x1.shape
    dff = w1.shape[1]
    tf = 1024 if dff % 1024 == 0 else dff
    tm = tl.tm
    per_batch = s1 // tm
    parts = 2 if (b * per_batch) % 2 == 0 else 1
    nct, ctx_row = tl.nct, tl.ctx_row

    def mod_spec(j, p):
        def idx(ii, jj):
            sub = ii * parts + p
            bb, ss = sub // per_batch, sub % per_batch
            return (jnp.where(ss + mod_off < nct, ctx_row, bb), j, 0, 0)
        return pl.BlockSpec((None, None, 1, d), idx)

    mod_specs = [mod_spec(j, p) for p in range(parts) for j in (3, 4, 5)]
    x_spec = pl.BlockSpec((parts * tm, d), lambda ii, jj: (ii, 0))
    out = pl.pallas_call(
        functools.partial(_mlp_kernel, parts=parts, tm=tm),
        grid=(b * per_batch // parts, dff // tf),
        in_specs=[x_spec, pl.BlockSpec((1, d), lambda ii, jj: (0, 0))] + mod_specs
        + [pl.BlockSpec((d, tf), lambda ii, jj: (0, jj)), pl.BlockSpec((tf, d), lambda ii, jj: (jj, 0))],
        out_specs=x_spec,
        out_shape=jax.ShapeDtypeStruct((b * s1, d), F32),
        scratch_shapes=[pltpu.VMEM((parts * tm, d), BF16), pltpu.VMEM((parts * tm, d), F32)],
        compiler_params=_params(("parallel", "arbitrary")),
        name="mlp",
    )(x1.reshape(b * s1, d), g, *([mod] * (3 * parts)), w1, w2)
    return out.reshape(b, s1, d)


def _rope_tables(lc, l):
    rows = l // GRID_W
    row = jnp.repeat(jnp.arange(rows), GRID_W).astype(F32)
    col = jnp.tile(jnp.arange(GRID_W), rows).astype(F32)
    n_freq = MLA_ROPE // 4
    inv_freq = jnp.power(ROPE_BASE, -jnp.arange(n_freq, dtype=F32) / n_freq)
    ang = jnp.concatenate([row[:, None] * inv_freq, col[:, None] * inv_freq], axis=-1)
    cos = jnp.concatenate([jnp.ones((lc, MLA_ROPE // 2), F32), jnp.cos(ang)], axis=0)
    sin = jnp.concatenate([jnp.zeros((lc, MLA_ROPE // 2), F32), jnp.sin(ang)], axis=0)
    s = lc + l
    pad = jnp.zeros((s, LANES - MLA_NOPE - MLA_ROPE), F32)
    z16 = jnp.zeros((s, MLA_ROPE // 2), F32)
    zn = jnp.zeros((s, MLA_NOPE), F32)
    tc = jnp.concatenate([jnp.ones((s, MLA_NOPE), F32), cos, cos, pad], axis=1)
    ts1 = jnp.concatenate([zn, -sin, z16, pad], axis=1)
    ts2 = jnp.concatenate([zn, z16, sin, pad], axis=1)
    return tc, ts1, ts2


def _slot_cols(w, per_head):
    k = w.shape[0]
    w = w.reshape(k, MLA_HEADS, per_head)
    return jnp.pad(w, ((0, 0), (0, 0), (0, HEAD_SLOT - per_head))).reshape(k, MLA_HEADS * HEAD_SLOT)


def _block_diag(blocks):
    n = len(blocks)
    rows = []
    for i, blk in enumerate(blocks):
        rows.append(jnp.concatenate(
            [blk if j == i else jnp.zeros((blk.shape[0], blocks[j].shape[1]), blk.dtype) for j in range(n)], axis=1))
    return jnp.concatenate(rows, axis=0)


def _row(x):
    return x.reshape(1, -1).astype(F32)


def kernel(x, c, ctx, c_ctx, norm1_g, norm2_g, w_ada, b_ada, w_in, pool_w, pool_scale, mla_q_norm, mla_w_uq, mla_kv_norm, mla_w_ukv, qk_gain_q, qk_gain_k, rwkv_mu, rwkv_w0, rwkv_w2, rwkv_a0, rwkv_a2, rwkv_ka, rwkv_kk, rwkv_rk, rwkv_g2, rwkv_ln_w, rwkv_ln_b, w_br_pool, w_br_mla, w_br_rwkv, w_o, mlp_w1, mlp_w2):
    b, l, d = x.shape
    lc = ctx.shape[1]
    depth = w_in.shape[0]
    tl = _Tiles(b, lc, l, d)

    pool_width = pool_scale.shape[-1]
    q_rank = mla_q_norm.shape[-1]
    kv_rank = mla_kv_norm.shape[-1]
    rw_width = rwkv_kk.shape[-1]
    rw_in = rwkv_mu.shape[-1]

    rows = -(-(b + 1) // SUBLANES) * SUBLANES
    c_all = jnp.concatenate([c, c_ctx[None, :], jnp.zeros((rows - b - 1, d), F32)], axis=0)
    mod_all = _ada_mod(c_all, w_ada, b_ada).reshape(depth, rows, N_MOD, 1, d)
    tabs = _rope_tables(lc, l)

    xc = jnp.concatenate([ctx, x], axis=1)
    out = None
    for i in range(depth):
        need_ctx = i < depth - 1
        off = 0 if need_ctx else tl.nct
        mod = mod_all[i]

        wi = w_in[i].astype(BF16)
        o0 = 0
        wp = wi[:, o0:o0 + pool_width]; o0 += pool_width
        wq = wi[:, o0:o0 + q_rank]; o0 += q_rank
        wkv = wi[:, o0:o0 + kv_rank + MLA_ROPE]; o0 += kv_rank + MLA_ROPE
        wkv = jnp.pad(wkv, ((0, 0), (0, LANES - MLA_ROPE)))
        wrw = wi[:, o0:o0 + rw_in]; o0 += rw_in
        wg = wi[:, o0:]

        zp, zq, zkv, zrw, gate = _in_proj(tl, xc, _row(norm1_g[i]), mod, wp, wq, wkv, wrw, wg)

        pw_bd = _block_diag([pool_w[i, g] for g in range(pool_w.shape[1])]).astype(BF16)
        o_pool = _pool(tl, zp, pw_bd, _row(pool_scale[i]), off)

        wuq = _slot_cols(mla_w_uq[i], MLA_NOPE + MLA_ROPE).astype(BF16)
        wukv = mla_w_ukv[i].reshape(kv_rank, MLA_HEADS, MLA_NOPE + MLA_V)
        wuk = _slot_cols(wukv[:, :, :MLA_NOPE].reshape(kv_rank, -1), MLA_NOPE).astype(BF16)
        wuv = wukv[:, :, MLA_NOPE:].reshape(kv_rank, MLA_HEADS * MLA_V).astype(BF16)
        zpad = jnp.zeros((LANES - MLA_NOPE - MLA_ROPE,), F32)
        gq = _row(jnp.concatenate([qk_gain_q[i], zpad]))
        gk = _row(jnp.concatenate([qk_gain_k[i, :MLA_NOPE], jnp.zeros((LANES - MLA_NOPE,), F32)]))
        gkr = _row(jnp.concatenate([qk_gain_k[i, MLA_NOPE:], jnp.zeros((LANES - MLA_ROPE,), F32)]))
        smalls = (_row(mla_q_norm[i]), _row(mla_kv_norm[i]), gq, gk, gkr)
        q, k, v = _qkv_up(tl, zq, zkv, tabs, smalls, wuq, wuk, wuv)
        o_mla_l = _attention(q, k, v, tq=tl.tm, q_off_tiles=tl.nct, n_q_tiles=tl.ns - tl.nct, n_keys=tl.s)
        if need_ctx:
            o_mla_c = _attention(q, k, v, tq=tl.tm, q_off_tiles=0, n_q_tiles=tl.nct, n_keys=lc)
            o_mla = jnp.concatenate([o_mla_c, o_mla_l], axis=1)
        else:
            o_mla = o_mla_l

        w2cat = _block_diag([rwkv_w2[i, 0], rwkv_w2[i, 1]]).astype(BF16)
        a2cat = _block_diag([rwkv_a2[i, 0], rwkv_a2[i, 1]]).astype(BF16)
        rsmalls = (rwkv_mu[i].astype(F32), _row(rwkv_kk[i]), _row(rwkv_w0[i]), w2cat, _row(rwkv_a0[i]), a2cat,
                   _row(rwkv_ka[i]), _row(rwkv_rk[i]), rwkv_g2[i].astype(BF16))
        r, vv, nkk, gg, bonus, wf, kf, bf, rb, vb, nkkb, wb, kb, bb = _rwkv_prep(tl, zrw, rsmalls)
        y = _rwkv_scan(tl, (r, vv, nkk, wf, kf, bf), (rb, vb, nkkb, wb, kb, bb))

        x1 = _merge(tl, off, y, bonus, gg, o_pool, o_mla, gate, xc, mod,
                    _row(rwkv_ln_w[i]), _row(rwkv_ln_b[i]),
                    w_br_pool[i].astype(BF16), w_br_mla[i].astype(BF16), w_br_rwkv[i].astype(BF16),
                    w_o[i].astype(BF16))
        xc_next = _mlp(tl, x1, off, _row(norm2_g[i]), mod, mlp_w1[i].astype(BF16), mlp_w2[i].astype(BF16))
        if need_ctx:
            xc = xc_next
        else:
            out = xc_next
    return out
```

```python
import functools
import math

import jax
import jax.numpy as jnp
from jax import lax
from jax.experimental import pallas as pl
from jax.experimental.pallas import tpu as pltpu

F32 = jnp.float32
BF16 = jnp.bfloat16

NORM_EPS = 1e-6
RWKV_GN_EPS = 64e-5
GRID_W = 64
ROPE_BASE = 10000.0
POOL_HALF_WINDOWS = (1, 2, 4, 8)
N_MOD = 6
MLA_HEADS = 8
MLA_NOPE = 64
MLA_ROPE = 32
MLA_V = 64
RWKV_HEAD = 64
DECAY_RANK = 64
AAA_RANK = 64
GATE_RANK = 128

LANES = 128
SUBLANES = 8
HEAD_SLOT = LANES
HALO = SUBLANES
VMEM_LIMIT = 56 * 1024 * 1024

LOG2E = 1.4426950408889634


def _dot(a, b):
    return jnp.dot(a, b, preferred_element_type=F32)


def _sigmoid(x):
    return 1.0 / (1.0 + jnp.exp(-x))


def _rms(x, width):
    return lax.rsqrt(jnp.sum(x * x, axis=-1, keepdims=True) * (1.0 / width) + NORM_EPS)


def _norm_mod(x, g, shift, scale):
    return (x * _rms(x, x.shape[-1]) * g) * (1.0 + scale) + shift


def _lane(shape):
    return lax.broadcasted_iota(jnp.int32, shape, 1)


def _seg64_sum(x):
    cols = []
    for c in range(x.shape[1] // LANES):
        xc = x[:, c * LANES:(c + 1) * LANES]
        lo_m = _lane(xc.shape) < 64
        lo = jnp.sum(jnp.where(lo_m, xc, 0.0), axis=-1, keepdims=True)
        hi = jnp.sum(jnp.where(lo_m, 0.0, xc), axis=-1, keepdims=True)
        cols.append(jnp.where(lo_m, lo, hi))
    return cols[0] if len(cols) == 1 else jnp.concatenate(cols, axis=1)


def _params(sem, **flags):
    return pltpu.CompilerParams(dimension_semantics=sem, vmem_limit_bytes=VMEM_LIMIT, flags=flags or None)


class _Tiles:
    def __init__(self, batch, lc, l, d):
        self.batch, self.lc, self.l, self.d = batch, lc, l, d
        self.s = lc + l
        self.tm = 256 if (lc % 256 == 0 and l % 256 == 0) else 128
        assert lc % self.tm == 0 and l % self.tm == 0
        self.nct = lc // self.tm
        self.ns = self.s // self.tm
        self.ctx_row = batch

    def grid(self, off):
        return (self.batch, self.ns - off)

    def tok(self, width, off):
        return pl.BlockSpec((None, self.tm, width), lambda b, s: (b, s + off, 0))

    def halo_prev(self, width, off):
        r = self.tm // HALO
        return pl.BlockSpec((None, HALO, width), lambda b, s: (b, jnp.maximum((s + off) * r - 1, 0), 0))

    def halo_next(self, width, off):
        r = self.tm // HALO
        last = self.s // HALO - 1
        return pl.BlockSpec((None, HALO, width), lambda b, s: (b, jnp.minimum((s + off + 1) * r, last), 0))

    def mod(self, j, off):
        nct, ctx_row = self.nct, self.ctx_row
        return pl.BlockSpec((None, None, 1, self.d),
                            lambda b, s: (jnp.where(s + off < nct, ctx_row, b), j, 0, 0))

    def const(self, shape):
        nd = len(shape)
        return pl.BlockSpec(shape, lambda b, s: (0,) * nd)


def _ada_kernel(c_ref, w_ref, b_ref, o_ref):
    c = c_ref[...]
    s = (c * _sigmoid(c)).astype(BF16)
    o_ref[...] = _dot(s, w_ref[...].astype(BF16)) + b_ref[...]


def _ada_mod(c_all, w_ada, b_ada):
    depth, d, n = w_ada.shape
    rows = c_all.shape[0]
    tn = 1024
    return pl.pallas_call(
        _ada_kernel,
        grid=(depth, n // tn),
        in_specs=[pl.BlockSpec((rows, d), lambda i, j: (0, 0)),
                  pl.BlockSpec((None, d, tn), lambda i, j: (i, 0, j)),
                  pl.BlockSpec((None, 1, tn), lambda i, j: (i, 0, j))],
        out_specs=pl.BlockSpec((None, rows, tn), lambda i, j: (i, 0, j)),
        out_shape=jax.ShapeDtypeStruct((depth, rows, n), F32),
        compiler_params=_params(("parallel", "parallel")),
        name="ada_mod",
    )(c_all, w_ada, b_ada.reshape(depth, 1, n))


def _in_proj_kernel(x_ref, g_ref, sh_ref, sc_ref, wp_ref, wq_ref, wkv_ref, wrw_ref, wg_ref,
                    zp_ref, zq_ref, zkv_ref, zrw_ref, gate_ref):
    h = _norm_mod(x_ref[...], g_ref[...], sh_ref[...], sc_ref[...]).astype(BF16)
    zp_ref[...] = _dot(h, wp_ref[...])
    zq_ref[...] = _dot(h, wq_ref[...])
    zkv_ref[...] = _dot(h, wkv_ref[...])
    zrw_ref[...] = _dot(h, wrw_ref[...])
    d = x_ref.shape[-1]
    for c in range(wg_ref.shape[1] // d):
        gate_ref[:, c * d:(c + 1) * d] = _sigmoid(_dot(h, wg_ref[:, c * d:(c + 1) * d])).astype(BF16)


def _in_proj(tl, xc, g, mod, wp, wq, wkv, wrw, wg):
    b, s, d = xc.shape
    widths = (wp.shape[1], wq.shape[1], wkv.shape[1], wrw.shape[1], wg.shape[1])
    dts = (F32, F32, F32, F32, BF16)
    return pl.pallas_call(
        _in_proj_kernel,
        grid=tl.grid(0),
        in_specs=[tl.tok(d, 0), tl.const((1, d)), tl.mod(0, 0), tl.mod(1, 0)]
        + [tl.const(w.shape) for w in (wp, wq, wkv, wrw, wg)],
        out_specs=[tl.tok(w, 0) for w in widths],
        out_shape=[jax.ShapeDtypeStruct((b, s, w), dt) for w, dt in zip(widths, dts)],
        compiler_params=_params(("parallel", "parallel")),
        name="in_proj",
    )(xc, g, mod, mod, wp, wq, wkv, wrw, wg)


def _seq_flags(tl, off):
    s_abs = pl.program_id(1) + off
    is_ctx = s_abs < tl.nct
    first = jnp.logical_or(s_abs == 0, s_abs == tl.nct)
    last = jnp.logical_or(s_abs == tl.nct - 1, s_abs == tl.ns - 1)
    seq_len = jnp.where(is_ctx, tl.lc, tl.l)
    tile_in_seq = jnp.where(is_ctx, s_abs, s_abs - tl.nct)
    return first, last, seq_len, tile_in_seq


def _pool_kernel(u_ref, up_ref, un_ref, pw_ref, ps_ref, o_ref, *, tl, off):
    first, last, seq_len, tile_in_seq = _seq_flags(tl, off)
    tm = tl.tm
    u = u_ref[...]
    prev = jnp.where(first, 0.0, up_ref[...])
    nxt = jnp.where(last, 0.0, un_ref[...])
    e = jnp.concatenate([prev, u, nxt], axis=0)
    n = tm + 2 * HALO
    w2 = e + pltpu.roll(e, 1, 0)
    w4 = pltpu.roll(w2, n - 1, 0) + pltpu.roll(w2, 1, 0)
    w8 = pltpu.roll(w4, n - 2, 0) + pltpu.roll(w4, 2, 0)
    w16 = pltpu.roll(w8, n - 4, 0) + pltpu.roll(w8, 4, 0)
    sums = [w[HALO:HALO + tm] for w in (w2, w4, w8, w16)]
    width = u.shape[1]
    group = width // len(POOL_HALF_WINDOWS)
    lane = _lane((tm, width))
    pos = tile_in_seq * tm + lax.broadcasted_iota(jnp.int32, (tm, width), 0)
    total = sums[-1]
    half = jnp.full((tm, width), POOL_HALF_WINDOWS[-1], jnp.int32)
    for gi in range(len(POOL_HALF_WINDOWS) - 2, -1, -1):
        sel = lane < (gi + 1) * group
        total = jnp.where(sel, sums[gi], total)
        half = jnp.where(sel, POOL_HALF_WINDOWS[gi], half)
    cnt = jnp.minimum(pos + half, seq_len) - jnp.maximum(pos - half, 0)
    pooled = total / cnt.astype(F32) - u
    o_ref[...] = (_dot(pooled.astype(BF16), pw_ref[...]) * ps_ref[...]).astype(BF16)


def _pool(tl, zp, pw_bd, ps, off):
    b, s, w = zp.shape
    return pl.pallas_call(
        functools.partial(_pool_kernel, tl=tl, off=off),
        grid=tl.grid(off),
        in_specs=[tl.tok(w, off), tl.halo_prev(w, off), tl.halo_next(w, off),
                  tl.const(pw_bd.shape), tl.const(ps.shape)],
        out_specs=tl.tok(w, off),
        out_shape=jax.ShapeDtypeStruct((b, s, w), BF16),
        compiler_params=_params(("parallel", "parallel")),
        name="pool_mixer",
    )(zp, zp, zp, pw_bd, ps)


def _rope(x, tc, ts1, ts2):
    return x * tc + pltpu.roll(x, LANES - MLA_ROPE // 2, 1) * ts1 + pltpu.roll(x, MLA_ROPE // 2, 1) * ts2


def _qkv_kernel(zq_ref, zkv_ref, tc_ref, ts1_ref, ts2_ref, qng_ref, kvng_ref, gq_ref, gk_ref, gkr_ref,
                wuq_ref, wuk_ref, wuv_ref, q_ref, k_ref, v_ref):
    tc, ts1, ts2 = tc_ref[...], ts1_ref[...], ts2_ref[...]
    lane = _lane(tc.shape)
    nope_m = lane < MLA_NOPE
    rope_m = jnp.logical_and(lane >= MLA_NOPE, lane < MLA_NOPE + MLA_ROPE)

    zq = zq_ref[...]
    qc = (zq * _rms(zq, zq.shape[-1]) * qng_ref[...]).astype(BF16)
    q = _dot(qc, wuq_ref[...])
    q_scale = LOG2E * (MLA_NOPE + MLA_ROPE) ** -0.5
    for h in range(MLA_HEADS):
        qh = q[:, h * HEAD_SLOT:(h + 1) * HEAD_SLOT]
        sq = qh * qh
        ss_n = jnp.sum(jnp.where(nope_m, sq, 0.0), axis=-1, keepdims=True) * (1.0 / MLA_NOPE)
        ss_r = jnp.sum(jnp.where(rope_m, sq, 0.0), axis=-1, keepdims=True) * (1.0 / MLA_ROPE)
        inv = jnp.where(nope_m, lax.rsqrt(ss_n + NORM_EPS), lax.rsqrt(ss_r + NORM_EPS))
        qh = _rope(qh * inv * gq_ref[...], tc, ts1, ts2)
        q_ref[:, h * HEAD_SLOT:(h + 1) * HEAD_SLOT] = (qh * q_scale).astype(BF16)

    zkv = zkv_ref[...]
    kv_w = kvng_ref.shape[-1]
    kvc = zkv[:, :kv_w]
    kvn = (kvc * _rms(kvc, kv_w) * kvng_ref[...]).astype(BF16)
    kr = zkv[:, kv_w:kv_w + LANES]
    kr = kr * lax.rsqrt(jnp.sum(kr * kr, axis=-1, keepdims=True) * (1.0 / MLA_ROPE) + NORM_EPS) * gkr_ref[...]
    kr = _rope(pltpu.roll(kr, MLA_NOPE, 1), tc, ts1, ts2)
    kn = _dot(kvn, wuk_ref[...])
    for h in range(MLA_HEADS):
        kh = kn[:, h * HEAD_SLOT:(h + 1) * HEAD_SLOT]
        inv = lax.rsqrt(jnp.sum(kh * kh, axis=-1, keepdims=True) * (1.0 / MLA_NOPE) + NORM_EPS)
        k_ref[:, h * HEAD_SLOT:(h + 1) * HEAD_SLOT] = (kh * inv * gk_ref[...] + kr).astype(BF16)
    v_ref[...] = _dot(kvn, wuv_ref[...]).astype(BF16)


def _qkv_up(tl, zq, zkv, tabs, smalls, wuq, wuk, wuv):
    b, s, _ = zq.shape
    tc, ts1, ts2 = tabs
    tab_spec = pl.BlockSpec((tl.tm, LANES), lambda bb, ss: (ss, 0))
    widths = (wuq.shape[1], wuk.shape[1], wuv.shape[1])
    nct, nlt = tl.nct, tl.ns - tl.nct
    q_spec = pl.BlockSpec((None, tl.tm, widths[0]), lambda bb, ss: (bb, jnp.where(ss < nct, nlt + ss, ss - nct), 0))
    return pl.pallas_call(
        _qkv_kernel,
        grid=tl.grid(0),
        in_specs=[tl.tok(zq.shape[-1], 0), tl.tok(zkv.shape[-1], 0), tab_spec, tab_spec, tab_spec]
        + [tl.const(a.shape) for a in smalls] + [tl.const(w.shape) for w in (wuq, wuk, wuv)],
        out_specs=[q_spec, tl.tok(widths[1], 0), tl.tok(widths[2], 0)],
        out_shape=[jax.ShapeDtypeStruct((b, s, w), BF16) for w in widths],
        compiler_params=_params(("parallel", "parallel")),
        name="qkv_up",
    )(zq, zkv, tc, ts1, ts2, *smalls, wuq, wuk, wuv)


def _attn_kernel(q_ref, k_ref, v_ref, o_ref):
    v = v_ref[...]
    outs = []
    for hh in range(2):
        q = q_ref[:, hh * HEAD_SLOT:(hh + 1) * HEAD_SLOT]
        k = k_ref[:, hh * HEAD_SLOT:(hh + 1) * HEAD_SLOT]
        s = lax.dot_general(q, k, (((1,), (1,)), ((), ())), preferred_element_type=F32)
        p = jnp.exp2(s - jnp.max(s, axis=-1, keepdims=True))
        denom = jnp.sum(p, axis=-1, keepdims=True)
        outs.append(_dot(p.astype(BF16), v) / denom)
    o_ref[...] = jnp.where(_lane(outs[0].shape) < MLA_V, outs[0], outs[1]).astype(BF16)


def _attention(q, k, v, *, tq, q_off_tiles, n_q_tiles, n_keys):
    b, _, qw = q.shape
    pairs = qw // (2 * HEAD_SLOT)
    return pl.pallas_call(
        _attn_kernel,
        grid=(b, pairs, n_q_tiles),
        in_specs=[pl.BlockSpec((None, tq, 2 * HEAD_SLOT), lambda bb, hp, i: (bb, i + q_off_tiles, hp)),
                  pl.BlockSpec((None, n_keys, 2 * HEAD_SLOT), lambda bb, hp, i: (bb, 0, hp)),
                  pl.BlockSpec((None, n_keys, 2 * MLA_V), lambda bb, hp, i: (bb, 0, hp))],
        out_specs=pl.BlockSpec((None, tq, 2 * MLA_V), lambda bb, hp, i: (bb, i, hp)),
        out_shape=jax.ShapeDtypeStruct((b, n_q_tiles * tq, pairs * 2 * MLA_V), BF16),
        compiler_params=_params(("parallel", "parallel", "arbitrary")),
        name="attention",
    )(q, k, v)


def _rev_rows(x):
    n = x.shape[0]
    row = lax.broadcasted_iota(jnp.int32, x.shape, 0)
    for sh in (1, 2, 4):
        x = jnp.where((row & sh) == 0, pltpu.roll(x, n - sh, 0), pltpu.roll(x, sh, 0))
    groups = [x[g * SUBLANES:(g + 1) * SUBLANES] for g in range(n // SUBLANES)]
    return jnp.concatenate(groups[::-1], axis=0)


def _rev_tile(tl, s_abs):
    return jnp.where(s_abs < tl.nct, tl.nct - 1 - s_abs, tl.ns - 1 + tl.nct - s_abs)


def _rwkv_prep_kernel(z_ref, zp_ref, zn_ref, mu_ref, kkw_ref, w0_ref, w2_ref, a0_ref, a2_ref, ka_ref,
                      rk_ref, g2_ref,
                      r_o, v_o, nkk_o, gg_o, bonus_o, wf_o, kf_o, bf_o, rb_o, vb_o, nkkb_o, wb_o, kb_o, bb_o,
                      *, tl, off):
    first, last, _, _ = _seq_flags(tl, off)
    tm = tl.tm
    z = z_ref[...]
    row = lax.broadcasted_iota(jnp.int32, z.shape, 0)
    prev_row = jnp.where(first, 0.0, zp_ref[HALO - 1:HALO, :])
    next_row = jnp.where(last, 0.0, zn_ref[0:1, :])
    z_prev = jnp.where(row == 0, prev_row, pltpu.roll(z, 1, 0))
    z_next = jnp.where(row == tm - 1, next_row, pltpu.roll(z, tm - 1, 0))
    zs = z + mu_ref[0:1, :] * (z_prev - z) + mu_ref[1:2, :] * (z_next - z)

    w = kkw_ref.shape[-1]
    r, k, v = zs[:, 0:w], zs[:, w:2 * w], zs[:, 2 * w:3 * w]
    o = 3 * w
    wd = zs[:, o:o + 2 * DECAY_RANK]
    ad = zs[:, o + 2 * DECAY_RANK:o + 2 * DECAY_RANK + 2 * AAA_RANK]
    gd = zs[:, o + 2 * DECAY_RANK + 2 * AAA_RANK:]

    kk = k * kkw_ref[...]
    kk = kk * lax.rsqrt(jnp.maximum(_seg64_sum(kk * kk), 1e-24))
    u = w0_ref[...] + _dot(jnp.tanh(wd).astype(BF16), w2_ref[...])
    decay = jnp.exp(-math.exp(-0.5) * _sigmoid(u))
    a = _sigmoid(a0_ref[...] + _dot(ad.astype(BF16), a2_ref[...]))
    ka = ka_ref[...]
    k_sum = jnp.zeros_like(k)
    for d, (w_o, k_o, b_o, order) in enumerate(((wf_o, kf_o, bf_o, lambda t: t), (wb_o, kb_o, bb_o, _rev_rows))):
        a_d = a[:, d * w:(d + 1) * w]
        k_d = k * (1.0 + (a_d - 1.0) * ka[:, d * w:(d + 1) * w])
        w_o[...] = order(decay[:, d * w:(d + 1) * w])
        k_o[...] = order(k_d).astype(k_o.dtype)
        b_o[...] = order(kk * a_d).astype(b_o.dtype)
        k_sum = k_sum + k_d
    r_o[...] = r.astype(r_o.dtype)
    v_o[...] = v
    nkk_o[...] = (-kk).astype(nkk_o.dtype)
    rb_o[...] = _rev_rows(r).astype(rb_o.dtype)
    vb_o[...] = _rev_rows(v)
    nkkb_o[...] = _rev_rows(-kk).astype(nkkb_o.dtype)
    gg_o[...] = _dot(_sigmoid(gd).astype(BF16), g2_ref[...])
    bonus_o[...] = _seg64_sum(r * (0.5 * k_sum) * rk_ref[...]) * v


def _rwkv_prep(tl, zrw, smalls):
    b, s, win = zrw.shape
    w = smalls[1].shape[-1]
    n_fwd, n_bwd = 8, 6
    bwd_spec = pl.BlockSpec((None, tl.tm, w), lambda bb, ss: (bb, _rev_tile(tl, ss), 0))
    return pl.pallas_call(
        functools.partial(_rwkv_prep_kernel, tl=tl, off=0),
        grid=tl.grid(0),
        in_specs=[tl.tok(win, 0), tl.halo_prev(win, 0), tl.halo_next(win, 0)]
        + [tl.const(a.shape) for a in smalls],
        out_specs=[tl.tok(w, 0)] * n_fwd + [bwd_spec] * n_bwd,
        out_shape=[jax.ShapeDtypeStruct((b, s, w), dt) for dt in
                   (BF16, F32, BF16, F32, F32, F32, BF16, BF16, BF16, F32, BF16, F32, BF16, BF16)],
        compiler_params=_params(("parallel", "parallel")),
        name="rwkv_prep",
    )(zrw, zrw, zrw, *smalls)


def _scan_step(t, a_ref, w_ref, b_ref, k_ref, r_ref, v_ref, y_ref, s_ref):
    n_k = s_ref.shape[0]
    v = v_ref[t]
    acc = [jnp.zeros_like(v), jnp.zeros_like(v)]
    for kk in range(n_k):
        acc[kk % 2] = acc[kk % 2] + s_ref[kk] * a_ref[t, pl.ds(kk, 1), :]
    sa = acc[0] + acc[1]
    yacc = [jnp.zeros_like(v), jnp.zeros_like(v)]
    for kk in range(n_k):
        sn = (s_ref[kk] * w_ref[t, pl.ds(kk, 1), :] + sa * b_ref[t, pl.ds(kk, 1), :]
              + v * k_ref[t, pl.ds(kk, 1), :])
        s_ref[kk] = sn
        yacc[kk % 2] = yacc[kk % 2] + sn * r_ref[t, pl.ds(kk, 1), :]
    y_ref[t] = yacc[0] + yacc[1]


def _attn_scan_kernel(q_ref, k_ref, v_ref, a_ref, w_ref, b_ref, kk_ref, r_ref, vv_ref, o_ref, y_ref,
                      s_ref, m_ref, acc_ref, ops_ref, *, tk, unroll):
    first = jnp.logical_and(pl.program_id(0) == 0, jnp.logical_and(pl.program_id(1) == 0, pl.program_id(2) == 0))

    @pl.when(first)
    def _():
        s_ref[...] = jnp.zeros_like(s_ref)

    for i, ref in enumerate((a_ref, b_ref, kk_ref, r_ref)):
        ops_ref[i] = ref[...].astype(F32)
    a_ref, b_ref, kk_ref, r_ref = (ops_ref.at[i] for i in range(4))

    m_ref[...] = jnp.full_like(m_ref, -1e30)
    acc_ref[...] = jnp.zeros_like(acc_ref)
    lo_half = _lane((tk, 2 * MLA_V)) < MLA_V

    def block(j):
        start = j * tk if isinstance(j, int) else pl.multiple_of(j * tk, tk)
        vblk = v_ref[pl.ds(start, tk), :]
        v_ones = (jnp.where(lo_half, vblk, 1.0), jnp.where(lo_half, 1.0, vblk))
        for hh in range(2):
            q = q_ref[:, hh * HEAD_SLOT:(hh + 1) * HEAD_SLOT]
            kblk = k_ref[pl.ds(start, tk), hh * HEAD_SLOT:(hh + 1) * HEAD_SLOT]
            s = lax.dot_general(q, kblk, (((1,), (1,)), ((), ())), preferred_element_type=F32)
            m_old = m_ref[hh]
            m_new = jnp.maximum(m_old, jnp.max(s, axis=-1, keepdims=True))
            p = jnp.exp2(s - jnp.concatenate([m_new] * (tk // LANES), axis=1))
            acc_ref[hh] = jnp.exp2(m_old - m_new) * acc_ref[hh] + _dot(p.astype(BF16), v_ones[hh])
            m_ref[hh] = m_new
        _scan_step(j, a_ref, w_ref, b_ref, kk_ref, r_ref, vv_ref, y_ref, s_ref)

    nkv = vv_ref.shape[0]

    def body(jj, carry):
        for u in range(unroll):
            block(jj * unroll + u)
        return carry

    lax.fori_loop(0, nkv // unroll, body, 0)
    for j in range(nkv - nkv % unroll, nkv):
        block(j)
    a0, a1 = acc_ref[0], acc_ref[1]
    o = jnp.where(_lane(a0.shape) < MLA_V, a0 / pltpu.roll(a0, MLA_V, 1), a1 / pltpu.roll(a1, MLA_V, 1))
    o_ref[...] = o.astype(BF16)


def _attn_scan(q, k, v, ops_k, op_v, *, tq, tk, n_q):
    b, s, qw = q.shape[0], k.shape[1], q.shape[2]
    pairs = qw // (2 * HEAD_SLOT)
    nq_t, nkv = n_q // tq, s // tk
    n_k, nv = ops_k[0].shape[1], op_v.shape[1]
    assert op_v.shape[0] == b * pairs * nq_t * nkv and op_v.shape[2] == LANES

    def step_idx(bb, hp, i):
        return ((bb * pairs + hp) * nq_t + i, 0, 0)

    kspec = pl.BlockSpec((nkv, n_k, LANES), step_idx)
    vspec = pl.BlockSpec((nkv, nv, LANES), step_idx)
    return pl.pallas_call(
        functools.partial(_attn_scan_kernel, tk=tk, unroll=8),
        grid=(b, pairs, nq_t),
        in_specs=[pl.BlockSpec((None, tq, 2 * HEAD_SLOT), lambda bb, hp, i: (bb, i, hp)),
                  pl.BlockSpec((None, s, 2 * HEAD_SLOT), lambda bb, hp, i: (bb, 0, hp)),
                  pl.BlockSpec((None, s, 2 * MLA_V), lambda bb, hp, i: (bb, 0, hp))] + [kspec] * 5 + [vspec],
        out_specs=[pl.BlockSpec((None, tq, 2 * MLA_V), lambda bb, hp, i: (bb, i, hp)), vspec],
        out_shape=[jax.ShapeDtypeStruct((b, n_q, pairs * 2 * MLA_V), BF16),
                   jax.ShapeDtypeStruct(op_v.shape, F32)],
        scratch_shapes=[pltpu.VMEM((n_k, nv, LANES), F32), pltpu.VMEM((2, tq, LANES), F32),
                        pltpu.VMEM((2, tq, 2 * MLA_V), F32), pltpu.VMEM((4, nkv, n_k, LANES), F32)],
        compiler_params=_params(("arbitrary", "arbitrary", "arbitrary")),
        name="attn_scan",
    )(q, k, v, *ops_k, op_v)


def _scan_operands(fwd, bwd, n_steps):
    r, v, nkk, w_f, k_f, b_f = fwd
    r_b, v_b, nkk_b, w_b, k_b, b_b = bwd
    b, s, width = r.shape
    heads = width // RWKV_HEAD
    half = RWKV_HEAD // 2
    nl = 2 * 2 * b * heads
    assert nl <= LANES

    def pad(x, step_fill):
        x = jnp.pad(x, ((0, 0), (0, 0), (0, LANES - nl)))
        return jnp.pad(x, ((0, n_steps - s), (0, 0), (0, 0)), constant_values=step_fill)

    def k_layout(x_f, x_b, step_fill=0.0):
        z = jnp.stack([x_f, x_b]).reshape(2, b, s, heads, RWKV_HEAD)
        z = z.transpose(2, 4, 0, 1, 3).reshape(s, RWKV_HEAD, 2 * b * heads)
        return pad(jnp.concatenate([z, z], axis=-1), step_fill)

    def v_layout(x_f, x_b):
        z = jnp.stack([x_f, x_b]).reshape(2, b, s, heads, 2, half)
        return pad(z.transpose(2, 5, 4, 0, 1, 3).reshape(s, half, nl), 0.0)

    ops_k = [k_layout(nkk, nkk_b), k_layout(w_f, w_b, 1.0), k_layout(b_f, b_b), k_layout(k_f, k_b),
             k_layout(r, r_b)]
    return ops_k, v_layout(v, v_b)


def _scan_result(y, b, s, width):
    heads = width // RWKV_HEAD
    half = RWKV_HEAD // 2
    y = y[:s, :, :2 * 2 * b * heads]
    return y.reshape(s, half, 2, 2, b, heads).transpose(3, 4, 0, 5, 2, 1).reshape(2, b, s, width)


def _merge_kernel(yf_ref, yb_ref, bonus_ref, gg_ref, op_ref, om_ref, gate_ref, x_ref, g1_ref, lnw_ref, lnb_ref,
                  wbp_ref, wbm_ref, wbr_ref, wo_ref, o_ref):
    y = yf_ref[...] + _rev_rows(yb_ref[...])
    mu = _seg64_sum(y) * (1.0 / RWKV_HEAD)
    yc = y - mu
    var = _seg64_sum(yc * yc) * (1.0 / RWKV_HEAD)
    yn = yc * lax.rsqrt(var + RWKV_GN_EPS) * lnw_ref[...] + lnb_ref[...]
    o_rw = ((yn + bonus_ref[...]) * gg_ref[...]).astype(BF16)
    d = x_ref.shape[-1]
    m = (gate_ref[:, 0:d].astype(F32) * _dot(op_ref[...], wbp_ref[...])
         + gate_ref[:, d:2 * d].astype(F32) * _dot(om_ref[...], wbm_ref[...])
         + gate_ref[:, 2 * d:3 * d].astype(F32) * _dot(o_rw, wbr_ref[...]))
    o_ref[...] = x_ref[...] + g1_ref[...] * _dot(m.astype(BF16), wo_ref[...])


def _merge(tl, off, y, bonus, gg, o_pool, o_mla, gate, xc, mod, lnw, lnb, wbp, wbm, wbr, wo):
    b, _, d = xc.shape
    n_tiles = tl.ns - off
    out_spec = pl.BlockSpec((None, tl.tm, d), lambda bb, ss: (bb, ss, 0))
    mla_spec = pl.BlockSpec((None, tl.tm, o_mla.shape[-1]), lambda bb, ss: (bb, ss, 0))
    rw = y.shape[-1]
    yf_spec = pl.BlockSpec((None, None, tl.tm, rw), lambda bb, ss: (0, bb, ss + off, 0))
    yb_spec = pl.BlockSpec((None, None, tl.tm, rw), lambda bb, ss: (1, bb, _rev_tile(tl, ss + off), 0))
    return pl.pallas_call(
        _merge_kernel,
        grid=tl.grid(off),
        in_specs=[yf_spec, yb_spec, tl.tok(rw, off), tl.tok(rw, off), tl.tok(o_pool.shape[-1], off), mla_spec,
                  tl.tok(gate.shape[-1], off), tl.tok(d, off), tl.mod(2, off),
                  tl.const(lnw.shape), tl.const(lnb.shape)] + [tl.const(w.shape) for w in (wbp, wbm, wbr, wo)],
        out_specs=out_spec,
        out_shape=jax.ShapeDtypeStruct((b, n_tiles * tl.tm, d), F32),
        compiler_params=_params(("parallel", "parallel")),
        name="merge",
    )(y, y, bonus, gg, o_pool, o_mla, gate, xc, mod, lnw, lnb, wbp, wbm, wbr, wo)


def _mlp_kernel(x_ref, g_ref, *rest, parts, tm):
    mods, (w1_ref, w2_ref, o_ref, h_ref, acc_ref) = rest[:3 * parts], rest[3 * parts:]
    j = pl.program_id(1)

    @pl.when(j == 0)
    def _():
        for p in range(parts):
            rows = slice(p * tm, (p + 1) * tm)
            h_ref[rows, :] = _norm_mod(x_ref[rows, :], g_ref[...], mods[3 * p][...], mods[3 * p + 1][...]).astype(BF16)
        acc_ref[...] = jnp.zeros_like(acc_ref)

    a = jnp.maximum(_dot(h_ref[...], w1_ref[...]), 0.0)
    acc_ref[...] += _dot((a * a).astype(BF16), w2_ref[...])

    @pl.when(j == pl.num_programs(1) - 1)
    def _():
        for p in range(parts):
            rows = slice(p * tm, (p + 1) * tm)
            o_ref[rows, :] = x_ref[rows, :] + mods[3 * p + 2][...] * acc_ref[rows, :]


def _mlp(tl, x1, mod_off, g, mod, w1, w2):
    b, s1, d = x1.shape
    dff = w1.shape[1]
    tf = 1024 if dff % 1024 == 0 else dff
    tm = tl.tm
    per_batch = s1 // tm
    parts = next(p for p in (4, 2, 1) if (b * per_batch) % p == 0)
    nct, ctx_row = tl.nct, tl.ctx_row

    def mod_spec(j, p):
        def idx(ii, jj):
            sub = ii * parts + p
            bb, ss = sub // per_batch, sub % per_batch
            return (jnp.where(ss + mod_off < nct, ctx_row, bb), j, 0, 0)
        return pl.BlockSpec((None, None, 1, d), idx)

    mod_specs = [mod_spec(j, p) for p in range(parts) for j in (3, 4, 5)]
    x_spec = pl.BlockSpec((parts * tm, d), lambda ii, jj: (ii, 0))
    out = pl.pallas_call(
        functools.partial(_mlp_kernel, parts=parts, tm=tm),
        grid=(b * per_batch // parts, dff // tf),
        in_specs=[x_spec, pl.BlockSpec((1, d), lambda ii, jj: (0, 0))] + mod_specs
        + [pl.BlockSpec((d, tf), lambda ii, jj: (0, jj)), pl.BlockSpec((tf, d), lambda ii, jj: (jj, 0))],
        out_specs=x_spec,
        out_shape=jax.ShapeDtypeStruct((b * s1, d), F32),
        scratch_shapes=[pltpu.VMEM((parts * tm, d), BF16), pltpu.VMEM((parts * tm, d), F32)],
        compiler_params=_params(("parallel", "arbitrary")),
        name="mlp",
    )(x1.reshape(b * s1, d), g, *([mod] * (3 * parts)), w1, w2)
    return out.reshape(b, s1, d)


def _rope_tables(lc, l):
    rows = l // GRID_W
    row = jnp.repeat(jnp.arange(rows), GRID_W).astype(F32)
    col = jnp.tile(jnp.arange(GRID_W), rows).astype(F32)
    n_freq = MLA_ROPE // 4
    inv_freq = jnp.power(ROPE_BASE, -jnp.arange(n_freq, dtype=F32) / n_freq)
    ang = jnp.concatenate([row[:, None] * inv_freq, col[:, None] * inv_freq], axis=-1)
    cos = jnp.concatenate([jnp.ones((lc, MLA_ROPE // 2), F32), jnp.cos(ang)], axis=0)
    sin = jnp.concatenate([jnp.zeros((lc, MLA_ROPE // 2), F32), jnp.sin(ang)], axis=0)
    s = lc + l
    pad = jnp.zeros((s, LANES - MLA_NOPE - MLA_ROPE), F32)
    z16 = jnp.zeros((s, MLA_ROPE // 2), F32)
    zn = jnp.zeros((s, MLA_NOPE), F32)
    tc = jnp.concatenate([jnp.ones((s, MLA_NOPE), F32), cos, cos, pad], axis=1)
    ts1 = jnp.concatenate([zn, -sin, z16, pad], axis=1)
    ts2 = jnp.concatenate([zn, z16, sin, pad], axis=1)
    return tc, ts1, ts2


def _slot_cols(w, per_head):
    k = w.shape[0]
    w = w.reshape(k, MLA_HEADS, per_head)
    return jnp.pad(w, ((0, 0), (0, 0), (0, HEAD_SLOT - per_head))).reshape(k, MLA_HEADS * HEAD_SLOT)


def _block_diag(blocks):
    n = len(blocks)
    rows = []
    for i, blk in enumerate(blocks):
        rows.append(jnp.concatenate(
            [blk if j == i else jnp.zeros((blk.shape[0], blocks[j].shape[1]), blk.dtype) for j in range(n)], axis=1))
    return jnp.concatenate(rows, axis=0)


def _row(x):
    return x.reshape(1, -1).astype(F32)


def kernel(x, c, ctx, c_ctx, norm1_g, norm2_g, w_ada, b_ada, w_in, pool_w, pool_scale, mla_q_norm, mla_w_uq, mla_kv_norm, mla_w_ukv, qk_gain_q, qk_gain_k, rwkv_mu, rwkv_w0, rwkv_w2, rwkv_a0, rwkv_a2, rwkv_ka, rwkv_kk, rwkv_rk, rwkv_g2, rwkv_ln_w, rwkv_ln_b, w_br_pool, w_br_mla, w_br_rwkv, w_o, mlp_w1, mlp_w2):
    b, l, d = x.shape
    lc = ctx.shape[1]
    depth = w_in.shape[0]
    tl = _Tiles(b, lc, l, d)

    pool_width = pool_scale.shape[-1]
    q_rank = mla_q_norm.shape[-1]
    kv_rank = mla_kv_norm.shape[-1]
    rw_width = rwkv_kk.shape[-1]
    rw_in = rwkv_mu.shape[-1]

    rows = -(-(b + 1) // SUBLANES) * SUBLANES
    c_all = jnp.concatenate([c, c_ctx[None, :], jnp.zeros((rows - b - 1, d), F32)], axis=0)
    mod_all = _ada_mod(c_all, w_ada, b_ada).reshape(depth, rows, N_MOD, 1, d)
    tabs = _rope_tables(lc, l)

    tk = 256 if tl.s % 256 == 0 else LANES
    tq = 512
    while tq * tk > (MLA_HEADS // 2) * b * l or l % tq:
        tq //= 2
    assert tq >= 16

    xc = jnp.concatenate([ctx, x], axis=1)
    out = None
    for i in range(depth):
        need_ctx = i < depth - 1
        off = 0 if need_ctx else tl.nct
        mod = mod_all[i]

        wi = w_in[i].astype(BF16)
        o0 = 0
        wp = wi[:, o0:o0 + pool_width]; o0 += pool_width
        wq = wi[:, o0:o0 + q_rank]; o0 += q_rank
        wkv = wi[:, o0:o0 + kv_rank + MLA_ROPE]; o0 += kv_rank + MLA_ROPE
        wkv = jnp.pad(wkv, ((0, 0), (0, LANES - MLA_ROPE)))
        wrw = wi[:, o0:o0 + rw_in]; o0 += rw_in
        wg = wi[:, o0:]

        zp, zq, zkv, zrw, gate = _in_proj(tl, xc, _row(norm1_g[i]), mod, wp, wq, wkv, wrw, wg)

        pw_bd = _block_diag([pool_w[i, g] for g in range(pool_w.shape[1])]).astype(BF16)
        o_pool = _pool(tl, zp, pw_bd, _row(pool_scale[i]), off)

        wuq = _slot_cols(mla_w_uq[i], MLA_NOPE + MLA_ROPE).astype(BF16)
        wukv = mla_w_ukv[i].reshape(kv_rank, MLA_HEADS, MLA_NOPE + MLA_V)
        wuk = _slot_cols(wukv[:, :, :MLA_NOPE].reshape(kv_rank, -1), MLA_NOPE).astype(BF16)
        wuv = wukv[:, :, MLA_NOPE:].reshape(kv_rank, MLA_HEADS * MLA_V).astype(BF16)
        zpad = jnp.zeros((LANES - MLA_NOPE - MLA_ROPE,), F32)
        gq = _row(jnp.concatenate([qk_gain_q[i], zpad]))
        gk = _row(jnp.concatenate([qk_gain_k[i, :MLA_NOPE], jnp.zeros((LANES - MLA_NOPE,), F32)]))
        gkr = _row(jnp.concatenate([qk_gain_k[i, MLA_NOPE:], jnp.zeros((LANES - MLA_ROPE,), F32)]))
        smalls = (_row(mla_q_norm[i]), _row(mla_kv_norm[i]), gq, gk, gkr)
        q, k, v = _qkv_up(tl, zq, zkv, tabs, smalls, wuq, wuk, wuv)

        w2cat = _block_diag([rwkv_w2[i, 0], rwkv_w2[i, 1]]).astype(BF16)
        a2cat = _block_diag([rwkv_a2[i, 0], rwkv_a2[i, 1]]).astype(BF16)
        rsmalls = (rwkv_mu[i].astype(F32), _row(rwkv_kk[i]), _row(rwkv_w0[i]), w2cat, _row(rwkv_a0[i]), a2cat,
                   _row(rwkv_ka[i]), _row(rwkv_rk[i]), rwkv_g2[i].astype(BF16))
        r, vv, nkk, gg, bonus, wf, kf, bf, rb, vb, nkkb, wb, kb, bb = _rwkv_prep(tl, zrw, rsmalls)
        n_steps = b * (MLA_HEADS // 2) * (l // tq) * (tl.s // tk)
        ops_k, op_v = _scan_operands((r, vv, nkk, wf, kf, bf), (rb, vb, nkkb, wb, kb, bb), n_steps)
        o_mla_l, y = _attn_scan(q, k, v, ops_k, op_v, tq=tq, tk=tk, n_q=l)
        y = _scan_result(y, b, tl.s, rw_width)
        if need_ctx:
            o_mla_c = _attention(q, k, v, tq=tl.tm, q_off_tiles=l // tl.tm, n_q_tiles=tl.nct, n_keys=lc)
            o_mla = jnp.concatenate([o_mla_c, o_mla_l], axis=1)
        else:
            o_mla = o_mla_l

        x1 = _merge(tl, off, y, bonus, gg, o_pool, o_mla, gate, xc, mod,
                    _row(rwkv_ln_w[i]), _row(rwkv_ln_b[i]),
                    w_br_pool[i].astype(BF16), w_br_mla[i].astype(BF16), w_br_rwkv[i].astype(BF16),
                    w_o[i].astype(BF16))
        xc_next = _mlp(tl, x1, off, _row(norm2_g[i]), mod, mlp_w1[i].astype(BF16), mlp_w2[i].astype(BF16))
        if need_ctx:
            xc = xc_next
        else:
            out = xc_next
    return out
```

```python
import functools
import math

import jax
import jax.numpy as jnp
from jax import lax
from jax.experimental import pallas as pl
from jax.experimental.pallas import tpu as pltpu

F32 = jnp.float32
BF16 = jnp.bfloat16

NORM_EPS = 1e-6
RWKV_GN_EPS = 64e-5
GRID_W = 64
ROPE_BASE = 10000.0
POOL_HALF_WINDOWS = (1, 2, 4, 8)
N_MOD = 6
MLA_HEADS = 8
MLA_NOPE = 64
MLA_ROPE = 32
MLA_V = 64
RWKV_HEAD = 64
DECAY_RANK = 64
AAA_RANK = 64
GATE_RANK = 128

LANES = 128
SUBLANES = 8
HEAD_SLOT = LANES
HALO = SUBLANES
VMEM_LIMIT = 56 * 1024 * 1024

LOG2E = 1.4426950408889634


def _dot(a, b):
    return jnp.dot(a, b, preferred_element_type=F32)


def _sigmoid(x):
    return 1.0 / (1.0 + jnp.exp(-x))


def _rms(x, width):
    return lax.rsqrt(jnp.sum(x * x, axis=-1, keepdims=True) * (1.0 / width) + NORM_EPS)


def _norm_mod(x, g, shift, scale):
    return (x * _rms(x, x.shape[-1]) * g) * (1.0 + scale) + shift


def _lane(shape):
    return lax.broadcasted_iota(jnp.int32, shape, 1)


def _seg64_sum(x):
    cols = []
    for c in range(x.shape[1] // LANES):
        xc = x[:, c * LANES:(c + 1) * LANES]
        lo_m = _lane(xc.shape) < 64
        lo = jnp.sum(jnp.where(lo_m, xc, 0.0), axis=-1, keepdims=True)
        hi = jnp.sum(jnp.where(lo_m, 0.0, xc), axis=-1, keepdims=True)
        cols.append(jnp.where(lo_m, lo, hi))
    return cols[0] if len(cols) == 1 else jnp.concatenate(cols, axis=1)


def _params(sem, **flags):
    return pltpu.CompilerParams(dimension_semantics=sem, vmem_limit_bytes=VMEM_LIMIT, flags=flags or None)


class _Tiles:
    def __init__(self, batch, lc, l, d):
        self.batch, self.lc, self.l, self.d = batch, lc, l, d
        self.s = lc + l
        self.tm = 256 if (lc % 256 == 0 and l % 256 == 0) else 128
        assert lc % self.tm == 0 and l % self.tm == 0
        self.nct = lc // self.tm
        self.ns = self.s // self.tm
        self.ctx_row = batch

    def grid(self, off):
        return (self.batch, self.ns - off)

    def tok(self, width, off):
        return pl.BlockSpec((None, self.tm, width), lambda b, s: (b, s + off, 0))

    def tok_split(self, width, off, lat_off):
        nct = self.nct
        ctx = pl.BlockSpec((None, self.tm, width), lambda b, s: (b, jnp.minimum(s + off, nct - 1), 0))
        lat = pl.BlockSpec((None, self.tm, width), lambda b, s: (b, jnp.maximum(s + off - nct, 0) + lat_off, 0))
        return [ctx, lat]

    def pick(self, off, ctx_ref, lat_ref):
        return jnp.where(pl.program_id(1) + off < self.nct, ctx_ref[...], lat_ref[...])

    def halo_prev(self, width, off):
        r = self.tm // HALO
        return pl.BlockSpec((None, HALO, width), lambda b, s: (b, jnp.maximum((s + off) * r - 1, 0), 0))

    def halo_next(self, width, off):
        r = self.tm // HALO
        last = self.s // HALO - 1
        return pl.BlockSpec((None, HALO, width), lambda b, s: (b, jnp.minimum((s + off + 1) * r, last), 0))

    def mod(self, j, off):
        nct, ctx_row = self.nct, self.ctx_row
        return pl.BlockSpec((None, None, 1, self.d),
                            lambda b, s: (jnp.where(s + off < nct, ctx_row, b), j, 0, 0))

    def const(self, shape):
        nd = len(shape)
        return pl.BlockSpec(shape, lambda b, s: (0,) * nd)


def _ada_kernel(c_ref, w_ref, b_ref, o_ref):
    c = c_ref[...]
    s = (c * _sigmoid(c)).astype(BF16)
    o_ref[...] = _dot(s, w_ref[...].astype(BF16)) + b_ref[...]


def _ada_mod(c_all, w_ada, b_ada):
    depth, d, n = w_ada.shape
    rows = c_all.shape[0]
    tn = 1024
    return pl.pallas_call(
        _ada_kernel,
        grid=(depth, n // tn),
        in_specs=[pl.BlockSpec((rows, d), lambda i, j: (0, 0)),
                  pl.BlockSpec((None, d, tn), lambda i, j: (i, 0, j)),
                  pl.BlockSpec((None, 1, tn), lambda i, j: (i, 0, j))],
        out_specs=pl.BlockSpec((None, rows, tn), lambda i, j: (i, 0, j)),
        out_shape=jax.ShapeDtypeStruct((depth, rows, n), F32),
        compiler_params=_params(("parallel", "parallel")),
        name="ada_mod",
    )(c_all, w_ada, b_ada.reshape(depth, 1, n))


def _in_proj_kernel(xc_ref, xl_ref, g_ref, sh_ref, sc_ref, wp_ref, wq_ref, wkv_ref, wrw_ref, wg_ref,
                    zp_ref, zq_ref, zkv_ref, zrw_ref, gate_ref, *, tl):
    h = _norm_mod(tl.pick(0, xc_ref, xl_ref), g_ref[...], sh_ref[...], sc_ref[...]).astype(BF16)
    zp_ref[...] = _dot(h, wp_ref[...])
    zq_ref[...] = _dot(h, wq_ref[...])
    zkv_ref[...] = _dot(h, wkv_ref[...])
    zrw_ref[...] = _dot(h, wrw_ref[...])
    d = h.shape[-1]
    for c in range(wg_ref.shape[1] // d):
        gate_ref[:, c * d:(c + 1) * d] = _sigmoid(_dot(h, wg_ref[:, c * d:(c + 1) * d])).astype(BF16)


def _in_proj(tl, x_ctx, x_lat, lat_off, g, mod, wp, wq, wkv, wrw, wg):
    b, d = x_ctx.shape[0], x_ctx.shape[-1]
    widths = (wp.shape[1], wq.shape[1], wkv.shape[1], wrw.shape[1], wg.shape[1])
    dts = (F32, F32, F32, F32, BF16)
    return pl.pallas_call(
        functools.partial(_in_proj_kernel, tl=tl),
        grid=tl.grid(0),
        in_specs=tl.tok_split(d, 0, lat_off) + [tl.const((1, d)), tl.mod(0, 0), tl.mod(1, 0)]
        + [tl.const(w.shape) for w in (wp, wq, wkv, wrw, wg)],
        out_specs=[tl.tok(w, 0) for w in widths],
        out_shape=[jax.ShapeDtypeStruct((b, tl.s, w), dt) for w, dt in zip(widths, dts)],
        compiler_params=_params(("parallel", "parallel")),
        name="in_proj",
    )(x_ctx, x_lat, g, mod, mod, wp, wq, wkv, wrw, wg)


def _seq_flags(tl, off):
    s_abs = pl.program_id(1) + off
    is_ctx = s_abs < tl.nct
    first = jnp.logical_or(s_abs == 0, s_abs == tl.nct)
    last = jnp.logical_or(s_abs == tl.nct - 1, s_abs == tl.ns - 1)
    seq_len = jnp.where(is_ctx, tl.lc, tl.l)
    tile_in_seq = jnp.where(is_ctx, s_abs, s_abs - tl.nct)
    return first, last, seq_len, tile_in_seq


def _pool_kernel(u_ref, up_ref, un_ref, pw_ref, ps_ref, o_ref, *, tl, off):
    first, last, seq_len, tile_in_seq = _seq_flags(tl, off)
    tm = tl.tm
    u = u_ref[...]
    prev = jnp.where(first, 0.0, up_ref[...])
    nxt = jnp.where(last, 0.0, un_ref[...])
    e = jnp.concatenate([prev, u, nxt], axis=0)
    n = tm + 2 * HALO
    w2 = e + pltpu.roll(e, 1, 0)
    w4 = pltpu.roll(w2, n - 1, 0) + pltpu.roll(w2, 1, 0)
    w8 = pltpu.roll(w4, n - 2, 0) + pltpu.roll(w4, 2, 0)
    w16 = pltpu.roll(w8, n - 4, 0) + pltpu.roll(w8, 4, 0)
    sums = [w[HALO:HALO + tm] for w in (w2, w4, w8, w16)]
    width = u.shape[1]
    group = width // len(POOL_HALF_WINDOWS)
    lane = _lane((tm, width))
    pos = tile_in_seq * tm + lax.broadcasted_iota(jnp.int32, (tm, width), 0)
    total = sums[-1]
    half = jnp.full((tm, width), POOL_HALF_WINDOWS[-1], jnp.int32)
    for gi in range(len(POOL_HALF_WINDOWS) - 2, -1, -1):
        sel = lane < (gi + 1) * group
        total = jnp.where(sel, sums[gi], total)
        half = jnp.where(sel, POOL_HALF_WINDOWS[gi], half)
    cnt = jnp.minimum(pos + half, seq_len) - jnp.maximum(pos - half, 0)
    pooled = total / cnt.astype(F32) - u
    o_ref[...] = (_dot(pooled.astype(BF16), pw_ref[...]) * ps_ref[...]).astype(BF16)


def _pool(tl, zp, pw_bd, ps, off):
    b, s, w = zp.shape
    return pl.pallas_call(
        functools.partial(_pool_kernel, tl=tl, off=off),
        grid=tl.grid(off),
        in_specs=[tl.tok(w, off), tl.halo_prev(w, off), tl.halo_next(w, off),
                  tl.const(pw_bd.shape), tl.const(ps.shape)],
        out_specs=tl.tok(w, off),
        out_shape=jax.ShapeDtypeStruct((b, s, w), BF16),
        compiler_params=_params(("parallel", "parallel")),
        name="pool_mixer",
    )(zp, zp, zp, pw_bd, ps)


def _rope(x, tc, ts1, ts2):
    return x * tc + pltpu.roll(x, LANES - MLA_ROPE // 2, 1) * ts1 + pltpu.roll(x, MLA_ROPE // 2, 1) * ts2


def _head_inv_rms(x, seg):
    w = seg.shape[0]
    cols = [lax.rsqrt(_dot((xc * xc).astype(BF16), seg) + NORM_EPS)
            for xc in (x[:, c * w:(c + 1) * w] for c in range(x.shape[1] // w))]
    return jnp.concatenate(cols, axis=1)


def _qkv_kernel(zq_ref, zkv_ref, tc_ref, ts1_ref, ts2_ref, qng_ref, kvng_ref, gq_ref, gk_ref, gkr_ref, seg_ref,
                wuq_ref, wuk_ref, wuv_ref, q_ref, k_ref, v_ref):
    tc, ts1, ts2 = tc_ref[...], ts1_ref[...], ts2_ref[...]
    seg = seg_ref[...]

    zq = zq_ref[...]
    qc = (zq * _rms(zq, zq.shape[-1]) * qng_ref[...]).astype(BF16)
    q = _dot(qc, wuq_ref[...])
    q = q * _head_inv_rms(q, seg)
    q_scale = LOG2E * (MLA_NOPE + MLA_ROPE) ** -0.5
    for h in range(MLA_HEADS):
        qh = _rope(q[:, h * HEAD_SLOT:(h + 1) * HEAD_SLOT] * gq_ref[...], tc, ts1, ts2)
        q_ref[:, h * HEAD_SLOT:(h + 1) * HEAD_SLOT] = (qh * q_scale).astype(BF16)

    zkv = zkv_ref[...]
    kv_w = kvng_ref.shape[-1]
    kvc = zkv[:, :kv_w]
    kvn = (kvc * _rms(kvc, kv_w) * kvng_ref[...]).astype(BF16)
    kr = zkv[:, kv_w:kv_w + LANES]
    kr = kr * lax.rsqrt(jnp.sum(kr * kr, axis=-1, keepdims=True) * (1.0 / MLA_ROPE) + NORM_EPS) * gkr_ref[...]
    kr = _rope(pltpu.roll(kr, MLA_NOPE, 1), tc, ts1, ts2)
    kn = _dot(kvn, wuk_ref[...])
    kn = kn * _head_inv_rms(kn, seg)
    for h in range(MLA_HEADS):
        kh = kn[:, h * HEAD_SLOT:(h + 1) * HEAD_SLOT]
        k_ref[:, h * HEAD_SLOT:(h + 1) * HEAD_SLOT] = (kh * gk_ref[...] + kr).astype(BF16)
    v_ref[...] = _dot(kvn, wuv_ref[...]).astype(BF16)


def _qkv_up(tl, zq, zkv, tabs, smalls, wuq, wuk, wuv):
    b, s, _ = zq.shape
    tc, ts1, ts2 = tabs
    tab_spec = pl.BlockSpec((tl.tm, LANES), lambda bb, ss: (ss, 0))
    widths = (wuq.shape[1], wuk.shape[1], wuv.shape[1])
    nct, nlt = tl.nct, tl.ns - tl.nct
    q_spec = pl.BlockSpec((None, tl.tm, widths[0]), lambda bb, ss: (bb, jnp.where(ss < nct, nlt + ss, ss - nct), 0))
    return pl.pallas_call(
        _qkv_kernel,
        grid=tl.grid(0),
        in_specs=[tl.tok(zq.shape[-1], 0), tl.tok(zkv.shape[-1], 0), tab_spec, tab_spec, tab_spec]
        + [tl.const(a.shape) for a in smalls] + [tl.const(w.shape) for w in (wuq, wuk, wuv)],
        out_specs=[q_spec, tl.tok(widths[1], 0), tl.tok(widths[2], 0)],
        out_shape=[jax.ShapeDtypeStruct((b, s, w), BF16) for w in widths],
        compiler_params=_params(("parallel", "parallel")),
        name="qkv_up",
    )(zq, zkv, tc, ts1, ts2, *smalls, wuq, wuk, wuv)


def _attn_kernel(q_ref, k_ref, v_ref, o_ref):
    v = v_ref[...]
    outs = []
    for hh in range(2):
        q = q_ref[:, hh * HEAD_SLOT:(hh + 1) * HEAD_SLOT]
        k = k_ref[:, hh * HEAD_SLOT:(hh + 1) * HEAD_SLOT]
        s = lax.dot_general(q, k, (((1,), (1,)), ((), ())), preferred_element_type=F32)
        p = jnp.exp2(s - jnp.max(s, axis=-1, keepdims=True))
        denom = jnp.sum(p, axis=-1, keepdims=True)
        outs.append(_dot(p.astype(BF16), v) / denom)
    o_ref[...] = jnp.where(_lane(outs[0].shape) < MLA_V, outs[0], outs[1]).astype(BF16)


def _attention(q, k, v, *, tq, q_off_tiles, n_q_tiles, n_keys):
    b, _, qw = q.shape
    pairs = qw // (2 * HEAD_SLOT)
    return pl.pallas_call(
        _attn_kernel,
        grid=(b, pairs, n_q_tiles),
        in_specs=[pl.BlockSpec((None, tq, 2 * HEAD_SLOT), lambda bb, hp, i: (bb, i + q_off_tiles, hp)),
                  pl.BlockSpec((None, n_keys, 2 * HEAD_SLOT), lambda bb, hp, i: (bb, 0, hp)),
                  pl.BlockSpec((None, n_keys, 2 * MLA_V), lambda bb, hp, i: (bb, 0, hp))],
        out_specs=pl.BlockSpec((None, tq, 2 * MLA_V), lambda bb, hp, i: (bb, i, hp)),
        out_shape=jax.ShapeDtypeStruct((b, n_q_tiles * tq, pairs * 2 * MLA_V), BF16),
        compiler_params=_params(("parallel", "parallel", "arbitrary")),
        name="attention",
    )(q, k, v)


def _rev_rows(x):
    n = x.shape[0]
    row = lax.broadcasted_iota(jnp.int32, x.shape, 0)
    for sh in (1, 2, 4):
        x = jnp.where((row & sh) == 0, pltpu.roll(x, n - sh, 0), pltpu.roll(x, sh, 0))
    groups = [x[g * SUBLANES:(g + 1) * SUBLANES] for g in range(n // SUBLANES)]
    return jnp.concatenate(groups[::-1], axis=0)


def _rev_tile(tl, s_abs):
    return jnp.where(s_abs < tl.nct, tl.nct - 1 - s_abs, tl.ns - 1 + tl.nct - s_abs)


def _rwkv_prep_kernel(z_ref, zp_ref, zn_ref, mu_ref, kkw_ref, w0_ref, w2_ref, a0_ref, a2_ref, ka_ref,
                      rk_ref, g2_ref, anti_ref,
                      r_o, v_o, nkk_o, gg_o, bonus_o, wf_o, kf_o, bf_o, rb_o, vb_o, nkkb_o, wb_o, kb_o, bb_o,
                      *, tl, off):
    first, last, _, _ = _seq_flags(tl, off)
    tm = tl.tm
    z = z_ref[...]
    row = lax.broadcasted_iota(jnp.int32, z.shape, 0)
    prev_row = jnp.where(first, 0.0, zp_ref[HALO - 1:HALO, :])
    next_row = jnp.where(last, 0.0, zn_ref[0:1, :])
    z_prev = jnp.where(row == 0, prev_row, pltpu.roll(z, 1, 0))
    z_next = jnp.where(row == tm - 1, next_row, pltpu.roll(z, tm - 1, 0))
    zs = z + mu_ref[0:1, :] * (z_prev - z) + mu_ref[1:2, :] * (z_next - z)

    w = kkw_ref.shape[-1]
    r, k, v = zs[:, 0:w], zs[:, w:2 * w], zs[:, 2 * w:3 * w]
    o = 3 * w
    wd = zs[:, o:o + 2 * DECAY_RANK]
    ad = zs[:, o + 2 * DECAY_RANK:o + 2 * DECAY_RANK + 2 * AAA_RANK]
    gd = zs[:, o + 2 * DECAY_RANK + 2 * AAA_RANK:]

    kk = k * kkw_ref[...]
    kk = kk * lax.rsqrt(jnp.maximum(_seg64_sum(kk * kk), 1e-24))
    u = w0_ref[...] + _dot(jnp.tanh(wd).astype(BF16), w2_ref[...])
    decay = jnp.exp(-math.exp(-0.5) * _sigmoid(u))
    a = _sigmoid(a0_ref[...] + _dot(ad.astype(BF16), a2_ref[...]))
    ka = ka_ref[...]
    k_sum = jnp.zeros_like(k)
    anti = anti_ref[...]

    def rev16(t):
        return _dot(anti, t.astype(BF16)).astype(BF16)

    for d, (w_o, k_o, b_o) in enumerate(((wf_o, kf_o, bf_o), (wb_o, kb_o, bb_o))):
        a_d = a[:, d * w:(d + 1) * w]
        k_d = k * (1.0 + (a_d - 1.0) * ka[:, d * w:(d + 1) * w])
        dec = decay[:, d * w:(d + 1) * w]
        w_o[...] = _rev_rows(dec) if d else dec
        k_o[...] = rev16(k_d) if d else k_d.astype(BF16)
        b_o[...] = rev16(kk * a_d) if d else (kk * a_d).astype(BF16)
        k_sum = k_sum + k_d
    r_o[...] = r.astype(BF16)
    v_o[...] = v
    nkk_o[...] = (-kk).astype(BF16)
    rb_o[...] = rev16(r)
    vb_o[...] = _rev_rows(v)
    nkkb_o[...] = rev16(-kk)
    gg_o[...] = _dot(_sigmoid(gd).astype(BF16), g2_ref[...])
    bonus_o[...] = _seg64_sum(r * (0.5 * k_sum) * rk_ref[...]) * v


def _rwkv_prep(tl, zrw, smalls):
    b, s, win = zrw.shape
    w = smalls[1].shape[-1]
    n_fwd, n_bwd = 8, 6
    bwd_spec = pl.BlockSpec((None, tl.tm, w), lambda bb, ss: (bb, _rev_tile(tl, ss), 0))
    return pl.pallas_call(
        functools.partial(_rwkv_prep_kernel, tl=tl, off=0),
        grid=tl.grid(0),
        in_specs=[tl.tok(win, 0), tl.halo_prev(win, 0), tl.halo_next(win, 0)]
        + [tl.const(a.shape) for a in smalls],
        out_specs=[tl.tok(w, 0)] * n_fwd + [bwd_spec] * n_bwd,
        out_shape=[jax.ShapeDtypeStruct((b, s, w), dt) for dt in
                   (BF16, F32, BF16, F32, F32, F32, BF16, BF16, BF16, F32, BF16, F32, BF16, BF16)],
        compiler_params=_params(("parallel", "parallel")),
        name="rwkv_prep",
    )(zrw, zrw, zrw, *smalls)


def _scan_step(t, a_ref, w_ref, b_ref, k_ref, r_ref, v_ref, y_ref, s_ref):
    n_k = s_ref.shape[0]
    v = v_ref[t]
    acc = [jnp.zeros_like(v), jnp.zeros_like(v)]
    for kk in range(n_k):
        acc[kk % 2] = acc[kk % 2] + s_ref[kk] * a_ref[t, pl.ds(kk, 1), :]
    sa = acc[0] + acc[1]
    yacc = [jnp.zeros_like(v), jnp.zeros_like(v)]
    for kk in range(n_k):
        sn = (s_ref[kk] * w_ref[t, pl.ds(kk, 1), :] + sa * b_ref[t, pl.ds(kk, 1), :]
              + v * k_ref[t, pl.ds(kk, 1), :])
        s_ref[kk] = sn
        yacc[kk % 2] = yacc[kk % 2] + sn * r_ref[t, pl.ds(kk, 1), :]
    y_ref[t] = yacc[0] + yacc[1]


def _attn_scan_kernel(q_ref, k_ref, v_ref, a_ref, w_ref, b_ref, kk_ref, r_ref, vv_ref, o_ref, y_ref,
                      s_ref, m_ref, acc_ref, ops_ref, *, tk, unroll):
    first = jnp.logical_and(pl.program_id(0) == 0, jnp.logical_and(pl.program_id(1) == 0, pl.program_id(2) == 0))

    @pl.when(first)
    def _():
        s_ref[...] = jnp.zeros_like(s_ref)

    for i, ref in enumerate((a_ref, b_ref, kk_ref, r_ref)):
        ops_ref[i] = ref[...].astype(F32)
    a_ref, b_ref, kk_ref, r_ref = (ops_ref.at[i] for i in range(4))

    m_ref[...] = jnp.full_like(m_ref, -1e30)
    acc_ref[...] = jnp.zeros_like(acc_ref)
    lo_half = _lane((tk, 2 * MLA_V)) < MLA_V

    def block(j):
        start = j * tk if isinstance(j, int) else pl.multiple_of(j * tk, tk)
        vblk = v_ref[pl.ds(start, tk), :]
        v_ones = (jnp.where(lo_half, vblk, 1.0), jnp.where(lo_half, 1.0, vblk))
        for hh in range(2):
            q = q_ref[:, hh * HEAD_SLOT:(hh + 1) * HEAD_SLOT]
            kblk = k_ref[pl.ds(start, tk), hh * HEAD_SLOT:(hh + 1) * HEAD_SLOT]
            s = lax.dot_general(q, kblk, (((1,), (1,)), ((), ())), preferred_element_type=F32)
            m_old = m_ref[hh]
            m_new = jnp.maximum(m_old, jnp.max(s, axis=-1, keepdims=True))
            p = jnp.exp2(s - jnp.concatenate([m_new] * (tk // LANES), axis=1))
            acc_ref[hh] = jnp.exp2(m_old - m_new) * acc_ref[hh] + _dot(p.astype(BF16), v_ones[hh])
            m_ref[hh] = m_new
        _scan_step(j, a_ref, w_ref, b_ref, kk_ref, r_ref, vv_ref, y_ref, s_ref)

    nkv = vv_ref.shape[0]

    def body(jj, carry):
        for u in range(unroll):
            block(jj * unroll + u)
        return carry

    lax.fori_loop(0, nkv // unroll, body, 0)
    for j in range(nkv - nkv % unroll, nkv):
        block(j)
    a0, a1 = acc_ref[0], acc_ref[1]
    o = jnp.where(_lane(a0.shape) < MLA_V, a0 / pltpu.roll(a0, MLA_V, 1), a1 / pltpu.roll(a1, MLA_V, 1))
    o_ref[...] = o.astype(BF16)


def _attn_scan(q, k, v, ops_k, op_v, *, tq, tk, n_q):
    b, s, qw = q.shape[0], k.shape[1], q.shape[2]
    pairs = qw // (2 * HEAD_SLOT)
    nq_t, nkv = n_q // tq, s // tk
    n_k, nv = ops_k[0].shape[1], op_v.shape[1]
    assert op_v.shape[0] == b * pairs * nq_t * nkv and op_v.shape[2] == LANES

    def step_idx(bb, hp, i):
        return ((bb * pairs + hp) * nq_t + i, 0, 0)

    kspec = pl.BlockSpec((nkv, n_k, LANES), step_idx)
    vspec = pl.BlockSpec((nkv, nv, LANES), step_idx)
    return pl.pallas_call(
        functools.partial(_attn_scan_kernel, tk=tk, unroll=8),
        grid=(b, pairs, nq_t),
        in_specs=[pl.BlockSpec((None, tq, 2 * HEAD_SLOT), lambda bb, hp, i: (bb, i, hp)),
                  pl.BlockSpec((None, s, 2 * HEAD_SLOT), lambda bb, hp, i: (bb, 0, hp)),
                  pl.BlockSpec((None, s, 2 * MLA_V), lambda bb, hp, i: (bb, 0, hp))] + [kspec] * 5 + [vspec],
        out_specs=[pl.BlockSpec((None, tq, 2 * MLA_V), lambda bb, hp, i: (bb, i, hp)), vspec],
        out_shape=[jax.ShapeDtypeStruct((b, n_q, pairs * 2 * MLA_V), BF16),
                   jax.ShapeDtypeStruct(op_v.shape, F32)],
        scratch_shapes=[pltpu.VMEM((n_k, nv, LANES), F32), pltpu.VMEM((2, tq, LANES), F32),
                        pltpu.VMEM((2, tq, 2 * MLA_V), F32), pltpu.VMEM((4, nkv, n_k, LANES), F32)],
        compiler_params=_params(("arbitrary", "arbitrary", "arbitrary")),
        name="attn_scan",
    )(q, k, v, *ops_k, op_v)


def _scan_operands(fwd, bwd, n_steps):
    r, v, nkk, w_f, k_f, b_f = fwd
    r_b, v_b, nkk_b, w_b, k_b, b_b = bwd
    b, s, width = r.shape
    heads = width // RWKV_HEAD
    half = RWKV_HEAD // 2
    nl = 2 * 2 * b * heads
    assert nl <= LANES

    def pad(x, step_fill):
        x = jnp.pad(x, ((0, 0), (0, 0), (0, LANES - nl)))
        return jnp.pad(x, ((0, n_steps - s), (0, 0), (0, 0)), constant_values=step_fill)

    def k_layout(x_f, x_b, step_fill=0.0):
        z = jnp.stack([x_f, x_b]).reshape(2, b, s, heads, RWKV_HEAD)
        z = z.transpose(2, 4, 0, 1, 3).reshape(s, RWKV_HEAD, 2 * b * heads)
        return pad(jnp.concatenate([z, z], axis=-1), step_fill)

    def v_layout(x_f, x_b):
        z = jnp.stack([x_f, x_b]).reshape(2, b, s, heads, 2, half)
        return pad(z.transpose(2, 5, 4, 0, 1, 3).reshape(s, half, nl), 0.0)

    ops_k = [k_layout(nkk, nkk_b), k_layout(w_f, w_b, 1.0), k_layout(b_f, b_b), k_layout(k_f, k_b),
             k_layout(r, r_b)]
    return ops_k, v_layout(v, v_b)


def _scan_result(y, b, s, width):
    heads = width // RWKV_HEAD
    half = RWKV_HEAD // 2
    y = y[:s, :, :2 * 2 * b * heads]
    return y.reshape(s, half, 2, 2, b, heads).transpose(3, 4, 0, 5, 2, 1).reshape(2, b, s, width)


def _merge_kernel(yf_ref, yb_ref, bonus_ref, gg_ref, op_ref, omc_ref, oml_ref, gate_ref, xc_ref, xl_ref, g1_ref,
                  lnw_ref, lnb_ref, wbp_ref, wbm_ref, wbr_ref, wo_ref, o_ref, *, tl, off):
    y = yf_ref[...] + _rev_rows(yb_ref[...])
    mu = _seg64_sum(y) * (1.0 / RWKV_HEAD)
    yc = y - mu
    var = _seg64_sum(yc * yc) * (1.0 / RWKV_HEAD)
    yn = yc * lax.rsqrt(var + RWKV_GN_EPS) * lnw_ref[...] + lnb_ref[...]
    o_rw = ((yn + bonus_ref[...]) * gg_ref[...]).astype(BF16)
    x = tl.pick(off, xc_ref, xl_ref)
    d = x.shape[-1]
    m = (gate_ref[:, 0:d].astype(F32) * _dot(op_ref[...], wbp_ref[...])
         + gate_ref[:, d:2 * d].astype(F32) * _dot(tl.pick(off, omc_ref, oml_ref), wbm_ref[...])
         + gate_ref[:, 2 * d:3 * d].astype(F32) * _dot(o_rw, wbr_ref[...]))
    o_ref[...] = x + g1_ref[...] * _dot(m.astype(BF16), wo_ref[...])


def _merge(tl, off, y, bonus, gg, o_pool, o_mla_ctx, o_mla_lat, gate, x_ctx, x_lat, lat_off, mod,
           lnw, lnb, wbp, wbm, wbr, wo):
    b, d = x_ctx.shape[0], x_ctx.shape[-1]
    n_tiles = tl.ns - off
    out_spec = pl.BlockSpec((None, tl.tm, d), lambda bb, ss: (bb, ss, 0))
    rw = y.shape[-1]
    yf_spec = pl.BlockSpec((None, None, tl.tm, rw), lambda bb, ss: (0, bb, ss + off, 0))
    yb_spec = pl.BlockSpec((None, None, tl.tm, rw), lambda bb, ss: (1, bb, _rev_tile(tl, ss + off), 0))
    return pl.pallas_call(
        functools.partial(_merge_kernel, tl=tl, off=off),
        grid=tl.grid(off),
        in_specs=[yf_spec, yb_spec, tl.tok(rw, off), tl.tok(rw, off), tl.tok(o_pool.shape[-1], off)]
        + tl.tok_split(o_mla_lat.shape[-1], off, 0) + [tl.tok(gate.shape[-1], off)]
        + tl.tok_split(d, off, lat_off) + [tl.mod(2, off), tl.const(lnw.shape), tl.const(lnb.shape)]
        + [tl.const(w.shape) for w in (wbp, wbm, wbr, wo)],
        out_specs=out_spec,
        out_shape=jax.ShapeDtypeStruct((b, n_tiles * tl.tm, d), F32),
        compiler_params=_params(("parallel", "parallel")),
        name="merge",
    )(y, y, bonus, gg, o_pool, o_mla_ctx, o_mla_lat, gate, x_ctx, x_lat, mod, lnw, lnb, wbp, wbm, wbr, wo)


def _mlp_kernel(x_ref, g_ref, *rest, parts, tm):
    mods, (w1_ref, w2_ref, o_ref, h_ref, acc_ref) = rest[:3 * parts], rest[3 * parts:]
    j = pl.program_id(1)

    @pl.when(j == 0)
    def _():
        for p in range(parts):
            rows = slice(p * tm, (p + 1) * tm)
            h_ref[rows, :] = _norm_mod(x_ref[rows, :], g_ref[...], mods[3 * p][...], mods[3 * p + 1][...]).astype(BF16)
        acc_ref[...] = jnp.zeros_like(acc_ref)

    a = jnp.maximum(_dot(h_ref[...], w1_ref[...]), 0.0)
    acc_ref[...] += _dot((a * a).astype(BF16), w2_ref[...])

    @pl.when(j == pl.num_programs(1) - 1)
    def _():
        for p in range(parts):
            rows = slice(p * tm, (p + 1) * tm)
            o_ref[rows, :] = x_ref[rows, :] + mods[3 * p + 2][...] * acc_ref[rows, :]


def _mlp(tl, x1, mod_off, g, mod, w1, w2):
    b, s1, d = x1.shape
    dff = w1.shape[1]
    tf = 1024 if dff % 1024 == 0 else dff
    tm = tl.tm
    per_batch = s1 // tm
    parts = next(p for p in (4, 2, 1) if (b * per_batch) % p == 0)
    nct, ctx_row = tl.nct, tl.ctx_row

    def mod_spec(j, p):
        def idx(ii, jj):
            sub = ii * parts + p
            bb, ss = sub // per_batch, sub % per_batch
            return (jnp.where(ss + mod_off < nct, ctx_row, bb), j, 0, 0)
        return pl.BlockSpec((None, None, 1, d), idx)

    mod_specs = [mod_spec(j, p) for p in range(parts) for j in (3, 4, 5)]
    x_spec = pl.BlockSpec((parts * tm, d), lambda ii, jj: (ii, 0))
    out = pl.pallas_call(
        functools.partial(_mlp_kernel, parts=parts, tm=tm),
        grid=(b * per_batch // parts, dff // tf),
        in_specs=[x_spec, pl.BlockSpec((1, d), lambda ii, jj: (0, 0))] + mod_specs
        + [pl.BlockSpec((d, tf), lambda ii, jj: (0, jj)), pl.BlockSpec((tf, d), lambda ii, jj: (jj, 0))],
        out_specs=x_spec,
        out_shape=jax.ShapeDtypeStruct((b * s1, d), F32),
        scratch_shapes=[pltpu.VMEM((parts * tm, d), BF16), pltpu.VMEM((parts * tm, d), F32)],
        compiler_params=_params(("parallel", "arbitrary")),
        name="mlp",
    )(x1.reshape(b * s1, d), g, *([mod] * (3 * parts)), w1, w2)
    return out.reshape(b, s1, d)


def _rope_tables(lc, l):
    rows = l // GRID_W
    row = jnp.repeat(jnp.arange(rows), GRID_W).astype(F32)
    col = jnp.tile(jnp.arange(GRID_W), rows).astype(F32)
    n_freq = MLA_ROPE // 4
    inv_freq = jnp.power(ROPE_BASE, -jnp.arange(n_freq, dtype=F32) / n_freq)
    ang = jnp.concatenate([row[:, None] * inv_freq, col[:, None] * inv_freq], axis=-1)
    cos = jnp.concatenate([jnp.ones((lc, MLA_ROPE // 2), F32), jnp.cos(ang)], axis=0)
    sin = jnp.concatenate([jnp.zeros((lc, MLA_ROPE // 2), F32), jnp.sin(ang)], axis=0)
    s = lc + l
    pad = jnp.zeros((s, LANES - MLA_NOPE - MLA_ROPE), F32)
    z16 = jnp.zeros((s, MLA_ROPE // 2), F32)
    zn = jnp.zeros((s, MLA_NOPE), F32)
    tc = jnp.concatenate([jnp.ones((s, MLA_NOPE), F32), cos, cos, pad], axis=1)
    ts1 = jnp.concatenate([zn, -sin, z16, pad], axis=1)
    ts2 = jnp.concatenate([zn, z16, sin, pad], axis=1)
    return tc, ts1, ts2


def _slot_cols(w, per_head):
    k = w.shape[0]
    w = w.reshape(k, MLA_HEADS, per_head)
    return jnp.pad(w, ((0, 0), (0, 0), (0, HEAD_SLOT - per_head))).reshape(k, MLA_HEADS * HEAD_SLOT)


def _block_diag(blocks):
    n = len(blocks)
    rows = []
    for i, blk in enumerate(blocks):
        rows.append(jnp.concatenate(
            [blk if j == i else jnp.zeros((blk.shape[0], blocks[j].shape[1]), blk.dtype) for j in range(n)], axis=1))
    return jnp.concatenate(rows, axis=0)


def _row(x):
    return x.reshape(1, -1).astype(F32)


def kernel(x, c, ctx, c_ctx, norm1_g, norm2_g, w_ada, b_ada, w_in, pool_w, pool_scale, mla_q_norm, mla_w_uq, mla_kv_norm, mla_w_ukv, qk_gain_q, qk_gain_k, rwkv_mu, rwkv_w0, rwkv_w2, rwkv_a0, rwkv_a2, rwkv_ka, rwkv_kk, rwkv_rk, rwkv_g2, rwkv_ln_w, rwkv_ln_b, w_br_pool, w_br_mla, w_br_rwkv, w_o, mlp_w1, mlp_w2):
    b, l, d = x.shape
    lc = ctx.shape[1]
    depth = w_in.shape[0]
    tl = _Tiles(b, lc, l, d)

    pool_width = pool_scale.shape[-1]
    q_rank = mla_q_norm.shape[-1]
    kv_rank = mla_kv_norm.shape[-1]
    rw_width = rwkv_kk.shape[-1]
    rw_in = rwkv_mu.shape[-1]

    rows = -(-(b + 1) // SUBLANES) * SUBLANES
    c_all = jnp.concatenate([c, c_ctx[None, :], jnp.zeros((rows - b - 1, d), F32)], axis=0)
    mod_all = _ada_mod(c_all, w_ada, b_ada).reshape(depth, rows, N_MOD, 1, d)
    tabs = _rope_tables(lc, l)
    seg = jnp.zeros((HEAD_SLOT, HEAD_SLOT), F32)
    seg = seg.at[:MLA_NOPE, :MLA_NOPE].set(1.0 / MLA_NOPE)
    seg = seg.at[MLA_NOPE:MLA_NOPE + MLA_ROPE, MLA_NOPE:MLA_NOPE + MLA_ROPE].set(1.0 / MLA_ROPE)
    seg = _block_diag([seg, seg]).astype(BF16)
    anti = jnp.eye(tl.tm, dtype=BF16)[::-1]

    tk = 256 if tl.s % 256 == 0 else LANES
    tq = 512
    while tq * tk > (MLA_HEADS // 2) * b * l or l % tq:
        tq //= 2
    assert tq >= 16

    x_ctx, x_lat, lat_off = ctx, x, 0
    out = None
    for i in range(depth):
        need_ctx = i < depth - 1
        off = 0 if need_ctx else tl.nct
        mod = mod_all[i]

        wi = w_in[i].astype(BF16)
        o0 = 0
        wp = wi[:, o0:o0 + pool_width]; o0 += pool_width
        wq = wi[:, o0:o0 + q_rank]; o0 += q_rank
        wkv = wi[:, o0:o0 + kv_rank + MLA_ROPE]; o0 += kv_rank + MLA_ROPE
        wkv = jnp.pad(wkv, ((0, 0), (0, LANES - MLA_ROPE)))
        wrw = wi[:, o0:o0 + rw_in]; o0 += rw_in
        wg = wi[:, o0:]

        zp, zq, zkv, zrw, gate = _in_proj(tl, x_ctx, x_lat, lat_off, _row(norm1_g[i]), mod, wp, wq, wkv, wrw, wg)

        pw_bd = _block_diag([pool_w[i, g] for g in range(pool_w.shape[1])]).astype(BF16)
        o_pool = _pool(tl, zp, pw_bd, _row(pool_scale[i]), off)

        wuq = _slot_cols(mla_w_uq[i], MLA_NOPE + MLA_ROPE).astype(BF16)
        wukv = mla_w_ukv[i].reshape(kv_rank, MLA_HEADS, MLA_NOPE + MLA_V)
        wuk = _slot_cols(wukv[:, :, :MLA_NOPE].reshape(kv_rank, -1), MLA_NOPE).astype(BF16)
        wuv = wukv[:, :, MLA_NOPE:].reshape(kv_rank, MLA_HEADS * MLA_V).astype(BF16)
        zpad = jnp.zeros((LANES - MLA_NOPE - MLA_ROPE,), F32)
        gq = _row(jnp.concatenate([qk_gain_q[i], zpad]))
        gk = _row(jnp.concatenate([qk_gain_k[i, :MLA_NOPE], jnp.zeros((LANES - MLA_NOPE,), F32)]))
        gkr = _row(jnp.concatenate([qk_gain_k[i, MLA_NOPE:], jnp.zeros((LANES - MLA_ROPE,), F32)]))
        smalls = (_row(mla_q_norm[i]), _row(mla_kv_norm[i]), gq, gk, gkr, seg)
        q, k, v = _qkv_up(tl, zq, zkv, tabs, smalls, wuq, wuk, wuv)

        w2cat = _block_diag([rwkv_w2[i, 0], rwkv_w2[i, 1]]).astype(BF16)
        a2cat = _block_diag([rwkv_a2[i, 0], rwkv_a2[i, 1]]).astype(BF16)
        rsmalls = (rwkv_mu[i].astype(F32), _row(rwkv_kk[i]), _row(rwkv_w0[i]), w2cat, _row(rwkv_a0[i]), a2cat,
                   _row(rwkv_ka[i]), _row(rwkv_rk[i]), rwkv_g2[i].astype(BF16), anti)
        r, vv, nkk, gg, bonus, wf, kf, bf, rb, vb, nkkb, wb, kb, bb = _rwkv_prep(tl, zrw, rsmalls)
        n_steps = b * (MLA_HEADS // 2) * (l // tq) * (tl.s // tk)
        ops_k, op_v = _scan_operands((r, vv, nkk, wf, kf, bf), (rb, vb, nkkb, wb, kb, bb), n_steps)
        o_mla_l, y = _attn_scan(q, k, v, ops_k, op_v, tq=tq, tk=tk, n_q=l)
        y = _scan_result(y, b, tl.s, rw_width)
        if need_ctx:
            o_mla_c = _attention(q, k, v, tq=tl.tm, q_off_tiles=l // tl.tm, n_q_tiles=tl.nct, n_keys=lc)
        else:
            o_mla_c = o_mla_l

        x1 = _merge(tl, off, y, bonus, gg, o_pool, o_mla_c, o_mla_l, gate, x_ctx, x_lat, lat_off, mod,
                    _row(rwkv_ln_w[i]), _row(rwkv_ln_b[i]),
                    w_br_pool[i].astype(BF16), w_br_mla[i].astype(BF16), w_br_rwkv[i].astype(BF16),
                    w_o[i].astype(BF16))
        xc_next = _mlp(tl, x1, off, _row(norm2_g[i]), mod, mlp_w1[i].astype(BF16), mlp_w2[i].astype(BF16))
        if need_ctx:
            x_ctx, x_lat, lat_off = xc_next, xc_next, tl.nct
        else:
            out = xc_next
    return out
```

```python
import functools
import math

import jax
import jax.numpy as jnp
from jax import lax
from jax.experimental import pallas as pl
from jax.experimental.pallas import tpu as pltpu

F32 = jnp.float32
BF16 = jnp.bfloat16

NORM_EPS = 1e-6
RWKV_GN_EPS = 64e-5
GRID_W = 64
ROPE_BASE = 10000.0
POOL_HALF_WINDOWS = (1, 2, 4, 8)
N_MOD = 6
MLA_HEADS = 8
MLA_NOPE = 64
MLA_ROPE = 32
MLA_V = 64
RWKV_HEAD = 64
DECAY_RANK = 64
AAA_RANK = 64
GATE_RANK = 128

LANES = 128
SUBLANES = 8
HEAD_SLOT = LANES
HALO = SUBLANES
VMEM_LIMIT = 56 * 1024 * 1024

LOG2E = 1.4426950408889634


def _dot(a, b):
    return jnp.dot(a, b, preferred_element_type=F32)


def _sigmoid(x):
    return 1.0 / (1.0 + jnp.exp(-x))


def _rms(x, width):
    return lax.rsqrt(jnp.sum(x * x, axis=-1, keepdims=True) * (1.0 / width) + NORM_EPS)


def _norm_mod(x, g, shift, scale):
    return (x * _rms(x, x.shape[-1]) * g) * (1.0 + scale) + shift


def _lane(shape):
    return lax.broadcasted_iota(jnp.int32, shape, 1)


def _seg64_sum(x):
    cols = []
    for c in range(x.shape[1] // LANES):
        xc = x[:, c * LANES:(c + 1) * LANES]
        lo_m = _lane(xc.shape) < 64
        lo = jnp.sum(jnp.where(lo_m, xc, 0.0), axis=-1, keepdims=True)
        hi = jnp.sum(jnp.where(lo_m, 0.0, xc), axis=-1, keepdims=True)
        cols.append(jnp.where(lo_m, lo, hi))
    return cols[0] if len(cols) == 1 else jnp.concatenate(cols, axis=1)


def _params(sem, **flags):
    return pltpu.CompilerParams(dimension_semantics=sem, vmem_limit_bytes=VMEM_LIMIT, flags=flags or None)


class _Tiles:
    def __init__(self, batch, lc, l, d):
        self.batch, self.lc, self.l, self.d = batch, lc, l, d
        self.s = lc + l
        self.tm = 256 if (lc % 256 == 0 and l % 256 == 0) else 128
        assert lc % self.tm == 0 and l % self.tm == 0
        self.nct = lc // self.tm
        self.ns = self.s // self.tm
        self.ctx_row = batch

    def grid(self, off):
        return (self.batch, self.ns - off)

    def tok(self, width, off):
        return pl.BlockSpec((None, self.tm, width), lambda b, s: (b, s + off, 0))

    def tok_split(self, width, off, lat_off):
        nct = self.nct
        ctx = pl.BlockSpec((None, self.tm, width), lambda b, s: (b, jnp.minimum(s + off, nct - 1), 0))
        lat = pl.BlockSpec((None, self.tm, width), lambda b, s: (b, jnp.maximum(s + off - nct, 0) + lat_off, 0))
        return [ctx, lat]

    def pick(self, off, ctx_ref, lat_ref):
        return jnp.where(pl.program_id(1) + off < self.nct, ctx_ref[...], lat_ref[...])

    def halo_prev(self, width, off):
        r = self.tm // HALO
        return pl.BlockSpec((None, HALO, width), lambda b, s: (b, jnp.maximum((s + off) * r - 1, 0), 0))

    def halo_next(self, width, off):
        r = self.tm // HALO
        last = self.s // HALO - 1
        return pl.BlockSpec((None, HALO, width), lambda b, s: (b, jnp.minimum((s + off + 1) * r, last), 0))

    def mod(self, j, off):
        nct, ctx_row = self.nct, self.ctx_row
        return pl.BlockSpec((None, None, 1, self.d),
                            lambda b, s: (jnp.where(s + off < nct, ctx_row, b), j, 0, 0))

    def const(self, shape):
        nd = len(shape)
        return pl.BlockSpec(shape, lambda b, s: (0,) * nd)


def _ada_kernel(c_ref, w_ref, b_ref, o_ref):
    c = c_ref[...]
    s = (c * _sigmoid(c)).astype(BF16)
    o_ref[...] = _dot(s, w_ref[...].astype(BF16)) + b_ref[...]


def _ada_mod(c_all, w_ada, b_ada):
    depth, d, n = w_ada.shape
    rows = c_all.shape[0]
    tn = 1024
    return pl.pallas_call(
        _ada_kernel,
        grid=(depth, n // tn),
        in_specs=[pl.BlockSpec((rows, d), lambda i, j: (0, 0)),
                  pl.BlockSpec((None, d, tn), lambda i, j: (i, 0, j)),
                  pl.BlockSpec((None, 1, tn), lambda i, j: (i, 0, j))],
        out_specs=pl.BlockSpec((None, rows, tn), lambda i, j: (i, 0, j)),
        out_shape=jax.ShapeDtypeStruct((depth, rows, n), F32),
        compiler_params=_params(("parallel", "parallel")),
        name="ada_mod",
    )(c_all, w_ada, b_ada.reshape(depth, 1, n))


def _in_proj_kernel(xc_ref, xl_ref, g_ref, sh_ref, sc_ref, wp_ref, wq_ref, wkv_ref, wrw_ref, wg_ref,
                    zp_ref, zq_ref, zkv_ref, zrw_ref, gate_ref, *, tl):
    h = _norm_mod(tl.pick(0, xc_ref, xl_ref), g_ref[...], sh_ref[...], sc_ref[...]).astype(BF16)
    zp_ref[...] = _dot(h, wp_ref[...])
    zq_ref[...] = _dot(h, wq_ref[...])
    zkv_ref[...] = _dot(h, wkv_ref[...])
    zrw_ref[...] = _dot(h, wrw_ref[...])
    d = h.shape[-1]
    for c in range(wg_ref.shape[1] // d):
        gate_ref[:, c * d:(c + 1) * d] = _sigmoid(_dot(h, wg_ref[:, c * d:(c + 1) * d])).astype(BF16)


def _in_proj(tl, x_ctx, x_lat, lat_off, g, mod, wp, wq, wkv, wrw, wg):
    b, d = x_ctx.shape[0], x_ctx.shape[-1]
    widths = (wp.shape[1], wq.shape[1], wkv.shape[1], wrw.shape[1], wg.shape[1])
    dts = (F32, F32, F32, F32, BF16)
    return pl.pallas_call(
        functools.partial(_in_proj_kernel, tl=tl),
        grid=tl.grid(0),
        in_specs=tl.tok_split(d, 0, lat_off) + [tl.const((1, d)), tl.mod(0, 0), tl.mod(1, 0)]
        + [tl.const(w.shape) for w in (wp, wq, wkv, wrw, wg)],
        out_specs=[tl.tok(w, 0) for w in widths],
        out_shape=[jax.ShapeDtypeStruct((b, tl.s, w), dt) for w, dt in zip(widths, dts)],
        compiler_params=_params(("parallel", "parallel")),
        name="in_proj",
    )(x_ctx, x_lat, g, mod, mod, wp, wq, wkv, wrw, wg)


def _seq_flags(tl, off):
    s_abs = pl.program_id(1) + off
    is_ctx = s_abs < tl.nct
    first = jnp.logical_or(s_abs == 0, s_abs == tl.nct)
    last = jnp.logical_or(s_abs == tl.nct - 1, s_abs == tl.ns - 1)
    seq_len = jnp.where(is_ctx, tl.lc, tl.l)
    tile_in_seq = jnp.where(is_ctx, s_abs, s_abs - tl.nct)
    return first, last, seq_len, tile_in_seq


def _pool_kernel(u_ref, up_ref, un_ref, pw_ref, ps_ref, o_ref, *, tl, off):
    first, last, seq_len, tile_in_seq = _seq_flags(tl, off)
    tm = tl.tm
    u = u_ref[...]
    prev = jnp.where(first, 0.0, up_ref[...])
    nxt = jnp.where(last, 0.0, un_ref[...])
    e = jnp.concatenate([prev, u, nxt], axis=0)
    n = tm + 2 * HALO
    w2 = e + pltpu.roll(e, 1, 0)
    w4 = pltpu.roll(w2, n - 1, 0) + pltpu.roll(w2, 1, 0)
    w8 = pltpu.roll(w4, n - 2, 0) + pltpu.roll(w4, 2, 0)
    w16 = pltpu.roll(w8, n - 4, 0) + pltpu.roll(w8, 4, 0)
    sums = [w[HALO:HALO + tm] for w in (w2, w4, w8, w16)]
    width = u.shape[1]
    group = width // len(POOL_HALF_WINDOWS)
    lane = _lane((tm, width))
    pos = tile_in_seq * tm + lax.broadcasted_iota(jnp.int32, (tm, width), 0)
    total = sums[-1]
    half = jnp.full((tm, width), POOL_HALF_WINDOWS[-1], jnp.int32)
    for gi in range(len(POOL_HALF_WINDOWS) - 2, -1, -1):
        sel = lane < (gi + 1) * group
        total = jnp.where(sel, sums[gi], total)
        half = jnp.where(sel, POOL_HALF_WINDOWS[gi], half)
    cnt = jnp.minimum(pos + half, seq_len) - jnp.maximum(pos - half, 0)
    pooled = total / cnt.astype(F32) - u
    o_ref[...] = (_dot(pooled.astype(BF16), pw_ref[...]) * ps_ref[...]).astype(BF16)


def _pool(tl, zp, pw_bd, ps, off):
    b, s, w = zp.shape
    return pl.pallas_call(
        functools.partial(_pool_kernel, tl=tl, off=off),
        grid=tl.grid(off),
        in_specs=[tl.tok(w, off), tl.halo_prev(w, off), tl.halo_next(w, off),
                  tl.const(pw_bd.shape), tl.const(ps.shape)],
        out_specs=tl.tok(w, off),
        out_shape=jax.ShapeDtypeStruct((b, s, w), BF16),
        compiler_params=_params(("parallel", "parallel")),
        name="pool_mixer",
    )(zp, zp, zp, pw_bd, ps)


def _rope(x, tc, ts1, ts2):
    return x * tc + pltpu.roll(x, LANES - MLA_ROPE // 2, 1) * ts1 + pltpu.roll(x, MLA_ROPE // 2, 1) * ts2


def _head_inv_rms(x, seg):
    w = seg.shape[0]
    cols = [lax.rsqrt(_dot((xc * xc).astype(BF16), seg) + NORM_EPS)
            for xc in (x[:, c * w:(c + 1) * w] for c in range(x.shape[1] // w))]
    return jnp.concatenate(cols, axis=1)


def _qkv_kernel(zq_ref, zkv_ref, tc_ref, ts1_ref, ts2_ref, qng_ref, kvng_ref, gq_ref, gk_ref, gkr_ref, seg_ref,
                qpad_ref, kpad_ref, wuq_ref, wuk_ref, wuv_ref, q_ref, k_ref, v_ref):
    tc, ts1, ts2 = tc_ref[...], ts1_ref[...], ts2_ref[...]
    seg = seg_ref[...]

    zq = zq_ref[...]
    qc = (zq * _rms(zq, zq.shape[-1]) * qng_ref[...]).astype(BF16)
    q = _dot(qc, wuq_ref[...])
    q = q * _head_inv_rms(q, seg)
    q_scale = LOG2E * (MLA_NOPE + MLA_ROPE) ** -0.5
    for h in range(MLA_HEADS):
        qh = _rope(q[:, h * HEAD_SLOT:(h + 1) * HEAD_SLOT] * gq_ref[...], tc, ts1, ts2)
        q_ref[:, h * HEAD_SLOT:(h + 1) * HEAD_SLOT] = (qh * q_scale + qpad_ref[...]).astype(BF16)

    zkv = zkv_ref[...]
    kv_w = kvng_ref.shape[-1]
    kvc = zkv[:, :kv_w]
    kvn = (kvc * _rms(kvc, kv_w) * kvng_ref[...]).astype(BF16)
    kr = zkv[:, kv_w:kv_w + LANES]
    kr = kr * lax.rsqrt(jnp.sum(kr * kr, axis=-1, keepdims=True) * (1.0 / MLA_ROPE) + NORM_EPS) * gkr_ref[...]
    kr = _rope(pltpu.roll(kr, MLA_NOPE, 1), tc, ts1, ts2)
    kn = _dot(kvn, wuk_ref[...])
    kn = kn * _head_inv_rms(kn, seg)
    for h in range(MLA_HEADS):
        kh = kn[:, h * HEAD_SLOT:(h + 1) * HEAD_SLOT]
        k_ref[:, h * HEAD_SLOT:(h + 1) * HEAD_SLOT] = (kh * gk_ref[...] + kr + kpad_ref[...]).astype(BF16)
    v_ref[...] = _dot(kvn, wuv_ref[...]).astype(BF16)


def _qkv_up(tl, zq, zkv, tabs, smalls, wuq, wuk, wuv):
    b, s, _ = zq.shape
    tc, ts1, ts2 = tabs
    tab_spec = pl.BlockSpec((tl.tm, LANES), lambda bb, ss: (ss, 0))
    widths = (wuq.shape[1], wuk.shape[1], wuv.shape[1])
    nct, nlt = tl.nct, tl.ns - tl.nct
    q_spec = pl.BlockSpec((None, tl.tm, widths[0]), lambda bb, ss: (bb, jnp.where(ss < nct, nlt + ss, ss - nct), 0))
    return pl.pallas_call(
        _qkv_kernel,
        grid=tl.grid(0),
        in_specs=[tl.tok(zq.shape[-1], 0), tl.tok(zkv.shape[-1], 0), tab_spec, tab_spec, tab_spec]
        + [tl.const(a.shape) for a in smalls] + [tl.const(w.shape) for w in (wuq, wuk, wuv)],
        out_specs=[q_spec, tl.tok(widths[1], 0), tl.tok(widths[2], 0)],
        out_shape=[jax.ShapeDtypeStruct((b, s, w), BF16) for w in widths],
        compiler_params=_params(("parallel", "parallel")),
        name="qkv_up",
    )(zq, zkv, tc, ts1, ts2, *smalls, wuq, wuk, wuv)


def _attn_kernel(q_ref, k_ref, v_ref, o_ref):
    v = v_ref[...]
    outs = []
    for hh in range(2):
        q = q_ref[:, hh * HEAD_SLOT:(hh + 1) * HEAD_SLOT]
        k = k_ref[:, hh * HEAD_SLOT:(hh + 1) * HEAD_SLOT]
        s = lax.dot_general(q, k, (((1,), (1,)), ((), ())), preferred_element_type=F32)
        p = jnp.exp2(s - jnp.max(s, axis=-1, keepdims=True))
        denom = jnp.sum(p, axis=-1, keepdims=True)
        outs.append(_dot(p.astype(BF16), v) / denom)
    o_ref[...] = jnp.where(_lane(outs[0].shape) < MLA_V, outs[0], outs[1]).astype(BF16)


def _attention(q, k, v, *, tq, q_off_tiles, n_q_tiles, n_keys):
    b, _, qw = q.shape
    pairs = qw // (2 * HEAD_SLOT)
    return pl.pallas_call(
        _attn_kernel,
        grid=(b, pairs, n_q_tiles),
        in_specs=[pl.BlockSpec((None, tq, 2 * HEAD_SLOT), lambda bb, hp, i: (bb, i + q_off_tiles, hp)),
                  pl.BlockSpec((None, n_keys, 2 * HEAD_SLOT), lambda bb, hp, i: (bb, 0, hp)),
                  pl.BlockSpec((None, n_keys, 2 * MLA_V), lambda bb, hp, i: (bb, 0, hp))],
        out_specs=pl.BlockSpec((None, tq, 2 * MLA_V), lambda bb, hp, i: (bb, i, hp)),
        out_shape=jax.ShapeDtypeStruct((b, n_q_tiles * tq, pairs * 2 * MLA_V), BF16),
        compiler_params=_params(("parallel", "parallel", "arbitrary")),
        name="attention",
    )(q, k, v)


def _rev_rows(x):
    n = x.shape[0]
    row = lax.broadcasted_iota(jnp.int32, x.shape, 0)
    for sh in (1, 2, 4):
        x = jnp.where((row & sh) == 0, pltpu.roll(x, n - sh, 0), pltpu.roll(x, sh, 0))
    groups = [x[g * SUBLANES:(g + 1) * SUBLANES] for g in range(n // SUBLANES)]
    return jnp.concatenate(groups[::-1], axis=0)


def _rev_tile(tl, s_abs):
    return jnp.where(s_abs < tl.nct, tl.nct - 1 - s_abs, tl.ns - 1 + tl.nct - s_abs)


def _rwkv_prep_kernel(z_ref, zp_ref, zn_ref, mu_ref, kkw_ref, w0_ref, w2_ref, a0_ref, a2_ref, ka_ref,
                      rk_ref, g2_ref, anti_ref,
                      r_o, v_o, nkk_o, gg_o, bonus_o, wf_o, kf_o, bf_o, rb_o, vb_o, nkkb_o, wb_o, kb_o, bb_o,
                      *, tl, off):
    first, last, _, _ = _seq_flags(tl, off)
    tm = tl.tm
    z = z_ref[...]
    row = lax.broadcasted_iota(jnp.int32, z.shape, 0)
    prev_row = jnp.where(first, 0.0, zp_ref[HALO - 1:HALO, :])
    next_row = jnp.where(last, 0.0, zn_ref[0:1, :])
    z_prev = jnp.where(row == 0, prev_row, pltpu.roll(z, 1, 0))
    z_next = jnp.where(row == tm - 1, next_row, pltpu.roll(z, tm - 1, 0))
    zs = z + mu_ref[0:1, :] * (z_prev - z) + mu_ref[1:2, :] * (z_next - z)

    w = kkw_ref.shape[-1]
    r, k, v = zs[:, 0:w], zs[:, w:2 * w], zs[:, 2 * w:3 * w]
    o = 3 * w
    wd = zs[:, o:o + 2 * DECAY_RANK]
    ad = zs[:, o + 2 * DECAY_RANK:o + 2 * DECAY_RANK + 2 * AAA_RANK]
    gd = zs[:, o + 2 * DECAY_RANK + 2 * AAA_RANK:]

    kk = k * kkw_ref[...]
    kk = kk * lax.rsqrt(jnp.maximum(_seg64_sum(kk * kk), 1e-24))
    u = w0_ref[...] + _dot(jnp.tanh(wd).astype(BF16), w2_ref[...])
    decay = jnp.exp(-math.exp(-0.5) * _sigmoid(u))
    a = _sigmoid(a0_ref[...] + _dot(ad.astype(BF16), a2_ref[...]))
    ka = ka_ref[...]
    k_sum = jnp.zeros_like(k)
    anti = anti_ref[...]

    def rev16(t):
        return _dot(anti, t.astype(BF16)).astype(BF16)

    for d, (w_o, k_o, b_o) in enumerate(((wf_o, kf_o, bf_o), (wb_o, kb_o, bb_o))):
        a_d = a[:, d * w:(d + 1) * w]
        k_d = k * (1.0 + (a_d - 1.0) * ka[:, d * w:(d + 1) * w])
        dec = decay[:, d * w:(d + 1) * w]
        w_o[...] = _rev_rows(dec) if d else dec
        k_o[...] = rev16(k_d) if d else k_d.astype(BF16)
        b_o[...] = rev16(kk * a_d) if d else (kk * a_d).astype(BF16)
        k_sum = k_sum + k_d
    r_o[...] = r.astype(BF16)
    v_o[...] = v
    nkk_o[...] = (-kk).astype(BF16)
    rb_o[...] = rev16(r)
    vb_o[...] = _rev_rows(v)
    nkkb_o[...] = rev16(-kk)
    gg_o[...] = _dot(_sigmoid(gd).astype(BF16), g2_ref[...])
    bonus_o[...] = _seg64_sum(r * (0.5 * k_sum) * rk_ref[...]) * v


def _rwkv_prep(tl, zrw, smalls):
    b, s, win = zrw.shape
    w = smalls[1].shape[-1]
    n_fwd, n_bwd = 8, 6
    bwd_spec = pl.BlockSpec((None, tl.tm, w), lambda bb, ss: (bb, _rev_tile(tl, ss), 0))
    return pl.pallas_call(
        functools.partial(_rwkv_prep_kernel, tl=tl, off=0),
        grid=tl.grid(0),
        in_specs=[tl.tok(win, 0), tl.halo_prev(win, 0), tl.halo_next(win, 0)]
        + [tl.const(a.shape) for a in smalls],
        out_specs=[tl.tok(w, 0)] * n_fwd + [bwd_spec] * n_bwd,
        out_shape=[jax.ShapeDtypeStruct((b, s, w), dt) for dt in
                   (BF16, F32, BF16, F32, F32, F32, BF16, BF16, BF16, F32, BF16, F32, BF16, BF16)],
        compiler_params=_params(("parallel", "parallel")),
        name="rwkv_prep",
    )(zrw, zrw, zrw, *smalls)


def _zero_after(x):
    bits = pltpu.bitcast(x[:SUBLANES, :LANES].astype(F32), jnp.uint32)
    return pltpu.bitcast((bits >> 16) >> 16, F32)


def _scan_step(t, a_ref, w_ref, b_ref, k_ref, r_ref, v_ref, y_ref, s_ref, after):
    n_k = s_ref.shape[0]
    v = v_ref[t] + jnp.concatenate([after] * (v_ref.shape[1] // SUBLANES), axis=0)
    acc = [jnp.zeros_like(v), jnp.zeros_like(v)]
    for kk in range(n_k):
        acc[kk % 2] = acc[kk % 2] + s_ref[kk] * a_ref[t, pl.ds(kk, 1), :]
    sa = acc[0] + acc[1]
    yacc = [jnp.zeros_like(v), jnp.zeros_like(v)]
    for kk in range(n_k):
        sn = (s_ref[kk] * w_ref[t, pl.ds(kk, 1), :] + sa * b_ref[t, pl.ds(kk, 1), :]
              + v * k_ref[t, pl.ds(kk, 1), :])
        s_ref[kk] = sn
        yacc[kk % 2] = yacc[kk % 2] + sn * r_ref[t, pl.ds(kk, 1), :]
    y_ref[t] = yacc[0] + yacc[1]


def _attn_scan_kernel(q_ref, k_ref, v_ref, a_ref, w_ref, b_ref, kk_ref, r_ref, vv_ref, o_ref, y_ref,
                      s_ref, m_ref, acc_ref, ops_ref, *, tk, unroll, static_max):
    first = jnp.logical_and(pl.program_id(0) == 0, jnp.logical_and(pl.program_id(1) == 0, pl.program_id(2) == 0))

    @pl.when(first)
    def _():
        s_ref[...] = jnp.zeros_like(s_ref)

    for i, ref in enumerate((a_ref, b_ref, kk_ref, r_ref)):
        ops_ref[i] = ref[...].astype(F32)
    a_ref, b_ref, kk_ref, r_ref = (ops_ref.at[i] for i in range(4))

    m_ref[...] = jnp.full_like(m_ref, -1e30)
    acc_ref[...] = jnp.zeros_like(acc_ref)
    lo_half = _lane((tk, 2 * MLA_V)) < MLA_V

    def block(j, after):
        _scan_step(j, a_ref, w_ref, b_ref, kk_ref, r_ref, vv_ref, y_ref, s_ref, after)
        start = j * tk if isinstance(j, int) else pl.multiple_of(j * tk, tk)
        vblk = v_ref[pl.ds(start, tk), :]
        v_ones = (jnp.where(lo_half, vblk, 1.0), jnp.where(lo_half, 1.0, vblk))
        for hh in range(2):
            q = q_ref[:, hh * HEAD_SLOT:(hh + 1) * HEAD_SLOT]
            kblk = k_ref[pl.ds(start, tk), hh * HEAD_SLOT:(hh + 1) * HEAD_SLOT]
            s = lax.dot_general(q, kblk, (((1,), (1,)), ((), ())), preferred_element_type=F32)
            if static_max:
                new = acc_ref[hh] + _dot(jnp.exp2(s).astype(BF16), v_ones[hh])
            else:
                m_old = m_ref[hh]
                m_new = jnp.maximum(m_old, jnp.max(s, axis=-1, keepdims=True))
                p = jnp.exp2(s - jnp.concatenate([m_new] * (tk // LANES), axis=1))
                new = jnp.exp2(m_old - m_new) * acc_ref[hh] + _dot(p.astype(BF16), v_ones[hh])
                m_ref[hh] = m_new
            acc_ref[hh] = new
        return _zero_after(new)

    nkv = vv_ref.shape[0]

    def body(jj, after):
        for u in range(unroll):
            after = block(jj * unroll + u, after)
        return after

    after = lax.fori_loop(0, nkv // unroll, body, jnp.zeros((SUBLANES, LANES), F32))
    for j in range(nkv - nkv % unroll, nkv):
        after = block(j, after)
    a0, a1 = acc_ref[0], acc_ref[1]
    o = jnp.where(_lane(a0.shape) < MLA_V, a0 / pltpu.roll(a0, MLA_V, 1), a1 / pltpu.roll(a1, MLA_V, 1))
    o_ref[...] = o.astype(BF16)


def _score_bound(gain_q, gain_k):
    def norm2(g):
        return MLA_NOPE * jnp.max(jnp.square(g[:MLA_NOPE])) + MLA_ROPE * jnp.max(jnp.square(g[MLA_NOPE:]))
    q_scale = LOG2E * (MLA_NOPE + MLA_ROPE) ** -0.5
    return 1.02 * q_scale * jnp.sqrt(norm2(gain_q) * norm2(gain_k))


STATIC_SOFTMAX_MAX_BOUND = 50.0


def _attn_scan(q, k, v, ops_k, op_v, bound, *, tq, tk, n_q):
    b, s, qw = q.shape[0], k.shape[1], q.shape[2]
    pairs = qw // (2 * HEAD_SLOT)
    nq_t, nkv = n_q // tq, s // tk
    n_k, nv = ops_k[0].shape[1], op_v.shape[1]
    assert op_v.shape[0] == b * pairs * nq_t * nkv and op_v.shape[2] == LANES

    def step_idx(bb, hp, i):
        return ((bb * pairs + hp) * nq_t + i, 0, 0)

    kspec = pl.BlockSpec((nkv, n_k, LANES), step_idx)
    vspec = pl.BlockSpec((nkv, nv, LANES), step_idx)

    def call(static_max):
        return pl.pallas_call(
            functools.partial(_attn_scan_kernel, tk=tk, unroll=8, static_max=static_max),
            grid=(b, pairs, nq_t),
            in_specs=[pl.BlockSpec((None, tq, 2 * HEAD_SLOT), lambda bb, hp, i: (bb, i, hp)),
                      pl.BlockSpec((None, s, 2 * HEAD_SLOT), lambda bb, hp, i: (bb, 0, hp)),
                      pl.BlockSpec((None, s, 2 * MLA_V), lambda bb, hp, i: (bb, 0, hp))] + [kspec] * 5 + [vspec],
            out_specs=[pl.BlockSpec((None, tq, 2 * MLA_V), lambda bb, hp, i: (bb, i, hp)), vspec],
            out_shape=[jax.ShapeDtypeStruct((b, n_q, pairs * 2 * MLA_V), BF16),
                       jax.ShapeDtypeStruct(op_v.shape, F32)],
            scratch_shapes=[pltpu.VMEM((n_k, nv, LANES), F32), pltpu.VMEM((2, tq, LANES), F32),
                            pltpu.VMEM((2, tq, 2 * MLA_V), F32), pltpu.VMEM((4, nkv, n_k, LANES), F32)],
            compiler_params=_params(("arbitrary", "arbitrary", "arbitrary")),
            name="attn_scan_static_max" if static_max else "attn_scan",
        )

    return lax.cond(bound <= STATIC_SOFTMAX_MAX_BOUND, call(True), call(False), q, k, v, *ops_k, op_v)


def _scan_operands(fwd, bwd, n_steps):
    r, v, nkk, w_f, k_f, b_f = fwd
    r_b, v_b, nkk_b, w_b, k_b, b_b = bwd
    b, s, width = r.shape
    heads = width // RWKV_HEAD
    half = RWKV_HEAD // 2
    nl = 2 * 2 * b * heads
    assert nl <= LANES

    def pad(x, step_fill):
        x = jnp.pad(x, ((0, 0), (0, 0), (0, LANES - nl)))
        return jnp.pad(x, ((0, n_steps - s), (0, 0), (0, 0)), constant_values=step_fill)

    def k_layout(x_f, x_b, step_fill=0.0):
        z = jnp.stack([x_f, x_b]).reshape(2, b, s, heads, RWKV_HEAD)
        z = z.transpose(2, 4, 0, 1, 3).reshape(s, RWKV_HEAD, 2 * b * heads)
        return pad(jnp.concatenate([z, z], axis=-1), step_fill)

    def v_layout(x_f, x_b):
        z = jnp.stack([x_f, x_b]).reshape(2, b, s, heads, 2, half)
        return pad(z.transpose(2, 5, 4, 0, 1, 3).reshape(s, half, nl), 0.0)

    ops_k = [k_layout(nkk, nkk_b), k_layout(w_f, w_b, 1.0), k_layout(b_f, b_b), k_layout(k_f, k_b),
             k_layout(r, r_b)]
    return ops_k, v_layout(v, v_b)


def _scan_result(y, b, s, width):
    heads = width // RWKV_HEAD
    half = RWKV_HEAD // 2
    y = y[:s, :, :2 * 2 * b * heads]
    return y.reshape(s, half, 2, 2, b, heads).transpose(3, 4, 0, 5, 2, 1).reshape(2, b, s, width)


def _merge_kernel(yf_ref, yb_ref, bonus_ref, gg_ref, op_ref, omc_ref, oml_ref, gate_ref, xc_ref, xl_ref, g1_ref,
                  lnw_ref, lnb_ref, wbp_ref, wbm_ref, wbr_ref, wo_ref, o_ref, *, tl, off):
    y = yf_ref[...] + _rev_rows(yb_ref[...])
    mu = _seg64_sum(y) * (1.0 / RWKV_HEAD)
    yc = y - mu
    var = _seg64_sum(yc * yc) * (1.0 / RWKV_HEAD)
    yn = yc * lax.rsqrt(var + RWKV_GN_EPS) * lnw_ref[...] + lnb_ref[...]
    o_rw = ((yn + bonus_ref[...]) * gg_ref[...]).astype(BF16)
    x = tl.pick(off, xc_ref, xl_ref)
    d = x.shape[-1]
    m = (gate_ref[:, 0:d].astype(F32) * _dot(op_ref[...], wbp_ref[...])
         + gate_ref[:, d:2 * d].astype(F32) * _dot(tl.pick(off, omc_ref, oml_ref), wbm_ref[...])
         + gate_ref[:, 2 * d:3 * d].astype(F32) * _dot(o_rw, wbr_ref[...]))
    o_ref[...] = x + g1_ref[...] * _dot(m.astype(BF16), wo_ref[...])


def _merge(tl, off, y, bonus, gg, o_pool, o_mla_ctx, o_mla_lat, gate, x_ctx, x_lat, lat_off, mod,
           lnw, lnb, wbp, wbm, wbr, wo):
    b, d = x_ctx.shape[0], x_ctx.shape[-1]
    n_tiles = tl.ns - off
    out_spec = pl.BlockSpec((None, tl.tm, d), lambda bb, ss: (bb, ss, 0))
    rw = y.shape[-1]
    yf_spec = pl.BlockSpec((None, None, tl.tm, rw), lambda bb, ss: (0, bb, ss + off, 0))
    yb_spec = pl.BlockSpec((None, None, tl.tm, rw), lambda bb, ss: (1, bb, _rev_tile(tl, ss + off), 0))
    return pl.pallas_call(
        functools.partial(_merge_kernel, tl=tl, off=off),
        grid=tl.grid(off),
        in_specs=[yf_spec, yb_spec, tl.tok(rw, off), tl.tok(rw, off), tl.tok(o_pool.shape[-1], off)]
        + tl.tok_split(o_mla_lat.shape[-1], off, 0) + [tl.tok(gate.shape[-1], off)]
        + tl.tok_split(d, off, lat_off) + [tl.mod(2, off), tl.const(lnw.shape), tl.const(lnb.shape)]
        + [tl.const(w.shape) for w in (wbp, wbm, wbr, wo)],
        out_specs=out_spec,
        out_shape=jax.ShapeDtypeStruct((b, n_tiles * tl.tm, d), F32),
        compiler_params=_params(("parallel", "parallel")),
        name="merge",
    )(y, y, bonus, gg, o_pool, o_mla_ctx, o_mla_lat, gate, x_ctx, x_lat, mod, lnw, lnb, wbp, wbm, wbr, wo)


def _mlp_kernel(x_ref, g_ref, *rest, parts, tm):
    mods, (w1_ref, w2_ref, o_ref, h_ref, acc_ref) = rest[:3 * parts], rest[3 * parts:]
    j = pl.program_id(1)

    @pl.when(j == 0)
    def _():
        for p in range(parts):
            rows = slice(p * tm, (p + 1) * tm)
            h_ref[rows, :] = _norm_mod(x_ref[rows, :], g_ref[...], mods[3 * p][...], mods[3 * p + 1][...]).astype(BF16)
        acc_ref[...] = jnp.zeros_like(acc_ref)

    a = jnp.maximum(_dot(h_ref[...], w1_ref[...]), 0.0)
    acc_ref[...] += _dot((a * a).astype(BF16), w2_ref[...])

    @pl.when(j == pl.num_programs(1) - 1)
    def _():
        for p in range(parts):
            rows = slice(p * tm, (p + 1) * tm)
            o_ref[rows, :] = x_ref[rows, :] + mods[3 * p + 2][...] * acc_ref[rows, :]


def _mlp(tl, x1, mod_off, g, mod, w1, w2):
    b, s1, d = x1.shape
    dff = w1.shape[1]
    tf = 1024 if dff % 1024 == 0 else dff
    tm = tl.tm
    per_batch = s1 // tm
    parts = next(p for p in (4, 2, 1) if (b * per_batch) % p == 0)
    nct, ctx_row = tl.nct, tl.ctx_row

    def mod_spec(j, p):
        def idx(ii, jj):
            sub = ii * parts + p
            bb, ss = sub // per_batch, sub % per_batch
            return (jnp.where(ss + mod_off < nct, ctx_row, bb), j, 0, 0)
        return pl.BlockSpec((None, None, 1, d), idx)

    mod_specs = [mod_spec(j, p) for p in range(parts) for j in (3, 4, 5)]
    x_spec = pl.BlockSpec((parts * tm, d), lambda ii, jj: (ii, 0))
    out = pl.pallas_call(
        functools.partial(_mlp_kernel, parts=parts, tm=tm),
        grid=(b * per_batch // parts, dff // tf),
        in_specs=[x_spec, pl.BlockSpec((1, d), lambda ii, jj: (0, 0))] + mod_specs
        + [pl.BlockSpec((d, tf), lambda ii, jj: (0, jj)), pl.BlockSpec((tf, d), lambda ii, jj: (jj, 0))],
        out_specs=x_spec,
        out_shape=jax.ShapeDtypeStruct((b * s1, d), F32),
        scratch_shapes=[pltpu.VMEM((parts * tm, d), BF16), pltpu.VMEM((parts * tm, d), F32)],
        compiler_params=_params(("parallel", "arbitrary")),
        name="mlp",
    )(x1.reshape(b * s1, d), g, *([mod] * (3 * parts)), w1, w2)
    return out.reshape(b, s1, d)


def _rope_tables(lc, l):
    rows = l // GRID_W
    row = jnp.repeat(jnp.arange(rows), GRID_W).astype(F32)
    col = jnp.tile(jnp.arange(GRID_W), rows).astype(F32)
    n_freq = MLA_ROPE // 4
    inv_freq = jnp.power(ROPE_BASE, -jnp.arange(n_freq, dtype=F32) / n_freq)
    ang = jnp.concatenate([row[:, None] * inv_freq, col[:, None] * inv_freq], axis=-1)
    cos = jnp.concatenate([jnp.ones((lc, MLA_ROPE // 2), F32), jnp.cos(ang)], axis=0)
    sin = jnp.concatenate([jnp.zeros((lc, MLA_ROPE // 2), F32), jnp.sin(ang)], axis=0)
    s = lc + l
    pad = jnp.zeros((s, LANES - MLA_NOPE - MLA_ROPE), F32)
    z16 = jnp.zeros((s, MLA_ROPE // 2), F32)
    zn = jnp.zeros((s, MLA_NOPE), F32)
    tc = jnp.concatenate([jnp.ones((s, MLA_NOPE), F32), cos, cos, pad], axis=1)
    ts1 = jnp.concatenate([zn, -sin, z16, pad], axis=1)
    ts2 = jnp.concatenate([zn, z16, sin, pad], axis=1)
    return tc, ts1, ts2


def _slot_cols(w, per_head):
    k = w.shape[0]
    w = w.reshape(k, MLA_HEADS, per_head)
    return jnp.pad(w, ((0, 0), (0, 0), (0, HEAD_SLOT - per_head))).reshape(k, MLA_HEADS * HEAD_SLOT)


def _block_diag(blocks):
    n = len(blocks)
    rows = []
    for i, blk in enumerate(blocks):
        rows.append(jnp.concatenate(
            [blk if j == i else jnp.zeros((blk.shape[0], blocks[j].shape[1]), blk.dtype) for j in range(n)], axis=1))
    return jnp.concatenate(rows, axis=0)


def _row(x):
    return x.reshape(1, -1).astype(F32)


def kernel(x, c, ctx, c_ctx, norm1_g, norm2_g, w_ada, b_ada, w_in, pool_w, pool_scale, mla_q_norm, mla_w_uq, mla_kv_norm, mla_w_ukv, qk_gain_q, qk_gain_k, rwkv_mu, rwkv_w0, rwkv_w2, rwkv_a0, rwkv_a2, rwkv_ka, rwkv_kk, rwkv_rk, rwkv_g2, rwkv_ln_w, rwkv_ln_b, w_br_pool, w_br_mla, w_br_rwkv, w_o, mlp_w1, mlp_w2):
    b, l, d = x.shape
    lc = ctx.shape[1]
    depth = w_in.shape[0]
    tl = _Tiles(b, lc, l, d)

    pool_width = pool_scale.shape[-1]
    q_rank = mla_q_norm.shape[-1]
    kv_rank = mla_kv_norm.shape[-1]
    rw_width = rwkv_kk.shape[-1]
    rw_in = rwkv_mu.shape[-1]

    rows = -(-(b + 1) // SUBLANES) * SUBLANES
    c_all = jnp.concatenate([c, c_ctx[None, :], jnp.zeros((rows - b - 1, d), F32)], axis=0)
    mod_all = _ada_mod(c_all, w_ada, b_ada).reshape(depth, rows, N_MOD, 1, d)
    tabs = _rope_tables(lc, l)
    seg = jnp.zeros((HEAD_SLOT, HEAD_SLOT), F32)
    seg = seg.at[:MLA_NOPE, :MLA_NOPE].set(1.0 / MLA_NOPE)
    seg = seg.at[MLA_NOPE:MLA_NOPE + MLA_ROPE, MLA_NOPE:MLA_NOPE + MLA_ROPE].set(1.0 / MLA_ROPE)
    seg = _block_diag([seg, seg]).astype(BF16)
    anti = jnp.eye(tl.tm, dtype=BF16)[::-1]

    tk = 256 if tl.s % 256 == 0 else LANES
    tq = 512
    while tq * tk > (MLA_HEADS // 2) * b * l or l % tq:
        tq //= 2
    assert tq >= 16

    x_ctx, x_lat, lat_off = ctx, x, 0
    out = None
    for i in range(depth):
        need_ctx = i < depth - 1
        off = 0 if need_ctx else tl.nct
        mod = mod_all[i]

        wi = w_in[i].astype(BF16)
        o0 = 0
        wp = wi[:, o0:o0 + pool_width]; o0 += pool_width
        wq = wi[:, o0:o0 + q_rank]; o0 += q_rank
        wkv = wi[:, o0:o0 + kv_rank + MLA_ROPE]; o0 += kv_rank + MLA_ROPE
        wkv = jnp.pad(wkv, ((0, 0), (0, LANES - MLA_ROPE)))
        wrw = wi[:, o0:o0 + rw_in]; o0 += rw_in
        wg = wi[:, o0:]

        zp, zq, zkv, zrw, gate = _in_proj(tl, x_ctx, x_lat, lat_off, _row(norm1_g[i]), mod, wp, wq, wkv, wrw, wg)

        pw_bd = _block_diag([pool_w[i, g] for g in range(pool_w.shape[1])]).astype(BF16)
        o_pool = _pool(tl, zp, pw_bd, _row(pool_scale[i]), off)

        wuq = _slot_cols(mla_w_uq[i], MLA_NOPE + MLA_ROPE).astype(BF16)
        wukv = mla_w_ukv[i].reshape(kv_rank, MLA_HEADS, MLA_NOPE + MLA_V)
        wuk = _slot_cols(wukv[:, :, :MLA_NOPE].reshape(kv_rank, -1), MLA_NOPE).astype(BF16)
        wuv = wukv[:, :, MLA_NOPE:].reshape(kv_rank, MLA_HEADS * MLA_V).astype(BF16)
        zpad = jnp.zeros((LANES - MLA_NOPE - MLA_ROPE,), F32)
        gq = _row(jnp.concatenate([qk_gain_q[i], zpad]))
        gk = _row(jnp.concatenate([qk_gain_k[i, :MLA_NOPE], jnp.zeros((LANES - MLA_NOPE,), F32)]))
        gkr = _row(jnp.concatenate([qk_gain_k[i, MLA_NOPE:], jnp.zeros((LANES - MLA_ROPE,), F32)]))
        bound = _score_bound(qk_gain_q[i], qk_gain_k[i])
        spare = (jnp.arange(LANES) == MLA_NOPE + MLA_ROPE).astype(F32)[None, :]
        smalls = (_row(mla_q_norm[i]), _row(mla_kv_norm[i]), gq, gk, gkr, seg, spare, -bound * spare)
        q, k, v = _qkv_up(tl, zq, zkv, tabs, smalls, wuq, wuk, wuv)

        w2cat = _block_diag([rwkv_w2[i, 0], rwkv_w2[i, 1]]).astype(BF16)
        a2cat = _block_diag([rwkv_a2[i, 0], rwkv_a2[i, 1]]).astype(BF16)
        rsmalls = (rwkv_mu[i].astype(F32), _row(rwkv_kk[i]), _row(rwkv_w0[i]), w2cat, _row(rwkv_a0[i]), a2cat,
                   _row(rwkv_ka[i]), _row(rwkv_rk[i]), rwkv_g2[i].astype(BF16), anti)
        r, vv, nkk, gg, bonus, wf, kf, bf, rb, vb, nkkb, wb, kb, bb = _rwkv_prep(tl, zrw, rsmalls)
        n_steps = b * (MLA_HEADS // 2) * (l // tq) * (tl.s // tk)
        ops_k, op_v = _scan_operands((r, vv, nkk, wf, kf, bf), (rb, vb, nkkb, wb, kb, bb), n_steps)
        o_mla_l, y = _attn_scan(q, k, v, ops_k, op_v, bound, tq=tq, tk=tk, n_q=l)
        y = _scan_result(y, b, tl.s, rw_width)
        if need_ctx:
            o_mla_c = _attention(q, k, v, tq=tl.tm, q_off_tiles=l // tl.tm, n_q_tiles=tl.nct, n_keys=lc)
        else:
            o_mla_c = o_mla_l

        x1 = _merge(tl, off, y, bonus, gg, o_pool, o_mla_c, o_mla_l, gate, x_ctx, x_lat, lat_off, mod,
                    _row(rwkv_ln_w[i]), _row(rwkv_ln_b[i]),
                    w_br_pool[i].astype(BF16), w_br_mla[i].astype(BF16), w_br_rwkv[i].astype(BF16),
                    w_o[i].astype(BF16))
        xc_next = _mlp(tl, x1, off, _row(norm2_g[i]), mod, mlp_w1[i].astype(BF16), mlp_w2[i].astype(BF16))
        if need_ctx:
            x_ctx, x_lat, lat_off = xc_next, xc_next, tl.nct
        else:
            out = xc_next
    return out
```

```python
import functools
import math

import jax
import jax.numpy as jnp
from jax import lax
from jax.experimental import pallas as pl
from jax.experimental.pallas import tpu as pltpu

F32 = jnp.float32
BF16 = jnp.bfloat16

NORM_EPS = 1e-6
RWKV_GN_EPS = 64e-5
GRID_W = 64
ROPE_BASE = 10000.0
POOL_HALF_WINDOWS = (1, 2, 4, 8)
N_MOD = 6
MLA_HEADS = 8
MLA_NOPE = 64
MLA_ROPE = 32
MLA_V = 64
RWKV_HEAD = 64
DECAY_RANK = 64
AAA_RANK = 64
GATE_RANK = 128

LANES = 128
SUBLANES = 8
HEAD_SLOT = LANES
HALO = SUBLANES
VMEM_LIMIT = 56 * 1024 * 1024

LOG2E = 1.4426950408889634


def _dot(a, b):
    return jnp.dot(a, b, preferred_element_type=F32)


def _sigmoid(x):
    return 1.0 / (1.0 + jnp.exp(-x))


def _rms(x, width):
    return lax.rsqrt(jnp.sum(x * x, axis=-1, keepdims=True) * (1.0 / width) + NORM_EPS)


def _norm_mod(x, g, shift, scale):
    return (x * _rms(x, x.shape[-1]) * g) * (1.0 + scale) + shift


def _lane(shape):
    return lax.broadcasted_iota(jnp.int32, shape, 1)


def _seg64_sum(x):
    cols = []
    for c in range(x.shape[1] // LANES):
        xc = x[:, c * LANES:(c + 1) * LANES]
        lo_m = _lane(xc.shape) < 64
        lo = jnp.sum(jnp.where(lo_m, xc, 0.0), axis=-1, keepdims=True)
        hi = jnp.sum(jnp.where(lo_m, 0.0, xc), axis=-1, keepdims=True)
        cols.append(jnp.where(lo_m, lo, hi))
    return cols[0] if len(cols) == 1 else jnp.concatenate(cols, axis=1)


def _params(sem, **flags):
    return pltpu.CompilerParams(dimension_semantics=sem, vmem_limit_bytes=VMEM_LIMIT, flags=flags or None)


class _Tiles:
    def __init__(self, batch, lc, l, d):
        self.batch, self.lc, self.l, self.d = batch, lc, l, d
        self.s = lc + l
        self.tm = 256 if (lc % 256 == 0 and l % 256 == 0) else 128
        assert lc % self.tm == 0 and l % self.tm == 0
        self.nct = lc // self.tm
        self.ns = self.s // self.tm
        self.ctx_row = batch

    def grid(self, off):
        return (self.batch, self.ns - off)

    def tok(self, width, off):
        return pl.BlockSpec((None, self.tm, width), lambda b, s: (b, s + off, 0))

    def tok_split(self, width, off, lat_off):
        nct = self.nct
        ctx = pl.BlockSpec((None, self.tm, width), lambda b, s: (b, jnp.minimum(s + off, nct - 1), 0))
        lat = pl.BlockSpec((None, self.tm, width), lambda b, s: (b, jnp.maximum(s + off - nct, 0) + lat_off, 0))
        return [ctx, lat]

    def pick(self, off, ctx_ref, lat_ref):
        return jnp.where(pl.program_id(1) + off < self.nct, ctx_ref[...], lat_ref[...])

    def halo_prev(self, width, off):
        r = self.tm // HALO
        return pl.BlockSpec((None, HALO, width), lambda b, s: (b, jnp.maximum((s + off) * r - 1, 0), 0))

    def halo_next(self, width, off):
        r = self.tm // HALO
        last = self.s // HALO - 1
        return pl.BlockSpec((None, HALO, width), lambda b, s: (b, jnp.minimum((s + off + 1) * r, last), 0))

    def mod(self, j, off):
        nct, ctx_row = self.nct, self.ctx_row
        return pl.BlockSpec((None, None, 1, self.d),
                            lambda b, s: (jnp.where(s + off < nct, ctx_row, b), j, 0, 0))

    def const(self, shape):
        nd = len(shape)
        return pl.BlockSpec(shape, lambda b, s: (0,) * nd)


def _ada_kernel(c_ref, w_ref, b_ref, o_ref):
    c = c_ref[...]
    s = (c * _sigmoid(c)).astype(BF16)
    o_ref[...] = _dot(s, w_ref[...].astype(BF16)) + b_ref[...]


def _ada_mod(c_all, w_ada, b_ada):
    depth, d, n = w_ada.shape
    rows = c_all.shape[0]
    tn = 1024
    return pl.pallas_call(
        _ada_kernel,
        grid=(depth, n // tn),
        in_specs=[pl.BlockSpec((rows, d), lambda i, j: (0, 0)),
                  pl.BlockSpec((None, d, tn), lambda i, j: (i, 0, j)),
                  pl.BlockSpec((None, 1, tn), lambda i, j: (i, 0, j))],
        out_specs=pl.BlockSpec((None, rows, tn), lambda i, j: (i, 0, j)),
        out_shape=jax.ShapeDtypeStruct((depth, rows, n), F32),
        compiler_params=_params(("parallel", "parallel")),
        name="ada_mod",
    )(c_all, w_ada, b_ada.reshape(depth, 1, n))


def _in_proj_kernel(xc_ref, xl_ref, g_ref, sh_ref, sc_ref, wp_ref, wq_ref, wkv_ref, wrw_ref, wg_ref,
                    zp_ref, zq_ref, zkv_ref, zrw_ref, gate_ref, *, tl):
    h = _norm_mod(tl.pick(0, xc_ref, xl_ref), g_ref[...], sh_ref[...], sc_ref[...]).astype(BF16)
    zp_ref[...] = _dot(h, wp_ref[...])
    zq_ref[...] = _dot(h, wq_ref[...])
    zkv_ref[...] = _dot(h, wkv_ref[...])
    zrw_ref[...] = _dot(h, wrw_ref[...])
    d = h.shape[-1]
    for c in range(wg_ref.shape[1] // d):
        gate_ref[:, c * d:(c + 1) * d] = _sigmoid(_dot(h, wg_ref[:, c * d:(c + 1) * d])).astype(BF16)


def _in_proj(tl, x_ctx, x_lat, lat_off, g, mod, wp, wq, wkv, wrw, wg):
    b, d = x_ctx.shape[0], x_ctx.shape[-1]
    widths = (wp.shape[1], wq.shape[1], wkv.shape[1], wrw.shape[1], wg.shape[1])
    dts = (F32, F32, F32, F32, BF16)
    return pl.pallas_call(
        functools.partial(_in_proj_kernel, tl=tl),
        grid=tl.grid(0),
        in_specs=tl.tok_split(d, 0, lat_off) + [tl.const((1, d)), tl.mod(0, 0), tl.mod(1, 0)]
        + [tl.const(w.shape) for w in (wp, wq, wkv, wrw, wg)],
        out_specs=[tl.tok(w, 0) for w in widths],
        out_shape=[jax.ShapeDtypeStruct((b, tl.s, w), dt) for w, dt in zip(widths, dts)],
        compiler_params=_params(("parallel", "parallel")),
        name="in_proj",
    )(x_ctx, x_lat, g, mod, mod, wp, wq, wkv, wrw, wg)


def _seq_flags(tl, off):
    s_abs = pl.program_id(1) + off
    is_ctx = s_abs < tl.nct
    first = jnp.logical_or(s_abs == 0, s_abs == tl.nct)
    last = jnp.logical_or(s_abs == tl.nct - 1, s_abs == tl.ns - 1)
    seq_len = jnp.where(is_ctx, tl.lc, tl.l)
    tile_in_seq = jnp.where(is_ctx, s_abs, s_abs - tl.nct)
    return first, last, seq_len, tile_in_seq


def _pool_kernel(u_ref, up_ref, un_ref, pw_ref, ps_ref, o_ref, *, tl, off):
    first, last, seq_len, tile_in_seq = _seq_flags(tl, off)
    tm = tl.tm
    u = u_ref[...]
    prev = jnp.where(first, 0.0, up_ref[...])
    nxt = jnp.where(last, 0.0, un_ref[...])
    e = jnp.concatenate([prev, u, nxt], axis=0)
    n = tm + 2 * HALO
    w2 = e + pltpu.roll(e, 1, 0)
    w4 = pltpu.roll(w2, n - 1, 0) + pltpu.roll(w2, 1, 0)
    w8 = pltpu.roll(w4, n - 2, 0) + pltpu.roll(w4, 2, 0)
    w16 = pltpu.roll(w8, n - 4, 0) + pltpu.roll(w8, 4, 0)
    sums = [w[HALO:HALO + tm] for w in (w2, w4, w8, w16)]
    width = u.shape[1]
    group = width // len(POOL_HALF_WINDOWS)
    lane = _lane((tm, width))
    pos = tile_in_seq * tm + lax.broadcasted_iota(jnp.int32, (tm, width), 0)
    total = sums[-1]
    half = jnp.full((tm, width), POOL_HALF_WINDOWS[-1], jnp.int32)
    for gi in range(len(POOL_HALF_WINDOWS) - 2, -1, -1):
        sel = lane < (gi + 1) * group
        total = jnp.where(sel, sums[gi], total)
        half = jnp.where(sel, POOL_HALF_WINDOWS[gi], half)
    cnt = jnp.minimum(pos + half, seq_len) - jnp.maximum(pos - half, 0)
    pooled = total / cnt.astype(F32) - u
    o_ref[...] = (_dot(pooled.astype(BF16), pw_ref[...]) * ps_ref[...]).astype(BF16)


def _pool(tl, zp, pw_bd, ps, off):
    b, s, w = zp.shape
    return pl.pallas_call(
        functools.partial(_pool_kernel, tl=tl, off=off),
        grid=tl.grid(off),
        in_specs=[tl.tok(w, off), tl.halo_prev(w, off), tl.halo_next(w, off),
                  tl.const(pw_bd.shape), tl.const(ps.shape)],
        out_specs=tl.tok(w, off),
        out_shape=jax.ShapeDtypeStruct((b, s, w), BF16),
        compiler_params=_params(("parallel", "parallel")),
        name="pool_mixer",
    )(zp, zp, zp, pw_bd, ps)


def _rope(x, tc, ts1, ts2):
    return x * tc + pltpu.roll(x, LANES - MLA_ROPE // 2, 1) * ts1 + pltpu.roll(x, MLA_ROPE // 2, 1) * ts2


def _head_inv_rms(x, seg):
    w = seg.shape[0]
    cols = [lax.rsqrt(_dot((xc * xc).astype(BF16), seg) + NORM_EPS)
            for xc in (x[:, c * w:(c + 1) * w] for c in range(x.shape[1] // w))]
    return jnp.concatenate(cols, axis=1)


def _qkv_kernel(zq_ref, zkv_ref, tc_ref, ts1_ref, ts2_ref, qng_ref, kvng_ref, gq_ref, gk_ref, gkr_ref, seg_ref,
                qpad_ref, kpad_ref, wuq_ref, wuk_ref, wuv_ref, q_ref, k_ref, v_ref):
    tc, ts1, ts2 = tc_ref[...], ts1_ref[...], ts2_ref[...]
    seg = seg_ref[...]

    zq = zq_ref[...]
    qc = (zq * _rms(zq, zq.shape[-1]) * qng_ref[...]).astype(BF16)
    q = _dot(qc, wuq_ref[...])
    q = q * _head_inv_rms(q, seg)
    q_scale = LOG2E * (MLA_NOPE + MLA_ROPE) ** -0.5
    for h in range(MLA_HEADS):
        qh = _rope(q[:, h * HEAD_SLOT:(h + 1) * HEAD_SLOT] * gq_ref[...], tc, ts1, ts2)
        q_ref[:, h * HEAD_SLOT:(h + 1) * HEAD_SLOT] = (qh * q_scale + qpad_ref[...]).astype(BF16)

    zkv = zkv_ref[...]
    kv_w = kvng_ref.shape[-1]
    kvc = zkv[:, :kv_w]
    kvn = (kvc * _rms(kvc, kv_w) * kvng_ref[...]).astype(BF16)
    kr = zkv[:, kv_w:kv_w + LANES]
    kr = kr * lax.rsqrt(jnp.sum(kr * kr, axis=-1, keepdims=True) * (1.0 / MLA_ROPE) + NORM_EPS) * gkr_ref[...]
    kr = _rope(pltpu.roll(kr, MLA_NOPE, 1), tc, ts1, ts2)
    kn = _dot(kvn, wuk_ref[...])
    kn = kn * _head_inv_rms(kn, seg)
    for h in range(MLA_HEADS):
        kh = kn[:, h * HEAD_SLOT:(h + 1) * HEAD_SLOT]
        k_ref[:, h * HEAD_SLOT:(h + 1) * HEAD_SLOT] = (kh * gk_ref[...] + kr + kpad_ref[...]).astype(BF16)
    v_ref[...] = _dot(kvn, wuv_ref[...]).astype(BF16)


def _qkv_up(tl, zq, zkv, tabs, smalls, wuq, wuk, wuv):
    b, s, _ = zq.shape
    tc, ts1, ts2 = tabs
    tab_spec = pl.BlockSpec((tl.tm, LANES), lambda bb, ss: (ss, 0))
    widths = (wuq.shape[1], wuk.shape[1], wuv.shape[1])
    nct, nlt = tl.nct, tl.ns - tl.nct
    q_spec = pl.BlockSpec((None, tl.tm, widths[0]), lambda bb, ss: (bb, jnp.where(ss < nct, nlt + ss, ss - nct), 0))
    return pl.pallas_call(
        _qkv_kernel,
        grid=tl.grid(0),
        in_specs=[tl.tok(zq.shape[-1], 0), tl.tok(zkv.shape[-1], 0), tab_spec, tab_spec, tab_spec]
        + [tl.const(a.shape) for a in smalls] + [tl.const(w.shape) for w in (wuq, wuk, wuv)],
        out_specs=[q_spec, tl.tok(widths[1], 0), tl.tok(widths[2], 0)],
        out_shape=[jax.ShapeDtypeStruct((b, s, w), BF16) for w in widths],
        compiler_params=_params(("parallel", "parallel")),
        name="qkv_up",
    )(zq, zkv, tc, ts1, ts2, *smalls, wuq, wuk, wuv)


def _attn_kernel(q_ref, k_ref, v_ref, o_ref):
    v = v_ref[...]
    outs = []
    for hh in range(2):
        q = q_ref[:, hh * HEAD_SLOT:(hh + 1) * HEAD_SLOT]
        k = k_ref[:, hh * HEAD_SLOT:(hh + 1) * HEAD_SLOT]
        s = lax.dot_general(q, k, (((1,), (1,)), ((), ())), preferred_element_type=F32)
        p = jnp.exp2(s - jnp.max(s, axis=-1, keepdims=True))
        denom = jnp.sum(p, axis=-1, keepdims=True)
        outs.append(_dot(p.astype(BF16), v) / denom)
    o_ref[...] = jnp.where(_lane(outs[0].shape) < MLA_V, outs[0], outs[1]).astype(BF16)


def _attention(q, k, v, *, tq, q_off_tiles, n_q_tiles, n_keys):
    b, _, qw = q.shape
    pairs = qw // (2 * HEAD_SLOT)
    return pl.pallas_call(
        _attn_kernel,
        grid=(b, pairs, n_q_tiles),
        in_specs=[pl.BlockSpec((None, tq, 2 * HEAD_SLOT), lambda bb, hp, i: (bb, i + q_off_tiles, hp)),
                  pl.BlockSpec((None, n_keys, 2 * HEAD_SLOT), lambda bb, hp, i: (bb, 0, hp)),
                  pl.BlockSpec((None, n_keys, 2 * MLA_V), lambda bb, hp, i: (bb, 0, hp))],
        out_specs=pl.BlockSpec((None, tq, 2 * MLA_V), lambda bb, hp, i: (bb, i, hp)),
        out_shape=jax.ShapeDtypeStruct((b, n_q_tiles * tq, pairs * 2 * MLA_V), BF16),
        compiler_params=_params(("parallel", "parallel", "arbitrary")),
        name="attention",
    )(q, k, v)


def _rev_rows(x):
    n = x.shape[0]
    row = lax.broadcasted_iota(jnp.int32, x.shape, 0)
    for sh in (1, 2, 4):
        x = jnp.where((row & sh) == 0, pltpu.roll(x, n - sh, 0), pltpu.roll(x, sh, 0))
    groups = [x[g * SUBLANES:(g + 1) * SUBLANES] for g in range(n // SUBLANES)]
    return jnp.concatenate(groups[::-1], axis=0)


def _rev_tile(tl, s_abs):
    return jnp.where(s_abs < tl.nct, tl.nct - 1 - s_abs, tl.ns - 1 + tl.nct - s_abs)


def _rwkv_prep_kernel(z_ref, zp_ref, zn_ref, mu_ref, kkw_ref, w0_ref, w2_ref, a0_ref, a2_ref, ka_ref,
                      rk_ref, g2_ref, anti_ref,
                      r_o, v_o, nkk_o, gg_o, bonus_o, wf_o, kf_o, bf_o, rb_o, vb_o, nkkb_o, wb_o, kb_o, bb_o,
                      *, tl, off):
    first, last, _, _ = _seq_flags(tl, off)
    tm = tl.tm
    z = z_ref[...]
    row = lax.broadcasted_iota(jnp.int32, z.shape, 0)
    prev_row = jnp.where(first, 0.0, zp_ref[HALO - 1:HALO, :])
    next_row = jnp.where(last, 0.0, zn_ref[0:1, :])
    z_prev = jnp.where(row == 0, prev_row, pltpu.roll(z, 1, 0))
    z_next = jnp.where(row == tm - 1, next_row, pltpu.roll(z, tm - 1, 0))
    zs = z + mu_ref[0:1, :] * (z_prev - z) + mu_ref[1:2, :] * (z_next - z)

    w = kkw_ref.shape[-1]
    r, k, v = zs[:, 0:w], zs[:, w:2 * w], zs[:, 2 * w:3 * w]
    o = 3 * w
    wd = zs[:, o:o + 2 * DECAY_RANK]
    ad = zs[:, o + 2 * DECAY_RANK:o + 2 * DECAY_RANK + 2 * AAA_RANK]
    gd = zs[:, o + 2 * DECAY_RANK + 2 * AAA_RANK:]

    kk = k * kkw_ref[...]
    kk = kk * lax.rsqrt(jnp.maximum(_seg64_sum(kk * kk), 1e-24))
    u = w0_ref[...] + _dot(jnp.tanh(wd).astype(BF16), w2_ref[...])
    decay = jnp.exp(-math.exp(-0.5) * _sigmoid(u))
    a = _sigmoid(a0_ref[...] + _dot(ad.astype(BF16), a2_ref[...]))
    ka = ka_ref[...]
    k_sum = jnp.zeros_like(k)
    anti = anti_ref[...]

    def rev16(t):
        return _dot(anti, t.astype(BF16)).astype(BF16)

    for d, (w_o, k_o, b_o) in enumerate(((wf_o, kf_o, bf_o), (wb_o, kb_o, bb_o))):
        a_d = a[:, d * w:(d + 1) * w]
        k_d = k * (1.0 + (a_d - 1.0) * ka[:, d * w:(d + 1) * w])
        dec = decay[:, d * w:(d + 1) * w]
        w_o[...] = _rev_rows(dec) if d else dec
        k_o[...] = rev16(k_d) if d else k_d.astype(BF16)
        b_o[...] = rev16(kk * a_d) if d else (kk * a_d).astype(BF16)
        k_sum = k_sum + k_d
    r_o[...] = r.astype(BF16)
    v_o[...] = v
    nkk_o[...] = (-kk).astype(BF16)
    rb_o[...] = rev16(r)
    vb_o[...] = _rev_rows(v)
    nkkb_o[...] = rev16(-kk)
    gg_o[...] = _dot(_sigmoid(gd).astype(BF16), g2_ref[...])
    bonus_o[...] = _seg64_sum(r * (0.5 * k_sum) * rk_ref[...]) * v


def _rwkv_prep(tl, zrw, smalls):
    b, s, win = zrw.shape
    w = smalls[1].shape[-1]
    n_fwd, n_bwd = 8, 6
    bwd_spec = pl.BlockSpec((None, tl.tm, w), lambda bb, ss: (bb, _rev_tile(tl, ss), 0))
    return pl.pallas_call(
        functools.partial(_rwkv_prep_kernel, tl=tl, off=0),
        grid=tl.grid(0),
        in_specs=[tl.tok(win, 0), tl.halo_prev(win, 0), tl.halo_next(win, 0)]
        + [tl.const(a.shape) for a in smalls],
        out_specs=[tl.tok(w, 0)] * n_fwd + [bwd_spec] * n_bwd,
        out_shape=[jax.ShapeDtypeStruct((b, s, w), dt) for dt in
                   (BF16, F32, BF16, F32, F32, F32, BF16, BF16, BF16, F32, BF16, F32, BF16, BF16)],
        compiler_params=_params(("parallel", "parallel")),
        name="rwkv_prep",
    )(zrw, zrw, zrw, *smalls)


def _zero_after(x):
    bits = pltpu.bitcast(x[:SUBLANES, :LANES].astype(F32), jnp.uint32)
    return pltpu.bitcast((bits >> 16) >> 16, F32)


def _scan_step(t, a_ref, w_ref, b_ref, k_ref, r_ref, v_ref, y_ref, s_ref, after):
    n_k = s_ref.shape[0]
    v = v_ref[t] + jnp.concatenate([after] * (v_ref.shape[1] // SUBLANES), axis=0)
    acc = [jnp.zeros_like(v), jnp.zeros_like(v)]
    for kk in range(n_k):
        acc[kk % 2] = acc[kk % 2] + s_ref[kk] * a_ref[t, pl.ds(kk, 1), :]
    sa = acc[0] + acc[1]
    yacc = [jnp.zeros_like(v), jnp.zeros_like(v)]
    for kk in range(n_k):
        sn = (s_ref[kk] * w_ref[t, pl.ds(kk, 1), :] + sa * b_ref[t, pl.ds(kk, 1), :]
              + v * k_ref[t, pl.ds(kk, 1), :])
        s_ref[kk] = sn
        yacc[kk % 2] = yacc[kk % 2] + sn * r_ref[t, pl.ds(kk, 1), :]
    y_ref[t] = yacc[0] + yacc[1]


def _attn_scan_kernel(static_ref, q_ref, k_ref, v_ref, a_ref, w_ref, b_ref, kk_ref, r_ref, vv_ref, o_ref, y_ref,
                      s_ref, m_ref, acc_ref, ops_ref, *, tk, unroll):
    first = jnp.logical_and(pl.program_id(0) == 0, jnp.logical_and(pl.program_id(1) == 0, pl.program_id(2) == 0))

    @pl.when(first)
    def _():
        s_ref[...] = jnp.zeros_like(s_ref)

    for i, ref in enumerate((a_ref, b_ref, kk_ref, r_ref)):
        ops_ref[i] = ref[...].astype(F32)
    a_ref, b_ref, kk_ref, r_ref = (ops_ref.at[i] for i in range(4))

    m_ref[...] = jnp.full_like(m_ref, -1e30)
    acc_ref[...] = jnp.zeros_like(acc_ref)
    lo_half = _lane((tk, 2 * MLA_V)) < MLA_V
    nkv = vv_ref.shape[0]

    def block(j, after, static_max):
        _scan_step(j, a_ref, w_ref, b_ref, kk_ref, r_ref, vv_ref, y_ref, s_ref, after)
        start = j * tk if isinstance(j, int) else pl.multiple_of(j * tk, tk)
        vblk = v_ref[pl.ds(start, tk), :]
        v_ones = (jnp.where(lo_half, vblk, 1.0), jnp.where(lo_half, 1.0, vblk))
        for hh in range(2):
            q = q_ref[:, hh * HEAD_SLOT:(hh + 1) * HEAD_SLOT]
            kblk = k_ref[pl.ds(start, tk), hh * HEAD_SLOT:(hh + 1) * HEAD_SLOT]
            s = lax.dot_general(q, kblk, (((1,), (1,)), ((), ())), preferred_element_type=F32)
            if static_max:
                new = acc_ref[hh] + _dot(jnp.exp2(s).astype(BF16), v_ones[hh])
            else:
                m_old = m_ref[hh]
                m_new = jnp.maximum(m_old, jnp.max(s, axis=-1, keepdims=True))
                p = jnp.exp2(s - jnp.concatenate([m_new] * (tk // LANES), axis=1))
                new = jnp.exp2(m_old - m_new) * acc_ref[hh] + _dot(p.astype(BF16), v_ones[hh])
                m_ref[hh] = m_new
            acc_ref[hh] = new
        return _zero_after(new)

    def run(static_max):
        def body(jj, after):
            for u in range(unroll):
                after = block(jj * unroll + u, after, static_max)
            return after

        after = lax.fori_loop(0, nkv // unroll, body, jnp.zeros((SUBLANES, LANES), F32))
        for j in range(nkv - nkv % unroll, nkv):
            after = block(j, after, static_max)

    use_static = static_ref[0] != 0
    pl.when(use_static)(functools.partial(run, True))
    pl.when(jnp.logical_not(use_static))(functools.partial(run, False))
    a0, a1 = acc_ref[0], acc_ref[1]
    o = jnp.where(_lane(a0.shape) < MLA_V, a0 / pltpu.roll(a0, MLA_V, 1), a1 / pltpu.roll(a1, MLA_V, 1))
    o_ref[...] = o.astype(BF16)


def _score_bound(gain_q, gain_k):
    def norm2(g):
        return MLA_NOPE * jnp.max(jnp.square(g[:MLA_NOPE])) + MLA_ROPE * jnp.max(jnp.square(g[MLA_NOPE:]))
    q_scale = LOG2E * (MLA_NOPE + MLA_ROPE) ** -0.5
    return 1.02 * q_scale * jnp.sqrt(norm2(gain_q) * norm2(gain_k))


STATIC_SOFTMAX_MAX_BOUND = 50.0


def _attn_scan(q, k, v, ops_k, op_v, bound, *, tq, tk, n_q):
    b, s, qw = q.shape[0], k.shape[1], q.shape[2]
    pairs = qw // (2 * HEAD_SLOT)
    nq_t, nkv = n_q // tq, s // tk
    n_k, nv = ops_k[0].shape[1], op_v.shape[1]
    assert op_v.shape[0] == b * pairs * nq_t * nkv and op_v.shape[2] == LANES

    def step_idx(bb, hp, i):
        return ((bb * pairs + hp) * nq_t + i, 0, 0)

    kspec = pl.BlockSpec((nkv, n_k, LANES), step_idx)
    vspec = pl.BlockSpec((nkv, nv, LANES), step_idx)

    use_static = (bound <= STATIC_SOFTMAX_MAX_BOUND).astype(jnp.int32).reshape(1)
    return pl.pallas_call(
        functools.partial(_attn_scan_kernel, tk=tk, unroll=8),
        grid=(b, pairs, nq_t),
        in_specs=[pl.BlockSpec(memory_space=pltpu.SMEM),
                  pl.BlockSpec((None, tq, 2 * HEAD_SLOT), lambda bb, hp, i: (bb, i, hp)),
                  pl.BlockSpec((None, s, 2 * HEAD_SLOT), lambda bb, hp, i: (bb, 0, hp)),
                  pl.BlockSpec((None, s, 2 * MLA_V), lambda bb, hp, i: (bb, 0, hp))] + [kspec] * 5 + [vspec],
        out_specs=[pl.BlockSpec((None, tq, 2 * MLA_V), lambda bb, hp, i: (bb, i, hp)), vspec],
        out_shape=[jax.ShapeDtypeStruct((b, n_q, pairs * 2 * MLA_V), BF16), jax.ShapeDtypeStruct(op_v.shape, F32)],
        scratch_shapes=[pltpu.VMEM((n_k, nv, LANES), F32), pltpu.VMEM((2, tq, LANES), F32),
                        pltpu.VMEM((2, tq, 2 * MLA_V), F32), pltpu.VMEM((4, nkv, n_k, LANES), F32)],
        compiler_params=_params(("arbitrary", "arbitrary", "arbitrary")),
        name="attn_scan",
    )(use_static, q, k, v, *ops_k, op_v)


def _scan_operands(fwd, bwd, n_steps):
    r, v, nkk, w_f, k_f, b_f = fwd
    r_b, v_b, nkk_b, w_b, k_b, b_b = bwd
    b, s, width = r.shape
    heads = width // RWKV_HEAD
    half = RWKV_HEAD // 2
    nl = 2 * 2 * b * heads
    assert nl <= LANES

    def pad(x, step_fill):
        x = jnp.pad(x, ((0, 0), (0, 0), (0, LANES - nl)))
        return jnp.pad(x, ((0, n_steps - s), (0, 0), (0, 0)), constant_values=step_fill)

    def k_layout(x_f, x_b, step_fill=0.0):
        z = jnp.stack([x_f, x_b]).reshape(2, b, s, heads, RWKV_HEAD)
        z = z.transpose(2, 4, 0, 1, 3).reshape(s, RWKV_HEAD, 2 * b * heads)
        return pad(jnp.concatenate([z, z], axis=-1), step_fill)

    def v_layout(x_f, x_b):
        z = jnp.stack([x_f, x_b]).reshape(2, b, s, heads, 2, half)
        return pad(z.transpose(2, 5, 4, 0, 1, 3).reshape(s, half, nl), 0.0)

    ops_k = [k_layout(nkk, nkk_b), k_layout(w_f, w_b, 1.0), k_layout(b_f, b_b), k_layout(k_f, k_b),
             k_layout(r, r_b)]
    return ops_k, v_layout(v, v_b)


def _scan_result(y, b, s, width):
    heads = width // RWKV_HEAD
    half = RWKV_HEAD // 2
    y = y[:s, :, :2 * 2 * b * heads]
    return y.reshape(s, half, 2, 2, b, heads).transpose(3, 4, 0, 5, 2, 1).reshape(2, b, s, width)


def _merge_kernel(yf_ref, yb_ref, bonus_ref, gg_ref, op_ref, omc_ref, oml_ref, gate_ref, xc_ref, xl_ref, g1_ref,
                  lnw_ref, lnb_ref, wbp_ref, wbm_ref, wbr_ref, wo_ref, o_ref, *, tl, off):
    y = yf_ref[...] + _rev_rows(yb_ref[...])
    mu = _seg64_sum(y) * (1.0 / RWKV_HEAD)
    yc = y - mu
    var = _seg64_sum(yc * yc) * (1.0 / RWKV_HEAD)
    yn = yc * lax.rsqrt(var + RWKV_GN_EPS) * lnw_ref[...] + lnb_ref[...]
    o_rw = ((yn + bonus_ref[...]) * gg_ref[...]).astype(BF16)
    x = tl.pick(off, xc_ref, xl_ref)
    d = x.shape[-1]
    m = (gate_ref[:, 0:d].astype(F32) * _dot(op_ref[...], wbp_ref[...])
         + gate_ref[:, d:2 * d].astype(F32) * _dot(tl.pick(off, omc_ref, oml_ref), wbm_ref[...])
         + gate_ref[:, 2 * d:3 * d].astype(F32) * _dot(o_rw, wbr_ref[...]))
    o_ref[...] = x + g1_ref[...] * _dot(m.astype(BF16), wo_ref[...])


def _merge(tl, off, y, bonus, gg, o_pool, o_mla_ctx, o_mla_lat, gate, x_ctx, x_lat, lat_off, mod,
           lnw, lnb, wbp, wbm, wbr, wo):
    b, d = x_ctx.shape[0], x_ctx.shape[-1]
    n_tiles = tl.ns - off
    out_spec = pl.BlockSpec((None, tl.tm, d), lambda bb, ss: (bb, ss, 0))
    rw = y.shape[-1]
    yf_spec = pl.BlockSpec((None, None, tl.tm, rw), lambda bb, ss: (0, bb, ss + off, 0))
    yb_spec = pl.BlockSpec((None, None, tl.tm, rw), lambda bb, ss: (1, bb, _rev_tile(tl, ss + off), 0))
    return pl.pallas_call(
        functools.partial(_merge_kernel, tl=tl, off=off),
        grid=tl.grid(off),
        in_specs=[yf_spec, yb_spec, tl.tok(rw, off), tl.tok(rw, off), tl.tok(o_pool.shape[-1], off)]
        + tl.tok_split(o_mla_lat.shape[-1], off, 0) + [tl.tok(gate.shape[-1], off)]
        + tl.tok_split(d, off, lat_off) + [tl.mod(2, off), tl.const(lnw.shape), tl.const(lnb.shape)]
        + [tl.const(w.shape) for w in (wbp, wbm, wbr, wo)],
        out_specs=out_spec,
        out_shape=jax.ShapeDtypeStruct((b, n_tiles * tl.tm, d), F32),
        compiler_params=_params(("parallel", "parallel")),
        name="merge",
    )(y, y, bonus, gg, o_pool, o_mla_ctx, o_mla_lat, gate, x_ctx, x_lat, mod, lnw, lnb, wbp, wbm, wbr, wo)


def _mlp_kernel(x_ref, g_ref, *rest, parts, tm):
    mods, (w1_ref, w2_ref, o_ref, h_ref, acc_ref) = rest[:3 * parts], rest[3 * parts:]
    j = pl.program_id(1)

    @pl.when(j == 0)
    def _():
        for p in range(parts):
            rows = slice(p * tm, (p + 1) * tm)
            h_ref[rows, :] = _norm_mod(x_ref[rows, :], g_ref[...], mods[3 * p][...], mods[3 * p + 1][...]).astype(BF16)
        acc_ref[...] = jnp.zeros_like(acc_ref)

    a = jnp.maximum(_dot(h_ref[...], w1_ref[...]), 0.0)
    acc_ref[...] += _dot((a * a).astype(BF16), w2_ref[...])

    @pl.when(j == pl.num_programs(1) - 1)
    def _():
        for p in range(parts):
            rows = slice(p * tm, (p + 1) * tm)
            o_ref[rows, :] = x_ref[rows, :] + mods[3 * p + 2][...] * acc_ref[rows, :]


def _mlp(tl, x1, mod_off, g, mod, w1, w2):
    b, s1, d = x1.shape
    dff = w1.shape[1]
    tf = 1024 if dff % 1024 == 0 else dff
    tm = tl.tm
    per_batch = s1 // tm
    parts = next(p for p in (4, 2, 1) if (b * per_batch) % p == 0)
    nct, ctx_row = tl.nct, tl.ctx_row

    def mod_spec(j, p):
        def idx(ii, jj):
            sub = ii * parts + p
            bb, ss = sub // per_batch, sub % per_batch
            return (jnp.where(ss + mod_off < nct, ctx_row, bb), j, 0, 0)
        return pl.BlockSpec((None, None, 1, d), idx)

    mod_specs = [mod_spec(j, p) for p in range(parts) for j in (3, 4, 5)]
    x_spec = pl.BlockSpec((parts * tm, d), lambda ii, jj: (ii, 0))
    out = pl.pallas_call(
        functools.partial(_mlp_kernel, parts=parts, tm=tm),
        grid=(b * per_batch // parts, dff // tf),
        in_specs=[x_spec, pl.BlockSpec((1, d), lambda ii, jj: (0, 0))] + mod_specs
        + [pl.BlockSpec((d, tf), lambda ii, jj: (0, jj)), pl.BlockSpec((tf, d), lambda ii, jj: (jj, 0))],
        out_specs=x_spec,
        out_shape=jax.ShapeDtypeStruct((b * s1, d), F32),
        scratch_shapes=[pltpu.VMEM((parts * tm, d), BF16), pltpu.VMEM((parts * tm, d), F32)],
        compiler_params=_params(("parallel", "arbitrary")),
        name="mlp",
    )(x1.reshape(b * s1, d), g, *([mod] * (3 * parts)), w1, w2)
    return out.reshape(b, s1, d)


def _rope_tables(lc, l):
    rows = l // GRID_W
    row = jnp.repeat(jnp.arange(rows), GRID_W).astype(F32)
    col = jnp.tile(jnp.arange(GRID_W), rows).astype(F32)
    n_freq = MLA_ROPE // 4
    inv_freq = jnp.power(ROPE_BASE, -jnp.arange(n_freq, dtype=F32) / n_freq)
    ang = jnp.concatenate([row[:, None] * inv_freq, col[:, None] * inv_freq], axis=-1)
    cos = jnp.concatenate([jnp.ones((lc, MLA_ROPE // 2), F32), jnp.cos(ang)], axis=0)
    sin = jnp.concatenate([jnp.zeros((lc, MLA_ROPE // 2), F32), jnp.sin(ang)], axis=0)
    s = lc + l
    pad = jnp.zeros((s, LANES - MLA_NOPE - MLA_ROPE), F32)
    z16 = jnp.zeros((s, MLA_ROPE // 2), F32)
    zn = jnp.zeros((s, MLA_NOPE), F32)
    tc = jnp.concatenate([jnp.ones((s, MLA_NOPE), F32), cos, cos, pad], axis=1)
    ts1 = jnp.concatenate([zn, -sin, z16, pad], axis=1)
    ts2 = jnp.concatenate([zn, z16, sin, pad], axis=1)
    return tc, ts1, ts2


def _slot_cols(w, per_head):
    k = w.shape[0]
    w = w.reshape(k, MLA_HEADS, per_head)
    return jnp.pad(w, ((0, 0), (0, 0), (0, HEAD_SLOT - per_head))).reshape(k, MLA_HEADS * HEAD_SLOT)


def _block_diag(blocks):
    n = len(blocks)
    rows = []
    for i, blk in enumerate(blocks):
        rows.append(jnp.concatenate(
            [blk if j == i else jnp.zeros((blk.shape[0], blocks[j].shape[1]), blk.dtype) for j in range(n)], axis=1))
    return jnp.concatenate(rows, axis=0)


def _row(x):
    return x.reshape(1, -1).astype(F32)


def kernel(x, c, ctx, c_ctx, norm1_g, norm2_g, w_ada, b_ada, w_in, pool_w, pool_scale, mla_q_norm, mla_w_uq, mla_kv_norm, mla_w_ukv, qk_gain_q, qk_gain_k, rwkv_mu, rwkv_w0, rwkv_w2, rwkv_a0, rwkv_a2, rwkv_ka, rwkv_kk, rwkv_rk, rwkv_g2, rwkv_ln_w, rwkv_ln_b, w_br_pool, w_br_mla, w_br_rwkv, w_o, mlp_w1, mlp_w2):
    b, l, d = x.shape
    lc = ctx.shape[1]
    depth = w_in.shape[0]
    tl = _Tiles(b, lc, l, d)

    pool_width = pool_scale.shape[-1]
    q_rank = mla_q_norm.shape[-1]
    kv_rank = mla_kv_norm.shape[-1]
    rw_width = rwkv_kk.shape[-1]
    rw_in = rwkv_mu.shape[-1]

    rows = -(-(b + 1) // SUBLANES) * SUBLANES
    c_all = jnp.concatenate([c, c_ctx[None, :], jnp.zeros((rows - b - 1, d), F32)], axis=0)
    mod_all = _ada_mod(c_all, w_ada, b_ada).reshape(depth, rows, N_MOD, 1, d)
    tabs = _rope_tables(lc, l)
    seg = jnp.zeros((HEAD_SLOT, HEAD_SLOT), F32)
    seg = seg.at[:MLA_NOPE, :MLA_NOPE].set(1.0 / MLA_NOPE)
    seg = seg.at[MLA_NOPE:MLA_NOPE + MLA_ROPE, MLA_NOPE:MLA_NOPE + MLA_ROPE].set(1.0 / MLA_ROPE)
    seg = _block_diag([seg, seg]).astype(BF16)
    anti = jnp.eye(tl.tm, dtype=BF16)[::-1]

    tk = 256 if tl.s % 256 == 0 else LANES
    tq = 512
    while tq * tk > (MLA_HEADS // 2) * b * l or l % tq:
        tq //= 2
    assert tq >= 16

    x_ctx, x_lat, lat_off = ctx, x, 0
    out = None
    for i in range(depth):
        need_ctx = i < depth - 1
        off = 0 if need_ctx else tl.nct
        mod = mod_all[i]

        wi = w_in[i].astype(BF16)
        o0 = 0
        wp = wi[:, o0:o0 + pool_width]; o0 += pool_width
        wq = wi[:, o0:o0 + q_rank]; o0 += q_rank
        wkv = wi[:, o0:o0 + kv_rank + MLA_ROPE]; o0 += kv_rank + MLA_ROPE
        wkv = jnp.pad(wkv, ((0, 0), (0, LANES - MLA_ROPE)))
        wrw = wi[:, o0:o0 + rw_in]; o0 += rw_in
        wg = wi[:, o0:]

        zp, zq, zkv, zrw, gate = _in_proj(tl, x_ctx, x_lat, lat_off, _row(norm1_g[i]), mod, wp, wq, wkv, wrw, wg)

        pw_bd = _block_diag([pool_w[i, g] for g in range(pool_w.shape[1])]).astype(BF16)
        o_pool = _pool(tl, zp, pw_bd, _row(pool_scale[i]), off)

        wuq = _slot_cols(mla_w_uq[i], MLA_NOPE + MLA_ROPE).astype(BF16)
        wukv = mla_w_ukv[i].reshape(kv_rank, MLA_HEADS, MLA_NOPE + MLA_V)
        wuk = _slot_cols(wukv[:, :, :MLA_NOPE].reshape(kv_rank, -1), MLA_NOPE).astype(BF16)
        wuv = wukv[:, :, MLA_NOPE:].reshape(kv_rank, MLA_HEADS * MLA_V).astype(BF16)
        zpad = jnp.zeros((LANES - MLA_NOPE - MLA_ROPE,), F32)
        gq = _row(jnp.concatenate([qk_gain_q[i], zpad]))
        gk = _row(jnp.concatenate([qk_gain_k[i, :MLA_NOPE], jnp.zeros((LANES - MLA_NOPE,), F32)]))
        gkr = _row(jnp.concatenate([qk_gain_k[i, MLA_NOPE:], jnp.zeros((LANES - MLA_ROPE,), F32)]))
        bound = _score_bound(qk_gain_q[i], qk_gain_k[i])
        spare = (jnp.arange(LANES) == MLA_NOPE + MLA_ROPE).astype(F32)[None, :]
        smalls = (_row(mla_q_norm[i]), _row(mla_kv_norm[i]), gq, gk, gkr, seg, spare, -bound * spare)
        q, k, v = _qkv_up(tl, zq, zkv, tabs, smalls, wuq, wuk, wuv)

        w2cat = _block_diag([rwkv_w2[i, 0], rwkv_w2[i, 1]]).astype(BF16)
        a2cat = _block_diag([rwkv_a2[i, 0], rwkv_a2[i, 1]]).astype(BF16)
        rsmalls = (rwkv_mu[i].astype(F32), _row(rwkv_kk[i]), _row(rwkv_w0[i]), w2cat, _row(rwkv_a0[i]), a2cat,
                   _row(rwkv_ka[i]), _row(rwkv_rk[i]), rwkv_g2[i].astype(BF16), anti)
        r, vv, nkk, gg, bonus, wf, kf, bf, rb, vb, nkkb, wb, kb, bb = _rwkv_prep(tl, zrw, rsmalls)
        n_steps = b * (MLA_HEADS // 2) * (l // tq) * (tl.s // tk)
        ops_k, op_v = _scan_operands((r, vv, nkk, wf, kf, bf), (rb, vb, nkkb, wb, kb, bb), n_steps)
        o_mla_l, y = _attn_scan(q, k, v, ops_k, op_v, bound, tq=tq, tk=tk, n_q=l)
        y = _scan_result(y, b, tl.s, rw_width)
        if need_ctx:
            o_mla_c = _attention(q, k, v, tq=tl.tm, q_off_tiles=l // tl.tm, n_q_tiles=tl.nct, n_keys=lc)
        else:
            o_mla_c = o_mla_l

        x1 = _merge(tl, off, y, bonus, gg, o_pool, o_mla_c, o_mla_l, gate, x_ctx, x_lat, lat_off, mod,
                    _row(rwkv_ln_w[i]), _row(rwkv_ln_b[i]),
                    w_br_pool[i].astype(BF16), w_br_mla[i].astype(BF16), w_br_rwkv[i].astype(BF16),
                    w_o[i].astype(BF16))
        xc_next = _mlp(tl, x1, off, _row(norm2_g[i]), mod, mlp_w1[i].astype(BF16), mlp_w2[i].astype(BF16))
        if need_ctx:
            x_ctx, x_lat, lat_off = xc_next, xc_next, tl.nct
        else:
            out = xc_next
    return out
```

```python
import functools
import math

import jax
import jax.numpy as jnp
from jax import lax
from jax.experimental import pallas as pl
from jax.experimental.pallas import tpu as pltpu

F32 = jnp.float32
BF16 = jnp.bfloat16

NORM_EPS = 1e-6
RWKV_GN_EPS = 64e-5
GRID_W = 64
ROPE_BASE = 10000.0
POOL_HALF_WINDOWS = (1, 2, 4, 8)
N_MOD = 6
MLA_HEADS = 8
MLA_NOPE = 64
MLA_ROPE = 32
MLA_V = 64
RWKV_HEAD = 64
DECAY_RANK = 64
AAA_RANK = 64
GATE_RANK = 128

LANES = 128
SUBLANES = 8
HEAD_SLOT = LANES
HALO = SUBLANES
VMEM_LIMIT = 56 * 1024 * 1024

LOG2E = 1.4426950408889634


def _dot(a, b):
    return jnp.dot(a, b, preferred_element_type=F32)


def _sigmoid(x):
    return 1.0 / (1.0 + jnp.exp(-x))


def _rms(x, width):
    return lax.rsqrt(jnp.sum(x * x, axis=-1, keepdims=True) * (1.0 / width) + NORM_EPS)


def _norm_mod(x, g, shift, scale):
    return (x * _rms(x, x.shape[-1]) * g) * (1.0 + scale) + shift


def _lane(shape):
    return lax.broadcasted_iota(jnp.int32, shape, 1)


def _seg64_sum(x):
    cols = []
    for c in range(x.shape[1] // LANES):
        xc = x[:, c * LANES:(c + 1) * LANES]
        lo_m = _lane(xc.shape) < 64
        lo = jnp.sum(jnp.where(lo_m, xc, 0.0), axis=-1, keepdims=True)
        hi = jnp.sum(jnp.where(lo_m, 0.0, xc), axis=-1, keepdims=True)
        cols.append(jnp.where(lo_m, lo, hi))
    return cols[0] if len(cols) == 1 else jnp.concatenate(cols, axis=1)


def _params(sem, **flags):
    return pltpu.CompilerParams(dimension_semantics=sem, vmem_limit_bytes=VMEM_LIMIT, flags=flags or None)


class _Tiles:
    def __init__(self, batch, lc, l, d):
        self.batch, self.lc, self.l, self.d = batch, lc, l, d
        self.s = lc + l
        self.tm = 256 if (lc % 256 == 0 and l % 256 == 0) else 128
        assert lc % self.tm == 0 and l % self.tm == 0
        self.nct = lc // self.tm
        self.ns = self.s // self.tm
        self.ctx_row = batch

    def grid(self, off):
        return (self.batch, self.ns - off)

    def tok(self, width, off):
        return pl.BlockSpec((None, self.tm, width), lambda b, s: (b, s + off, 0))

    def tok_split(self, width, off, lat_off):
        nct = self.nct
        ctx = pl.BlockSpec((None, self.tm, width), lambda b, s: (b, jnp.minimum(s + off, nct - 1), 0))
        lat = pl.BlockSpec((None, self.tm, width), lambda b, s: (b, jnp.maximum(s + off - nct, 0) + lat_off, 0))
        return [ctx, lat]

    def pick(self, off, ctx_ref, lat_ref):
        return jnp.where(pl.program_id(1) + off < self.nct, ctx_ref[...], lat_ref[...])

    def halo_prev(self, width, off):
        r = self.tm // HALO
        return pl.BlockSpec((None, HALO, width), lambda b, s: (b, jnp.maximum((s + off) * r - 1, 0), 0))

    def halo_next(self, width, off):
        r = self.tm // HALO
        last = self.s // HALO - 1
        return pl.BlockSpec((None, HALO, width), lambda b, s: (b, jnp.minimum((s + off + 1) * r, last), 0))

    def mod(self, j, off):
        nct, ctx_row = self.nct, self.ctx_row
        return pl.BlockSpec((None, None, 1, self.d),
                            lambda b, s: (jnp.where(s + off < nct, ctx_row, b), j, 0, 0))

    def const(self, shape):
        nd = len(shape)
        return pl.BlockSpec(shape, lambda b, s: (0,) * nd)


def _ada_kernel(c_ref, w_ref, b_ref, o_ref):
    c = c_ref[...]
    s = (c * _sigmoid(c)).astype(BF16)
    o_ref[...] = _dot(s, w_ref[...].astype(BF16)) + b_ref[...]


def _ada_mod(c_all, w_ada, b_ada):
    depth, d, n = w_ada.shape
    rows = c_all.shape[0]
    tn = 1024
    return pl.pallas_call(
        _ada_kernel,
        grid=(depth, n // tn),
        in_specs=[pl.BlockSpec((rows, d), lambda i, j: (0, 0)),
                  pl.BlockSpec((None, d, tn), lambda i, j: (i, 0, j)),
                  pl.BlockSpec((None, 1, tn), lambda i, j: (i, 0, j))],
        out_specs=pl.BlockSpec((None, rows, tn), lambda i, j: (i, 0, j)),
        out_shape=jax.ShapeDtypeStruct((depth, rows, n), F32),
        compiler_params=_params(("parallel", "parallel")),
        name="ada_mod",
    )(c_all, w_ada, b_ada.reshape(depth, 1, n))


def _in_proj_kernel(xc_ref, xl_ref, g_ref, sh_ref, sc_ref, wp_ref, wq_ref, wkv_ref, wrw_ref, wg_ref,
                    zp_ref, zq_ref, zkv_ref, zrw_ref, gate_ref, *, tl):
    h = _norm_mod(tl.pick(0, xc_ref, xl_ref), g_ref[...], sh_ref[...], sc_ref[...]).astype(BF16)
    zp_ref[...] = _dot(h, wp_ref[...])
    zq_ref[...] = _dot(h, wq_ref[...])
    zkv_ref[...] = _dot(h, wkv_ref[...])
    zrw_ref[...] = _dot(h, wrw_ref[...])
    d = h.shape[-1]
    for c in range(wg_ref.shape[1] // d):
        gate_ref[:, c * d:(c + 1) * d] = _sigmoid(_dot(h, wg_ref[:, c * d:(c + 1) * d])).astype(BF16)


def _in_proj(tl, x_ctx, x_lat, lat_off, g, mod, wp, wq, wkv, wrw, wg):
    b, d = x_ctx.shape[0], x_ctx.shape[-1]
    widths = (wp.shape[1], wq.shape[1], wkv.shape[1], wrw.shape[1], wg.shape[1])
    dts = (F32, F32, F32, F32, BF16)
    return pl.pallas_call(
        functools.partial(_in_proj_kernel, tl=tl),
        grid=tl.grid(0),
        in_specs=tl.tok_split(d, 0, lat_off) + [tl.const((1, d)), tl.mod(0, 0), tl.mod(1, 0)]
        + [tl.const(w.shape) for w in (wp, wq, wkv, wrw, wg)],
        out_specs=[tl.tok(w, 0) for w in widths],
        out_shape=[jax.ShapeDtypeStruct((b, tl.s, w), dt) for w, dt in zip(widths, dts)],
        compiler_params=_params(("parallel", "parallel")),
        name="in_proj",
    )(x_ctx, x_lat, g, mod, mod, wp, wq, wkv, wrw, wg)


def _seq_flags(tl, off):
    s_abs = pl.program_id(1) + off
    is_ctx = s_abs < tl.nct
    first = jnp.logical_or(s_abs == 0, s_abs == tl.nct)
    last = jnp.logical_or(s_abs == tl.nct - 1, s_abs == tl.ns - 1)
    seq_len = jnp.where(is_ctx, tl.lc, tl.l)
    tile_in_seq = jnp.where(is_ctx, s_abs, s_abs - tl.nct)
    return first, last, seq_len, tile_in_seq


def _pool_kernel(u_ref, up_ref, un_ref, pw_ref, ps_ref, o_ref, *, tl, off):
    first, last, seq_len, tile_in_seq = _seq_flags(tl, off)
    tm = tl.tm
    u = u_ref[...]
    prev = jnp.where(first, 0.0, up_ref[...])
    nxt = jnp.where(last, 0.0, un_ref[...])
    e = jnp.concatenate([prev, u, nxt], axis=0)
    n = tm + 2 * HALO
    w2 = e + pltpu.roll(e, 1, 0)
    w4 = pltpu.roll(w2, n - 1, 0) + pltpu.roll(w2, 1, 0)
    w8 = pltpu.roll(w4, n - 2, 0) + pltpu.roll(w4, 2, 0)
    w16 = pltpu.roll(w8, n - 4, 0) + pltpu.roll(w8, 4, 0)
    sums = [w[HALO:HALO + tm] for w in (w2, w4, w8, w16)]
    width = u.shape[1]
    group = width // len(POOL_HALF_WINDOWS)
    lane = _lane((tm, width))
    pos = tile_in_seq * tm + lax.broadcasted_iota(jnp.int32, (tm, width), 0)
    total = sums[-1]
    half = jnp.full((tm, width), POOL_HALF_WINDOWS[-1], jnp.int32)
    for gi in range(len(POOL_HALF_WINDOWS) - 2, -1, -1):
        sel = lane < (gi + 1) * group
        total = jnp.where(sel, sums[gi], total)
        half = jnp.where(sel, POOL_HALF_WINDOWS[gi], half)
    cnt = jnp.minimum(pos + half, seq_len) - jnp.maximum(pos - half, 0)
    pooled = total / cnt.astype(F32) - u
    o_ref[...] = (_dot(pooled.astype(BF16), pw_ref[...]) * ps_ref[...]).astype(BF16)


def _pool(tl, zp, pw_bd, ps, off):
    b, s, w = zp.shape
    return pl.pallas_call(
        functools.partial(_pool_kernel, tl=tl, off=off),
        grid=tl.grid(off),
        in_specs=[tl.tok(w, off), tl.halo_prev(w, off), tl.halo_next(w, off),
                  tl.const(pw_bd.shape), tl.const(ps.shape)],
        out_specs=tl.tok(w, off),
        out_shape=jax.ShapeDtypeStruct((b, s, w), BF16),
        compiler_params=_params(("parallel", "parallel")),
        name="pool_mixer",
    )(zp, zp, zp, pw_bd, ps)


def _rope(x, tc, ts1, ts2):
    return x * tc + pltpu.roll(x, LANES - MLA_ROPE // 2, 1) * ts1 + pltpu.roll(x, MLA_ROPE // 2, 1) * ts2


def _head_inv_rms(x, seg):
    w = seg.shape[0]
    cols = [lax.rsqrt(_dot((xc * xc).astype(BF16), seg) + NORM_EPS)
            for xc in (x[:, c * w:(c + 1) * w] for c in range(x.shape[1] // w))]
    return jnp.concatenate(cols, axis=1)


def _qkv_kernel(zq_ref, zkv_ref, tc_ref, ts1_ref, ts2_ref, qng_ref, kvng_ref, gq_ref, gk_ref, gkr_ref, seg_ref,
                qpad_ref, kpad_ref, wuq_ref, wuk_ref, wuv_ref, q_ref, k_ref, v_ref):
    tc, ts1, ts2 = tc_ref[...], ts1_ref[...], ts2_ref[...]
    seg = seg_ref[...]

    zq = zq_ref[...]
    qc = (zq * _rms(zq, zq.shape[-1]) * qng_ref[...]).astype(BF16)
    q = _dot(qc, wuq_ref[...])
    q = q * _head_inv_rms(q, seg)
    q_scale = LOG2E * (MLA_NOPE + MLA_ROPE) ** -0.5
    for h in range(MLA_HEADS):
        qh = _rope(q[:, h * HEAD_SLOT:(h + 1) * HEAD_SLOT] * gq_ref[...], tc, ts1, ts2)
        q_ref[:, h * HEAD_SLOT:(h + 1) * HEAD_SLOT] = (qh * q_scale + qpad_ref[...]).astype(BF16)

    zkv = zkv_ref[...]
    kv_w = kvng_ref.shape[-1]
    kvc = zkv[:, :kv_w]
    kvn = (kvc * _rms(kvc, kv_w) * kvng_ref[...]).astype(BF16)
    kr = zkv[:, kv_w:kv_w + LANES]
    kr = kr * lax.rsqrt(jnp.sum(kr * kr, axis=-1, keepdims=True) * (1.0 / MLA_ROPE) + NORM_EPS) * gkr_ref[...]
    kr = _rope(pltpu.roll(kr, MLA_NOPE, 1), tc, ts1, ts2)
    kn = _dot(kvn, wuk_ref[...])
    kn = kn * _head_inv_rms(kn, seg)
    for h in range(MLA_HEADS):
        kh = kn[:, h * HEAD_SLOT:(h + 1) * HEAD_SLOT]
        k_ref[:, h * HEAD_SLOT:(h + 1) * HEAD_SLOT] = (kh * gk_ref[...] + kr + kpad_ref[...]).astype(BF16)
    v_ref[...] = _dot(kvn, wuv_ref[...]).astype(BF16)


def _qkv_up(tl, zq, zkv, tabs, smalls, wuq, wuk, wuv):
    b, s, _ = zq.shape
    tc, ts1, ts2 = tabs
    tab_spec = pl.BlockSpec((tl.tm, LANES), lambda bb, ss: (ss, 0))
    widths = (wuq.shape[1], wuk.shape[1], wuv.shape[1])
    nct, nlt = tl.nct, tl.ns - tl.nct
    q_spec = pl.BlockSpec((None, tl.tm, widths[0]), lambda bb, ss: (bb, jnp.where(ss < nct, nlt + ss, ss - nct), 0))
    return pl.pallas_call(
        _qkv_kernel,
        grid=tl.grid(0),
        in_specs=[tl.tok(zq.shape[-1], 0), tl.tok(zkv.shape[-1], 0), tab_spec, tab_spec, tab_spec]
        + [tl.const(a.shape) for a in smalls] + [tl.const(w.shape) for w in (wuq, wuk, wuv)],
        out_specs=[q_spec, tl.tok(widths[1], 0), tl.tok(widths[2], 0)],
        out_shape=[jax.ShapeDtypeStruct((b, s, w), BF16) for w in widths],
        compiler_params=_params(("parallel", "parallel")),
        name="qkv_up",
    )(zq, zkv, tc, ts1, ts2, *smalls, wuq, wuk, wuv)


def _attn_kernel(q_ref, k_ref, v_ref, o_ref):
    v = v_ref[...]
    outs = []
    for hh in range(2):
        q = q_ref[:, hh * HEAD_SLOT:(hh + 1) * HEAD_SLOT]
        k = k_ref[:, hh * HEAD_SLOT:(hh + 1) * HEAD_SLOT]
        s = lax.dot_general(q, k, (((1,), (1,)), ((), ())), preferred_element_type=F32)
        p = jnp.exp2(s - jnp.max(s, axis=-1, keepdims=True))
        denom = jnp.sum(p, axis=-1, keepdims=True)
        outs.append(_dot(p.astype(BF16), v) / denom)
    o_ref[...] = jnp.where(_lane(outs[0].shape) < MLA_V, outs[0], outs[1]).astype(BF16)


def _attention(q, k, v, *, tq, q_off_tiles, n_q_tiles, n_keys):
    b, _, qw = q.shape
    pairs = qw // (2 * HEAD_SLOT)
    return pl.pallas_call(
        _attn_kernel,
        grid=(b, pairs, n_q_tiles),
        in_specs=[pl.BlockSpec((None, tq, 2 * HEAD_SLOT), lambda bb, hp, i: (bb, i + q_off_tiles, hp)),
                  pl.BlockSpec((None, n_keys, 2 * HEAD_SLOT), lambda bb, hp, i: (bb, 0, hp)),
                  pl.BlockSpec((None, n_keys, 2 * MLA_V), lambda bb, hp, i: (bb, 0, hp))],
        out_specs=pl.BlockSpec((None, tq, 2 * MLA_V), lambda bb, hp, i: (bb, i, hp)),
        out_shape=jax.ShapeDtypeStruct((b, n_q_tiles * tq, pairs * 2 * MLA_V), BF16),
        compiler_params=_params(("parallel", "parallel", "arbitrary")),
        name="attention",
    )(q, k, v)


def _rev_rows(x):
    n = x.shape[0]
    row = lax.broadcasted_iota(jnp.int32, x.shape, 0)
    for sh in (1, 2, 4):
        x = jnp.where((row & sh) == 0, pltpu.roll(x, n - sh, 0), pltpu.roll(x, sh, 0))
    groups = [x[g * SUBLANES:(g + 1) * SUBLANES] for g in range(n // SUBLANES)]
    return jnp.concatenate(groups[::-1], axis=0)


def _rev_tile(tl, s_abs):
    return jnp.where(s_abs < tl.nct, tl.nct - 1 - s_abs, tl.ns - 1 + tl.nct - s_abs)


def _rwkv_prep_kernel(z_ref, zp_ref, zn_ref, mu_ref, kkw_ref, w0_ref, w2_ref, a0_ref, a2_ref, ka_ref,
                      rk_ref, g2_ref, anti_ref,
                      r_o, v_o, nkk_o, gg_o, bonus_o, wf_o, kf_o, bf_o, rb_o, vb_o, nkkb_o, wb_o, kb_o, bb_o,
                      *, tl, off):
    first, last, _, _ = _seq_flags(tl, off)
    tm = tl.tm
    z = z_ref[...]
    row = lax.broadcasted_iota(jnp.int32, z.shape, 0)
    prev_row = jnp.where(first, 0.0, zp_ref[HALO - 1:HALO, :])
    next_row = jnp.where(last, 0.0, zn_ref[0:1, :])
    z_prev = jnp.where(row == 0, prev_row, pltpu.roll(z, 1, 0))
    z_next = jnp.where(row == tm - 1, next_row, pltpu.roll(z, tm - 1, 0))
    zs = z + mu_ref[0:1, :] * (z_prev - z) + mu_ref[1:2, :] * (z_next - z)

    w = kkw_ref.shape[-1]
    r, k, v = zs[:, 0:w], zs[:, w:2 * w], zs[:, 2 * w:3 * w]
    o = 3 * w
    wd = zs[:, o:o + 2 * DECAY_RANK]
    ad = zs[:, o + 2 * DECAY_RANK:o + 2 * DECAY_RANK + 2 * AAA_RANK]
    gd = zs[:, o + 2 * DECAY_RANK + 2 * AAA_RANK:]

    kk = k * kkw_ref[...]
    kk = kk * lax.rsqrt(jnp.maximum(_seg64_sum(kk * kk), 1e-24))
    u = w0_ref[...] + _dot(jnp.tanh(wd).astype(BF16), w2_ref[...])
    log_decay = -math.exp(-0.5) * _sigmoid(u)
    a = _sigmoid(a0_ref[...] + _dot(ad.astype(BF16), a2_ref[...]))
    ka = ka_ref[...]
    k_sum = jnp.zeros_like(k)
    anti = anti_ref[...]

    def rev16(t):
        return _dot(anti, t.astype(BF16)).astype(BF16)

    for d, (w_o, k_o, b_o) in enumerate(((wf_o, kf_o, bf_o), (wb_o, kb_o, bb_o))):
        a_d = a[:, d * w:(d + 1) * w]
        k_d = k * (1.0 + (a_d - 1.0) * ka[:, d * w:(d + 1) * w])
        lw = log_decay[:, d * w:(d + 1) * w]
        w_o[...] = rev16(lw) if d else lw.astype(BF16)
        k_o[...] = rev16(k_d) if d else k_d.astype(BF16)
        b_o[...] = rev16(kk * a_d) if d else (kk * a_d).astype(BF16)
        k_sum = k_sum + k_d
    r_o[...] = r.astype(BF16)
    v_o[...] = v
    nkk_o[...] = (-kk).astype(BF16)
    rb_o[...] = rev16(r)
    vb_o[...] = _rev_rows(v)
    nkkb_o[...] = rev16(-kk)
    gg_o[...] = _dot(_sigmoid(gd).astype(BF16), g2_ref[...])
    bonus_o[...] = _seg64_sum(r * (0.5 * k_sum) * rk_ref[...]) * v


def _rwkv_prep(tl, zrw, smalls):
    b, s, win = zrw.shape
    w = smalls[1].shape[-1]
    n_fwd, n_bwd = 8, 6
    bwd_spec = pl.BlockSpec((None, tl.tm, w), lambda bb, ss: (bb, _rev_tile(tl, ss), 0))
    return pl.pallas_call(
        functools.partial(_rwkv_prep_kernel, tl=tl, off=0),
        grid=tl.grid(0),
        in_specs=[tl.tok(win, 0), tl.halo_prev(win, 0), tl.halo_next(win, 0)]
        + [tl.const(a.shape) for a in smalls],
        out_specs=[tl.tok(w, 0)] * n_fwd + [bwd_spec] * n_bwd,
        out_shape=[jax.ShapeDtypeStruct((b, s, w), dt) for dt in
                   (BF16, F32, BF16, F32, F32, BF16, BF16, BF16, BF16, F32, BF16, BF16, BF16, BF16)],
        compiler_params=_params(("parallel", "parallel")),
        name="rwkv_prep",
    )(zrw, zrw, zrw, *smalls)


def _zero_after(x):
    bits = pltpu.bitcast(x[:SUBLANES, :LANES].astype(F32), jnp.uint32)
    return pltpu.bitcast((bits >> 16) >> 16, F32)


def _scan_step(t, a_ref, w_ref, b_ref, k_ref, r_ref, v_ref, y_ref, s_ref, after):
    n_k = s_ref.shape[0]
    v = v_ref[t] + jnp.concatenate([after] * (v_ref.shape[1] // SUBLANES), axis=0)
    acc = [jnp.zeros_like(v), jnp.zeros_like(v)]
    for kk in range(n_k):
        acc[kk % 2] = acc[kk % 2] + s_ref[kk] * a_ref[t, pl.ds(kk, 1), :]
    sa = acc[0] + acc[1]
    yacc = [jnp.zeros_like(v), jnp.zeros_like(v)]
    for kk in range(n_k):
        sn = (s_ref[kk] * w_ref[t, pl.ds(kk, 1), :] + sa * b_ref[t, pl.ds(kk, 1), :]
              + v * k_ref[t, pl.ds(kk, 1), :])
        s_ref[kk] = sn
        yacc[kk % 2] = yacc[kk % 2] + sn * r_ref[t, pl.ds(kk, 1), :]
    y_ref[t] = yacc[0] + yacc[1]


def _attn_scan_kernel(static_ref, q_ref, k_ref, v_ref, ab_ref, kr_ref, lw_ref, vv_ref, o_ref, y_ref,
                      s_ref, m_ref, acc_ref, ops_ref, *, tk, unroll):
    first = jnp.logical_and(pl.program_id(0) == 0, jnp.logical_and(pl.program_id(1) == 0, pl.program_id(2) == 0))

    @pl.when(first)
    def _():
        s_ref[...] = jnp.zeros_like(s_ref)

    def unpack_pair(ref):
        x = ref[...].astype(F32)
        swapped = pltpu.roll(x.reshape(-1, LANES), LANES // 2, 1).reshape(x.shape)
        lo = lax.broadcasted_iota(jnp.int32, x.shape, 2) < LANES // 2
        return jnp.where(lo, x, swapped), jnp.where(lo, swapped, x)

    ops_ref[0], ops_ref[1] = unpack_pair(ab_ref)
    ops_ref[2], ops_ref[3] = unpack_pair(kr_ref)
    ops_ref[4] = jnp.exp(lw_ref[...].astype(F32))
    a_ref, b_ref, kk_ref, r_ref, w_ref = (ops_ref.at[i] for i in range(5))

    m_ref[...] = jnp.full_like(m_ref, -1e30)
    acc_ref[...] = jnp.zeros_like(acc_ref)
    lo_half = _lane((tk, 2 * MLA_V)) < MLA_V
    nkv = vv_ref.shape[0]

    def block(j, after, static_max):
        _scan_step(j, a_ref, w_ref, b_ref, kk_ref, r_ref, vv_ref, y_ref, s_ref, after)
        start = j * tk if isinstance(j, int) else pl.multiple_of(j * tk, tk)
        vblk = v_ref[pl.ds(start, tk), :]
        v_ones = (jnp.where(lo_half, vblk, 1.0), jnp.where(lo_half, 1.0, vblk))
        for hh in range(2):
            q = q_ref[:, hh * HEAD_SLOT:(hh + 1) * HEAD_SLOT]
            kblk = k_ref[pl.ds(start, tk), hh * HEAD_SLOT:(hh + 1) * HEAD_SLOT]
            s = lax.dot_general(q, kblk, (((1,), (1,)), ((), ())), preferred_element_type=F32)
            if static_max:
                new = acc_ref[hh] + _dot(jnp.exp2(s).astype(BF16), v_ones[hh])
            else:
                m_old = m_ref[hh]
                m_new = jnp.maximum(m_old, jnp.max(s, axis=-1, keepdims=True))
                p = jnp.exp2(s - jnp.concatenate([m_new] * (tk // LANES), axis=1))
                new = jnp.exp2(m_old - m_new) * acc_ref[hh] + _dot(p.astype(BF16), v_ones[hh])
                m_ref[hh] = m_new
            acc_ref[hh] = new
        return _zero_after(new)

    def run(static_max):
        def body(jj, after):
            for u in range(unroll):
                after = block(jj * unroll + u, after, static_max)
            return after

        after = lax.fori_loop(0, nkv // unroll, body, jnp.zeros((SUBLANES, LANES), F32))
        for j in range(nkv - nkv % unroll, nkv):
            after = block(j, after, static_max)

    use_static = static_ref[0] != 0
    pl.when(use_static)(functools.partial(run, True))
    pl.when(jnp.logical_not(use_static))(functools.partial(run, False))
    a0, a1 = acc_ref[0], acc_ref[1]
    o = jnp.where(_lane(a0.shape) < MLA_V, a0 / pltpu.roll(a0, MLA_V, 1), a1 / pltpu.roll(a1, MLA_V, 1))
    o_ref[...] = o.astype(BF16)


def _score_bound(gain_q, gain_k):
    def norm2(g):
        return MLA_NOPE * jnp.max(jnp.square(g[:MLA_NOPE])) + MLA_ROPE * jnp.max(jnp.square(g[MLA_NOPE:]))
    q_scale = LOG2E * (MLA_NOPE + MLA_ROPE) ** -0.5
    return 1.02 * q_scale * jnp.sqrt(norm2(gain_q) * norm2(gain_k))


STATIC_SOFTMAX_MAX_BOUND = 50.0


def _attn_scan(q, k, v, ops_k, op_v, bound, *, tq, tk, n_q):
    b, s, qw = q.shape[0], k.shape[1], q.shape[2]
    pairs = qw // (2 * HEAD_SLOT)
    nq_t, nkv = n_q // tq, s // tk
    n_k, nv = ops_k[0].shape[1], op_v.shape[1]
    assert op_v.shape[0] == b * pairs * nq_t * nkv and op_v.shape[2] == LANES

    def step_idx(bb, hp, i):
        return ((bb * pairs + hp) * nq_t + i, 0, 0)

    kspec = pl.BlockSpec((nkv, n_k, LANES), step_idx)
    vspec = pl.BlockSpec((nkv, nv, LANES), step_idx)

    use_static = (bound <= STATIC_SOFTMAX_MAX_BOUND).astype(jnp.int32).reshape(1)
    return pl.pallas_call(
        functools.partial(_attn_scan_kernel, tk=tk, unroll=8),
        grid=(b, pairs, nq_t),
        in_specs=[pl.BlockSpec(memory_space=pltpu.SMEM),
                  pl.BlockSpec((None, tq, 2 * HEAD_SLOT), lambda bb, hp, i: (bb, i, hp)),
                  pl.BlockSpec((None, s, 2 * HEAD_SLOT), lambda bb, hp, i: (bb, 0, hp)),
                  pl.BlockSpec((None, s, 2 * MLA_V), lambda bb, hp, i: (bb, 0, hp))]
        + [kspec] * len(ops_k) + [vspec],
        out_specs=[pl.BlockSpec((None, tq, 2 * MLA_V), lambda bb, hp, i: (bb, i, hp)), vspec],
        out_shape=[jax.ShapeDtypeStruct((b, n_q, pairs * 2 * MLA_V), BF16), jax.ShapeDtypeStruct(op_v.shape, F32)],
        scratch_shapes=[pltpu.VMEM((n_k, nv, LANES), F32), pltpu.VMEM((2, tq, LANES), F32),
                        pltpu.VMEM((2, tq, 2 * MLA_V), F32), pltpu.VMEM((5, nkv, n_k, LANES), F32)],
        compiler_params=_params(("arbitrary", "arbitrary", "arbitrary")),
        name="attn_scan",
    )(use_static, q, k, v, *ops_k, op_v)


def _scan_operands(fwd, bwd, n_steps):
    r, v, nkk, w_f, k_f, b_f = fwd
    r_b, v_b, nkk_b, w_b, k_b, b_b = bwd
    b, s, width = r.shape
    heads = width // RWKV_HEAD
    half = RWKV_HEAD // 2
    nlh = 2 * b * heads
    assert nlh <= LANES // 2

    def pad(x):
        return jnp.pad(x, ((0, n_steps - s), (0, 0), (0, 0), (0, LANES // 2 - nlh))).reshape(n_steps, -1, LANES)

    def k_half(x_f, x_b):
        z = jnp.stack([x_f, x_b]).reshape(2, b, s, heads, RWKV_HEAD)
        return z.transpose(2, 4, 0, 1, 3).reshape(s, RWKV_HEAD, nlh)

    def pair(lo, hi):
        return pad(jnp.stack([lo, hi], axis=2))

    lw = k_half(w_f, w_b)
    ops_k = [pair(k_half(nkk, nkk_b), k_half(b_f, b_b)), pair(k_half(k_f, k_b), k_half(r, r_b)), pair(lw, lw)]
    z = jnp.stack([v, v_b]).reshape(2, b, s, heads, 2, half)
    return ops_k, pad(z.transpose(2, 5, 4, 0, 1, 3).reshape(s, half, 2, nlh))


def _scan_result(y, b, s, width):
    heads = width // RWKV_HEAD
    half = RWKV_HEAD // 2
    y = y[:s].reshape(s, half, 2, LANES // 2)[..., :2 * b * heads]
    return y.reshape(s, half, 2, 2, b, heads).transpose(3, 4, 0, 5, 2, 1).reshape(2, b, s, width)


def _merge_kernel(yf_ref, yb_ref, bonus_ref, gg_ref, op_ref, omc_ref, oml_ref, gate_ref, xc_ref, xl_ref, g1_ref,
                  lnw_ref, lnb_ref, wbp_ref, wbm_ref, wbr_ref, wo_ref, o_ref, *, tl, off):
    y = yf_ref[...] + _rev_rows(yb_ref[...])
    mu = _seg64_sum(y) * (1.0 / RWKV_HEAD)
    yc = y - mu
    var = _seg64_sum(yc * yc) * (1.0 / RWKV_HEAD)
    yn = yc * lax.rsqrt(var + RWKV_GN_EPS) * lnw_ref[...] + lnb_ref[...]
    o_rw = ((yn + bonus_ref[...]) * gg_ref[...]).astype(BF16)
    x = tl.pick(off, xc_ref, xl_ref)
    d = x.shape[-1]
    m = (gate_ref[:, 0:d].astype(F32) * _dot(op_ref[...], wbp_ref[...])
         + gate_ref[:, d:2 * d].astype(F32) * _dot(tl.pick(off, omc_ref, oml_ref), wbm_ref[...])
         + gate_ref[:, 2 * d:3 * d].astype(F32) * _dot(o_rw, wbr_ref[...]))
    o_ref[...] = x + g1_ref[...] * _dot(m.astype(BF16), wo_ref[...])


def _merge(tl, off, y, bonus, gg, o_pool, o_mla_ctx, o_mla_lat, gate, x_ctx, x_lat, lat_off, mod,
           lnw, lnb, wbp, wbm, wbr, wo):
    b, d = x_ctx.shape[0], x_ctx.shape[-1]
    n_tiles = tl.ns - off
    out_spec = pl.BlockSpec((None, tl.tm, d), lambda bb, ss: (bb, ss, 0))
    rw = y.shape[-1]
    yf_spec = pl.BlockSpec((None, None, tl.tm, rw), lambda bb, ss: (0, bb, ss + off, 0))
    yb_spec = pl.BlockSpec((None, None, tl.tm, rw), lambda bb, ss: (1, bb, _rev_tile(tl, ss + off), 0))
    return pl.pallas_call(
        functools.partial(_merge_kernel, tl=tl, off=off),
        grid=tl.grid(off),
        in_specs=[yf_spec, yb_spec, tl.tok(rw, off), tl.tok(rw, off), tl.tok(o_pool.shape[-1], off)]
        + tl.tok_split(o_mla_lat.shape[-1], off, 0) + [tl.tok(gate.shape[-1], off)]
        + tl.tok_split(d, off, lat_off) + [tl.mod(2, off), tl.const(lnw.shape), tl.const(lnb.shape)]
        + [tl.const(w.shape) for w in (wbp, wbm, wbr, wo)],
        out_specs=out_spec,
        out_shape=jax.ShapeDtypeStruct((b, n_tiles * tl.tm, d), F32),
        compiler_params=_params(("parallel", "parallel")),
        name="merge",
    )(y, y, bonus, gg, o_pool, o_mla_ctx, o_mla_lat, gate, x_ctx, x_lat, mod, lnw, lnb, wbp, wbm, wbr, wo)


def _mlp_kernel(x_ref, g_ref, *rest, parts, tm):
    mods, (w1_ref, w2_ref, o_ref, h_ref, acc_ref) = rest[:3 * parts], rest[3 * parts:]
    j = pl.program_id(1)

    @pl.when(j == 0)
    def _():
        for p in range(parts):
            rows = slice(p * tm, (p + 1) * tm)
            h_ref[rows, :] = _norm_mod(x_ref[rows, :], g_ref[...], mods[3 * p][...], mods[3 * p + 1][...]).astype(BF16)
        acc_ref[...] = jnp.zeros_like(acc_ref)

    a = jnp.maximum(_dot(h_ref[...], w1_ref[...]), 0.0)
    acc_ref[...] += _dot((a * a).astype(BF16), w2_ref[...])

    @pl.when(j == pl.num_programs(1) - 1)
    def _():
        for p in range(parts):
            rows = slice(p * tm, (p + 1) * tm)
            o_ref[rows, :] = x_ref[rows, :] + mods[3 * p + 2][...] * acc_ref[rows, :]


def _mlp(tl, x1, mod_off, g, mod, w1, w2):
    b, s1, d = x1.shape
    dff = w1.shape[1]
    tf = 1024 if dff % 1024 == 0 else dff
    tm = tl.tm
    per_batch = s1 // tm
    parts = next(p for p in (4, 2, 1) if (b * per_batch) % p == 0)
    nct, ctx_row = tl.nct, tl.ctx_row

    def mod_spec(j, p):
        def idx(ii, jj):
            sub = ii * parts + p
            bb, ss = sub // per_batch, sub % per_batch
            return (jnp.where(ss + mod_off < nct, ctx_row, bb), j, 0, 0)
        return pl.BlockSpec((None, None, 1, d), idx)

    mod_specs = [mod_spec(j, p) for p in range(parts) for j in (3, 4, 5)]
    x_spec = pl.BlockSpec((parts * tm, d), lambda ii, jj: (ii, 0))
    out = pl.pallas_call(
        functools.partial(_mlp_kernel, parts=parts, tm=tm),
        grid=(b * per_batch // parts, dff // tf),
        in_specs=[x_spec, pl.BlockSpec((1, d), lambda ii, jj: (0, 0))] + mod_specs
        + [pl.BlockSpec((d, tf), lambda ii, jj: (0, jj)), pl.BlockSpec((tf, d), lambda ii, jj: (jj, 0))],
        out_specs=x_spec,
        out_shape=jax.ShapeDtypeStruct((b * s1, d), F32),
        scratch_shapes=[pltpu.VMEM((parts * tm, d), BF16), pltpu.VMEM((parts * tm, d), F32)],
        compiler_params=_params(("parallel", "arbitrary")),
        name="mlp",
    )(x1.reshape(b * s1, d), g, *([mod] * (3 * parts)), w1, w2)
    return out.reshape(b, s1, d)


def _rope_tables(lc, l):
    rows = l // GRID_W
    row = jnp.repeat(jnp.arange(rows), GRID_W).astype(F32)
    col = jnp.tile(jnp.arange(GRID_W), rows).astype(F32)
    n_freq = MLA_ROPE // 4
    inv_freq = jnp.power(ROPE_BASE, -jnp.arange(n_freq, dtype=F32) / n_freq)
    ang = jnp.concatenate([row[:, None] * inv_freq, col[:, None] * inv_freq], axis=-1)
    cos = jnp.concatenate([jnp.ones((lc, MLA_ROPE // 2), F32), jnp.cos(ang)], axis=0)
    sin = jnp.concatenate([jnp.zeros((lc, MLA_ROPE // 2), F32), jnp.sin(ang)], axis=0)
    s = lc + l
    pad = jnp.zeros((s, LANES - MLA_NOPE - MLA_ROPE), F32)
    z16 = jnp.zeros((s, MLA_ROPE // 2), F32)
    zn = jnp.zeros((s, MLA_NOPE), F32)
    tc = jnp.concatenate([jnp.ones((s, MLA_NOPE), F32), cos, cos, pad], axis=1)
    ts1 = jnp.concatenate([zn, -sin, z16, pad], axis=1)
    ts2 = jnp.concatenate([zn, z16, sin, pad], axis=1)
    return tc, ts1, ts2


def _slot_cols(w, per_head):
    k = w.shape[0]
    w = w.reshape(k, MLA_HEADS, per_head)
    return jnp.pad(w, ((0, 0), (0, 0), (0, HEAD_SLOT - per_head))).reshape(k, MLA_HEADS * HEAD_SLOT)


def _block_diag(blocks):
    n = len(blocks)
    rows = []
    for i, blk in enumerate(blocks):
        rows.append(jnp.concatenate(
            [blk if j == i else jnp.zeros((blk.shape[0], blocks[j].shape[1]), blk.dtype) for j in range(n)], axis=1))
    return jnp.concatenate(rows, axis=0)


def _row(x):
    return x.reshape(1, -1).astype(F32)


def kernel(x, c, ctx, c_ctx, norm1_g, norm2_g, w_ada, b_ada, w_in, pool_w, pool_scale, mla_q_norm, mla_w_uq, mla_kv_norm, mla_w_ukv, qk_gain_q, qk_gain_k, rwkv_mu, rwkv_w0, rwkv_w2, rwkv_a0, rwkv_a2, rwkv_ka, rwkv_kk, rwkv_rk, rwkv_g2, rwkv_ln_w, rwkv_ln_b, w_br_pool, w_br_mla, w_br_rwkv, w_o, mlp_w1, mlp_w2):
    b, l, d = x.shape
    lc = ctx.shape[1]
    depth = w_in.shape[0]
    tl = _Tiles(b, lc, l, d)

    pool_width = pool_scale.shape[-1]
    q_rank = mla_q_norm.shape[-1]
    kv_rank = mla_kv_norm.shape[-1]
    rw_width = rwkv_kk.shape[-1]
    rw_in = rwkv_mu.shape[-1]

    rows = -(-(b + 1) // SUBLANES) * SUBLANES
    c_all = jnp.concatenate([c, c_ctx[None, :], jnp.zeros((rows - b - 1, d), F32)], axis=0)
    mod_all = _ada_mod(c_all, w_ada, b_ada).reshape(depth, rows, N_MOD, 1, d)
    tabs = _rope_tables(lc, l)
    seg = jnp.zeros((HEAD_SLOT, HEAD_SLOT), F32)
    seg = seg.at[:MLA_NOPE, :MLA_NOPE].set(1.0 / MLA_NOPE)
    seg = seg.at[MLA_NOPE:MLA_NOPE + MLA_ROPE, MLA_NOPE:MLA_NOPE + MLA_ROPE].set(1.0 / MLA_ROPE)
    seg = _block_diag([seg, seg]).astype(BF16)
    anti = jnp.eye(tl.tm, dtype=BF16)[::-1]

    tk = 256 if tl.s % 256 == 0 else LANES
    tq = 512
    while tq * tk > (MLA_HEADS // 2) * b * l or l % tq:
        tq //= 2
    assert tq >= 16

    x_ctx, x_lat, lat_off = ctx, x, 0
    out = None
    for i in range(depth):
        need_ctx = i < depth - 1
        off = 0 if need_ctx else tl.nct
        mod = mod_all[i]

        wi = w_in[i].astype(BF16)
        o0 = 0
        wp = wi[:, o0:o0 + pool_width]; o0 += pool_width
        wq = wi[:, o0:o0 + q_rank]; o0 += q_rank
        wkv = wi[:, o0:o0 + kv_rank + MLA_ROPE]; o0 += kv_rank + MLA_ROPE
        wkv = jnp.pad(wkv, ((0, 0), (0, LANES - MLA_ROPE)))
        wrw = wi[:, o0:o0 + rw_in]; o0 += rw_in
        wg = wi[:, o0:]

        zp, zq, zkv, zrw, gate = _in_proj(tl, x_ctx, x_lat, lat_off, _row(norm1_g[i]), mod, wp, wq, wkv, wrw, wg)

        pw_bd = _block_diag([pool_w[i, g] for g in range(pool_w.shape[1])]).astype(BF16)
        o_pool = _pool(tl, zp, pw_bd, _row(pool_scale[i]), off)

        wuq = _slot_cols(mla_w_uq[i], MLA_NOPE + MLA_ROPE).astype(BF16)
        wukv = mla_w_ukv[i].reshape(kv_rank, MLA_HEADS, MLA_NOPE + MLA_V)
        wuk = _slot_cols(wukv[:, :, :MLA_NOPE].reshape(kv_rank, -1), MLA_NOPE).astype(BF16)
        wuv = wukv[:, :, MLA_NOPE:].reshape(kv_rank, MLA_HEADS * MLA_V).astype(BF16)
        zpad = jnp.zeros((LANES - MLA_NOPE - MLA_ROPE,), F32)
        gq = _row(jnp.concatenate([qk_gain_q[i], zpad]))
        gk = _row(jnp.concatenate([qk_gain_k[i, :MLA_NOPE], jnp.zeros((LANES - MLA_NOPE,), F32)]))
        gkr = _row(jnp.concatenate([qk_gain_k[i, MLA_NOPE:], jnp.zeros((LANES - MLA_ROPE,), F32)]))
        bound = _score_bound(qk_gain_q[i], qk_gain_k[i])
        spare = (jnp.arange(LANES) == MLA_NOPE + MLA_ROPE).astype(F32)[None, :]
        smalls = (_row(mla_q_norm[i]), _row(mla_kv_norm[i]), gq, gk, gkr, seg, spare, -bound * spare)
        q, k, v = _qkv_up(tl, zq, zkv, tabs, smalls, wuq, wuk, wuv)

        w2cat = _block_diag([rwkv_w2[i, 0], rwkv_w2[i, 1]]).astype(BF16)
        a2cat = _block_diag([rwkv_a2[i, 0], rwkv_a2[i, 1]]).astype(BF16)
        rsmalls = (rwkv_mu[i].astype(F32), _row(rwkv_kk[i]), _row(rwkv_w0[i]), w2cat, _row(rwkv_a0[i]), a2cat,
                   _row(rwkv_ka[i]), _row(rwkv_rk[i]), rwkv_g2[i].astype(BF16), anti)
        r, vv, nkk, gg, bonus, wf, kf, bf, rb, vb, nkkb, wb, kb, bb = _rwkv_prep(tl, zrw, rsmalls)
        n_steps = b * (MLA_HEADS // 2) * (l // tq) * (tl.s // tk)
        ops_k, op_v = _scan_operands((r, vv, nkk, wf, kf, bf), (rb, vb, nkkb, wb, kb, bb), n_steps)
        o_mla_l, y = _attn_scan(q, k, v, ops_k, op_v, bound, tq=tq, tk=tk, n_q=l)
        y = _scan_result(y, b, tl.s, rw_width)
        if need_ctx:
            o_mla_c = _attention(q, k, v, tq=tl.tm, q_off_tiles=l // tl.tm, n_q_tiles=tl.nct, n_keys=lc)
        else:
            o_mla_c = o_mla_l

        x1 = _merge(tl, off, y, bonus, gg, o_pool, o_mla_c, o_mla_l, gate, x_ctx, x_lat, lat_off, mod,
                    _row(rwkv_ln_w[i]), _row(rwkv_ln_b[i]),
                    w_br_pool[i].astype(BF16), w_br_mla[i].astype(BF16), w_br_rwkv[i].astype(BF16),
                    w_o[i].astype(BF16))
        xc_next = _mlp(tl, x1, off, _row(norm2_g[i]), mod, mlp_w1[i].astype(BF16), mlp_w2[i].astype(BF16))
        if need_ctx:
            x_ctx, x_lat, lat_off = xc_next, xc_next, tl.nct
        else:
            out = xc_next
    return out
```

```python
import functools
import math

import jax
import jax.numpy as jnp
from jax import lax
from jax.experimental import pallas as pl
from jax.experimental.pallas import tpu as pltpu

F32 = jnp.float32
BF16 = jnp.bfloat16

NORM_EPS = 1e-6
RWKV_GN_EPS = 64e-5
GRID_W = 64
ROPE_BASE = 10000.0
POOL_HALF_WINDOWS = (1, 2, 4, 8)
N_MOD = 6
MLA_HEADS = 8
MLA_NOPE = 64
MLA_ROPE = 32
MLA_V = 64
RWKV_HEAD = 64
DECAY_RANK = 64
AAA_RANK = 64
GATE_RANK = 128

LANES = 128
SUBLANES = 8
HEAD_SLOT = LANES
HALO = SUBLANES
VMEM_LIMIT = 56 * 1024 * 1024

LOG2E = 1.4426950408889634


def _dot(a, b):
    return jnp.dot(a, b, preferred_element_type=F32)


def _sigmoid(x):
    return 1.0 / (1.0 + jnp.exp(-x))


def _rms(x, width):
    return lax.rsqrt(jnp.sum(x * x, axis=-1, keepdims=True) * (1.0 / width) + NORM_EPS)


def _norm_mod(x, g, shift, scale):
    return (x * _rms(x, x.shape[-1]) * g) * (1.0 + scale) + shift


def _lane(shape):
    return lax.broadcasted_iota(jnp.int32, shape, 1)


def _seg64_sum(x):
    cols = []
    for c in range(x.shape[1] // LANES):
        xc = x[:, c * LANES:(c + 1) * LANES]
        lo_m = _lane(xc.shape) < 64
        lo = jnp.sum(jnp.where(lo_m, xc, 0.0), axis=-1, keepdims=True)
        hi = jnp.sum(jnp.where(lo_m, 0.0, xc), axis=-1, keepdims=True)
        cols.append(jnp.where(lo_m, lo, hi))
    return cols[0] if len(cols) == 1 else jnp.concatenate(cols, axis=1)


def _params(sem, **flags):
    return pltpu.CompilerParams(dimension_semantics=sem, vmem_limit_bytes=VMEM_LIMIT, flags=flags or None)


class _Tiles:
    def __init__(self, batch, lc, l, d):
        self.batch, self.lc, self.l, self.d = batch, lc, l, d
        self.s = lc + l
        self.tm = 256 if (lc % 256 == 0 and l % 256 == 0) else 128
        assert lc % self.tm == 0 and l % self.tm == 0
        self.nct = lc // self.tm
        self.ns = self.s // self.tm
        self.ctx_row = batch

    def grid(self, off):
        return (self.batch, self.ns - off)

    def tok(self, width, off):
        return pl.BlockSpec((None, self.tm, width), lambda b, s: (b, s + off, 0))

    def tok_split(self, width, off, lat_off):
        nct = self.nct
        ctx = pl.BlockSpec((None, self.tm, width), lambda b, s: (b, jnp.minimum(s + off, nct - 1), 0))
        lat = pl.BlockSpec((None, self.tm, width), lambda b, s: (b, jnp.maximum(s + off - nct, 0) + lat_off, 0))
        return [ctx, lat]

    def pick(self, off, ctx_ref, lat_ref):
        return jnp.where(pl.program_id(1) + off < self.nct, ctx_ref[...], lat_ref[...])

    def halo_prev(self, width, off):
        r = self.tm // HALO
        return pl.BlockSpec((None, HALO, width), lambda b, s: (b, jnp.maximum((s + off) * r - 1, 0), 0))

    def halo_next(self, width, off):
        r = self.tm // HALO
        last = self.s // HALO - 1
        return pl.BlockSpec((None, HALO, width), lambda b, s: (b, jnp.minimum((s + off + 1) * r, last), 0))

    def mod(self, j, off):
        nct, ctx_row = self.nct, self.ctx_row
        return pl.BlockSpec((None, None, 1, self.d),
                            lambda b, s: (jnp.where(s + off < nct, ctx_row, b), j, 0, 0))

    def const(self, shape):
        nd = len(shape)
        return pl.BlockSpec(shape, lambda b, s: (0,) * nd)


def _ada_kernel(c_ref, w_ref, b_ref, o_ref):
    c = c_ref[...]
    s = (c * _sigmoid(c)).astype(BF16)
    o_ref[...] = _dot(s, w_ref[...].astype(BF16)) + b_ref[...]


def _ada_mod(c_all, w_ada, b_ada):
    depth, d, n = w_ada.shape
    rows = c_all.shape[0]
    tn = 1024
    return pl.pallas_call(
        _ada_kernel,
        grid=(depth, n // tn),
        in_specs=[pl.BlockSpec((rows, d), lambda i, j: (0, 0)),
                  pl.BlockSpec((None, d, tn), lambda i, j: (i, 0, j)),
                  pl.BlockSpec((None, 1, tn), lambda i, j: (i, 0, j))],
        out_specs=pl.BlockSpec((None, rows, tn), lambda i, j: (i, 0, j)),
        out_shape=jax.ShapeDtypeStruct((depth, rows, n), F32),
        compiler_params=_params(("parallel", "parallel")),
        name="ada_mod",
    )(c_all, w_ada, b_ada.reshape(depth, 1, n))


def _in_proj_kernel(xc_ref, xl_ref, g_ref, sh_ref, sc_ref, wp_ref, wq_ref, wkv_ref, wrw_ref, wg_ref,
                    zp_ref, zq_ref, zkv_ref, zrw_ref, gate_ref, *, tl):
    h = _norm_mod(tl.pick(0, xc_ref, xl_ref), g_ref[...], sh_ref[...], sc_ref[...]).astype(BF16)
    zp_ref[...] = _dot(h, wp_ref[...])
    zq_ref[...] = _dot(h, wq_ref[...])
    zkv_ref[...] = _dot(h, wkv_ref[...])
    zrw_ref[...] = _dot(h, wrw_ref[...])
    d = h.shape[-1]
    for c in range(wg_ref.shape[1] // d):
        gate_ref[:, c * d:(c + 1) * d] = _sigmoid(_dot(h, wg_ref[:, c * d:(c + 1) * d])).astype(BF16)


def _in_proj(tl, x_ctx, x_lat, lat_off, g, mod, wp, wq, wkv, wrw, wg):
    b, d = x_ctx.shape[0], x_ctx.shape[-1]
    widths = (wp.shape[1], wq.shape[1], wkv.shape[1], wrw.shape[1], wg.shape[1])
    dts = (F32, F32, F32, F32, BF16)
    return pl.pallas_call(
        functools.partial(_in_proj_kernel, tl=tl),
        grid=tl.grid(0),
        in_specs=tl.tok_split(d, 0, lat_off) + [tl.const((1, d)), tl.mod(0, 0), tl.mod(1, 0)]
        + [tl.const(w.shape) for w in (wp, wq, wkv, wrw, wg)],
        out_specs=[tl.tok(w, 0) for w in widths],
        out_shape=[jax.ShapeDtypeStruct((b, tl.s, w), dt) for w, dt in zip(widths, dts)],
        compiler_params=_params(("parallel", "parallel")),
        name="in_proj",
    )(x_ctx, x_lat, g, mod, mod, wp, wq, wkv, wrw, wg)


def _seq_flags(tl, off):
    s_abs = pl.program_id(1) + off
    is_ctx = s_abs < tl.nct
    first = jnp.logical_or(s_abs == 0, s_abs == tl.nct)
    last = jnp.logical_or(s_abs == tl.nct - 1, s_abs == tl.ns - 1)
    seq_len = jnp.where(is_ctx, tl.lc, tl.l)
    tile_in_seq = jnp.where(is_ctx, s_abs, s_abs - tl.nct)
    return first, last, seq_len, tile_in_seq


def _pool_kernel(u_ref, up_ref, un_ref, pw_ref, ps_ref, o_ref, *, tl, off):
    first, last, seq_len, tile_in_seq = _seq_flags(tl, off)
    tm = tl.tm
    u = u_ref[...]
    prev = jnp.where(first, 0.0, up_ref[...])
    nxt = jnp.where(last, 0.0, un_ref[...])
    e = jnp.concatenate([prev, u, nxt], axis=0)
    n = tm + 2 * HALO
    w2 = e + pltpu.roll(e, 1, 0)
    w4 = pltpu.roll(w2, n - 1, 0) + pltpu.roll(w2, 1, 0)
    w8 = pltpu.roll(w4, n - 2, 0) + pltpu.roll(w4, 2, 0)
    w16 = pltpu.roll(w8, n - 4, 0) + pltpu.roll(w8, 4, 0)
    sums = [w[HALO:HALO + tm] for w in (w2, w4, w8, w16)]
    width = u.shape[1]
    group = width // len(POOL_HALF_WINDOWS)
    lane = _lane((tm, width))
    pos = tile_in_seq * tm + lax.broadcasted_iota(jnp.int32, (tm, width), 0)
    total = sums[-1]
    half = jnp.full((tm, width), POOL_HALF_WINDOWS[-1], jnp.int32)
    for gi in range(len(POOL_HALF_WINDOWS) - 2, -1, -1):
        sel = lane < (gi + 1) * group
        total = jnp.where(sel, sums[gi], total)
        half = jnp.where(sel, POOL_HALF_WINDOWS[gi], half)
    cnt = jnp.minimum(pos + half, seq_len) - jnp.maximum(pos - half, 0)
    pooled = total / cnt.astype(F32) - u
    o_ref[...] = (_dot(pooled.astype(BF16), pw_ref[...]) * ps_ref[...]).astype(BF16)


def _pool(tl, zp, pw_bd, ps, off):
    b, s, w = zp.shape
    return pl.pallas_call(
        functools.partial(_pool_kernel, tl=tl, off=off),
        grid=tl.grid(off),
        in_specs=[tl.tok(w, off), tl.halo_prev(w, off), tl.halo_next(w, off),
                  tl.const(pw_bd.shape), tl.const(ps.shape)],
        out_specs=tl.tok(w, off),
        out_shape=jax.ShapeDtypeStruct((b, s, w), BF16),
        compiler_params=_params(("parallel", "parallel")),
        name="pool_mixer",
    )(zp, zp, zp, pw_bd, ps)


def _rope(x, tc, ts1, ts2):
    return x * tc + pltpu.roll(x, LANES - MLA_ROPE // 2, 1) * ts1 + pltpu.roll(x, MLA_ROPE // 2, 1) * ts2


def _head_inv_rms(x, seg):
    w = seg.shape[0]
    cols = [lax.rsqrt(_dot((xc * xc).astype(BF16), seg) + NORM_EPS)
            for xc in (x[:, c * w:(c + 1) * w] for c in range(x.shape[1] // w))]
    return jnp.concatenate(cols, axis=1)


def _qkv_kernel(zq_ref, zkv_ref, tc_ref, ts1_ref, ts2_ref, qng_ref, kvng_ref, gq_ref, gk_ref, gkr_ref, seg_ref,
                qpad_ref, kpad_ref, wuq_ref, wuk_ref, wuv_ref, q_ref, k_ref, v_ref):
    tc, ts1, ts2 = tc_ref[...], ts1_ref[...], ts2_ref[...]
    seg = seg_ref[...]

    zq = zq_ref[...]
    qc = (zq * _rms(zq, zq.shape[-1]) * qng_ref[...]).astype(BF16)
    q = _dot(qc, wuq_ref[...])
    q = q * _head_inv_rms(q, seg)
    q_scale = LOG2E * (MLA_NOPE + MLA_ROPE) ** -0.5
    for h in range(MLA_HEADS):
        qh = _rope(q[:, h * HEAD_SLOT:(h + 1) * HEAD_SLOT] * gq_ref[...], tc, ts1, ts2)
        q_ref[:, h * HEAD_SLOT:(h + 1) * HEAD_SLOT] = (qh * q_scale + qpad_ref[...]).astype(BF16)

    zkv = zkv_ref[...]
    kv_w = kvng_ref.shape[-1]
    kvc = zkv[:, :kv_w]
    kvn = (kvc * _rms(kvc, kv_w) * kvng_ref[...]).astype(BF16)
    kr = zkv[:, kv_w:kv_w + LANES]
    kr = kr * lax.rsqrt(jnp.sum(kr * kr, axis=-1, keepdims=True) * (1.0 / MLA_ROPE) + NORM_EPS) * gkr_ref[...]
    kr = _rope(pltpu.roll(kr, MLA_NOPE, 1), tc, ts1, ts2)
    kn = _dot(kvn, wuk_ref[...])
    kn = kn * _head_inv_rms(kn, seg)
    for h in range(MLA_HEADS):
        kh = kn[:, h * HEAD_SLOT:(h + 1) * HEAD_SLOT]
        k_ref[:, h * HEAD_SLOT:(h + 1) * HEAD_SLOT] = (kh * gk_ref[...] + kr + kpad_ref[...]).astype(BF16)
    v_ref[...] = _dot(kvn, wuv_ref[...]).astype(BF16)


def _qkv_up(tl, zq, zkv, tabs, smalls, wuq, wuk, wuv):
    b, s, _ = zq.shape
    tc, ts1, ts2 = tabs
    tab_spec = pl.BlockSpec((tl.tm, LANES), lambda bb, ss: (ss, 0))
    widths = (wuq.shape[1], wuk.shape[1], wuv.shape[1])
    nct, nlt = tl.nct, tl.ns - tl.nct
    q_spec = pl.BlockSpec((None, tl.tm, widths[0]), lambda bb, ss: (bb, jnp.where(ss < nct, nlt + ss, ss - nct), 0))
    return pl.pallas_call(
        _qkv_kernel,
        grid=tl.grid(0),
        in_specs=[tl.tok(zq.shape[-1], 0), tl.tok(zkv.shape[-1], 0), tab_spec, tab_spec, tab_spec]
        + [tl.const(a.shape) for a in smalls] + [tl.const(w.shape) for w in (wuq, wuk, wuv)],
        out_specs=[q_spec, tl.tok(widths[1], 0), tl.tok(widths[2], 0)],
        out_shape=[jax.ShapeDtypeStruct((b, s, w), BF16) for w in widths],
        compiler_params=_params(("parallel", "parallel")),
        name="qkv_up",
    )(zq, zkv, tc, ts1, ts2, *smalls, wuq, wuk, wuv)


def _attn_kernel(q_ref, k_ref, v_ref, o_ref):
    v = v_ref[...]
    outs = []
    for hh in range(2):
        q = q_ref[:, hh * HEAD_SLOT:(hh + 1) * HEAD_SLOT]
        k = k_ref[:, hh * HEAD_SLOT:(hh + 1) * HEAD_SLOT]
        s = lax.dot_general(q, k, (((1,), (1,)), ((), ())), preferred_element_type=F32)
        p = jnp.exp2(s - jnp.max(s, axis=-1, keepdims=True))
        denom = jnp.sum(p, axis=-1, keepdims=True)
        outs.append(_dot(p.astype(BF16), v) / denom)
    o_ref[...] = jnp.where(_lane(outs[0].shape) < MLA_V, outs[0], outs[1]).astype(BF16)


def _attention(q, k, v, *, tq, q_off_tiles, n_q_tiles, n_keys):
    b, _, qw = q.shape
    pairs = qw // (2 * HEAD_SLOT)
    return pl.pallas_call(
        _attn_kernel,
        grid=(b, pairs, n_q_tiles),
        in_specs=[pl.BlockSpec((None, tq, 2 * HEAD_SLOT), lambda bb, hp, i: (bb, i + q_off_tiles, hp)),
                  pl.BlockSpec((None, n_keys, 2 * HEAD_SLOT), lambda bb, hp, i: (bb, 0, hp)),
                  pl.BlockSpec((None, n_keys, 2 * MLA_V), lambda bb, hp, i: (bb, 0, hp))],
        out_specs=pl.BlockSpec((None, tq, 2 * MLA_V), lambda bb, hp, i: (bb, i, hp)),
        out_shape=jax.ShapeDtypeStruct((b, n_q_tiles * tq, pairs * 2 * MLA_V), BF16),
        compiler_params=_params(("parallel", "parallel", "arbitrary")),
        name="attention",
    )(q, k, v)


def _rev_rows(x):
    n = x.shape[0]
    row = lax.broadcasted_iota(jnp.int32, x.shape, 0)
    for sh in (1, 2, 4):
        x = jnp.where((row & sh) == 0, pltpu.roll(x, n - sh, 0), pltpu.roll(x, sh, 0))
    groups = [x[g * SUBLANES:(g + 1) * SUBLANES] for g in range(n // SUBLANES)]
    return jnp.concatenate(groups[::-1], axis=0)


def _rev_tile(tl, s_abs):
    return jnp.where(s_abs < tl.nct, tl.nct - 1 - s_abs, tl.ns - 1 + tl.nct - s_abs)


def _rwkv_prep_kernel(z_ref, zp_ref, zn_ref, mu_ref, kkw_ref, w0_ref, w2_ref, a0_ref, a2_ref, ka_ref,
                      rk_ref, g2_ref, anti_ref,
                      r_o, v_o, nkk_o, gg_o, bonus_o, wf_o, kf_o, bf_o, rb_o, vb_o, nkkb_o, wb_o, kb_o, bb_o,
                      *, tl, off):
    first, last, _, _ = _seq_flags(tl, off)
    tm = tl.tm
    z = z_ref[...]
    row = lax.broadcasted_iota(jnp.int32, z.shape, 0)
    prev_row = jnp.where(first, 0.0, zp_ref[HALO - 1:HALO, :])
    next_row = jnp.where(last, 0.0, zn_ref[0:1, :])
    z_prev = jnp.where(row == 0, prev_row, pltpu.roll(z, 1, 0))
    z_next = jnp.where(row == tm - 1, next_row, pltpu.roll(z, tm - 1, 0))
    zs = z + mu_ref[0:1, :] * (z_prev - z) + mu_ref[1:2, :] * (z_next - z)

    w = kkw_ref.shape[-1]
    r, k, v = zs[:, 0:w], zs[:, w:2 * w], zs[:, 2 * w:3 * w]
    o = 3 * w
    wd = zs[:, o:o + 2 * DECAY_RANK]
    ad = zs[:, o + 2 * DECAY_RANK:o + 2 * DECAY_RANK + 2 * AAA_RANK]
    gd = zs[:, o + 2 * DECAY_RANK + 2 * AAA_RANK:]

    kk = k * kkw_ref[...]
    kk = kk * lax.rsqrt(jnp.maximum(_seg64_sum(kk * kk), 1e-24))
    u = w0_ref[...] + _dot(jnp.tanh(wd).astype(BF16), w2_ref[...])
    log_decay = -math.exp(-0.5) * _sigmoid(u)
    a = _sigmoid(a0_ref[...] + _dot(ad.astype(BF16), a2_ref[...]))
    ka = ka_ref[...]
    k_sum = jnp.zeros_like(k)
    anti = anti_ref[...]

    def rev16(t):
        return _dot(anti, t.astype(BF16)).astype(BF16)

    for d, (w_o, k_o, b_o) in enumerate(((wf_o, kf_o, bf_o), (wb_o, kb_o, bb_o))):
        a_d = a[:, d * w:(d + 1) * w]
        k_d = k * (1.0 + (a_d - 1.0) * ka[:, d * w:(d + 1) * w])
        lw = log_decay[:, d * w:(d + 1) * w]
        w_o[...] = rev16(lw) if d else lw.astype(BF16)
        k_o[...] = rev16(k_d) if d else k_d.astype(BF16)
        b_o[...] = rev16(kk * a_d) if d else (kk * a_d).astype(BF16)
        k_sum = k_sum + k_d
    r_o[...] = r.astype(BF16)
    v_o[...] = v
    nkk_o[...] = (-kk).astype(BF16)
    rb_o[...] = rev16(r)
    vb_o[...] = _rev_rows(v)
    nkkb_o[...] = rev16(-kk)
    gg_o[...] = _dot(_sigmoid(gd).astype(BF16), g2_ref[...])
    bonus_o[...] = _seg64_sum(r * (0.5 * k_sum) * rk_ref[...]) * v


def _rwkv_prep(tl, zrw, smalls):
    b, s, win = zrw.shape
    w = smalls[1].shape[-1]
    n_fwd, n_bwd = 8, 6
    bwd_spec = pl.BlockSpec((None, tl.tm, w), lambda bb, ss: (bb, _rev_tile(tl, ss), 0))
    return pl.pallas_call(
        functools.partial(_rwkv_prep_kernel, tl=tl, off=0),
        grid=tl.grid(0),
        in_specs=[tl.tok(win, 0), tl.halo_prev(win, 0), tl.halo_next(win, 0)]
        + [tl.const(a.shape) for a in smalls],
        out_specs=[tl.tok(w, 0)] * n_fwd + [bwd_spec] * n_bwd,
        out_shape=[jax.ShapeDtypeStruct((b, s, w), dt) for dt in
                   (BF16, F32, BF16, F32, F32, BF16, BF16, BF16, BF16, F32, BF16, BF16, BF16, BF16)],
        compiler_params=_params(("parallel", "parallel")),
        name="rwkv_prep",
    )(zrw, zrw, zrw, *smalls)


def _zero_after(x):
    bits = pltpu.bitcast(x[:SUBLANES, :LANES].astype(F32), jnp.uint32)
    return pltpu.bitcast((bits >> 16) >> 16, F32)


def _scan_step(t, a_ref, b_ref, k_ref, r_ref, v_ref, y_ref, s_ref, after):
    n_k = s_ref.shape[0]
    v = v_ref[t] + jnp.concatenate([after] * (v_ref.shape[1] // SUBLANES), axis=0)
    acc = [jnp.zeros_like(v), jnp.zeros_like(v)]
    for kk in range(n_k):
        acc[kk % 2] = acc[kk % 2] + s_ref[kk] * a_ref[t, pl.ds(kk, 1), :]
    sa = acc[0] + acc[1]
    yacc = [jnp.zeros_like(v), jnp.zeros_like(v)]
    for kk in range(n_k):
        sn = s_ref[kk] + sa * b_ref[t, pl.ds(kk, 1), :] + v * k_ref[t, pl.ds(kk, 1), :]
        s_ref[kk] = sn
        yacc[kk % 2] = yacc[kk % 2] + sn * r_ref[t, pl.ds(kk, 1), :]
    y_ref[t] = yacc[0] + yacc[1]


def _attn_scan_kernel(static_ref, q_ref, k_ref, v_ref, ab_ref, kr_ref, lw_ref, vv_ref, o_ref, y_ref,
                      s_ref, m_ref, acc_ref, ops_ref, *, tk):
    first = jnp.logical_and(pl.program_id(0) == 0, jnp.logical_and(pl.program_id(1) == 0, pl.program_id(2) == 0))

    @pl.when(first)
    def _():
        s_ref[...] = jnp.zeros_like(s_ref)

    def unpack_pair(ref):
        x = ref[...].astype(F32)
        swapped = pltpu.roll(x.reshape(-1, LANES), LANES // 2, 1).reshape(x.shape)
        lo = lax.broadcasted_iota(jnp.int32, x.shape, 2) < LANES // 2
        return jnp.where(lo, x, swapped), jnp.where(lo, swapped, x)

    nkv = vv_ref.shape[0]
    a, b = unpack_pair(ab_ref)
    kd, r = unpack_pair(kr_ref)
    log_w = jnp.zeros(lw_ref.shape[1:], F32)
    for t in range(nkv):
        ops_ref[0, t] = a[t] * jnp.exp(log_w)
        log_w = log_w + lw_ref[t].astype(F32)
        inv_w = jnp.exp(-log_w)
        ops_ref[1, t] = b[t] * inv_w
        ops_ref[2, t] = kd[t] * inv_w
        ops_ref[3, t] = r[t] * jnp.exp(log_w)
    ops_ref[4, 0] = jnp.exp(log_w)
    a_ref, b_ref, kk_ref, r_ref = (ops_ref.at[i] for i in range(4))

    m_ref[...] = jnp.full_like(m_ref, -1e30)
    acc_ref[...] = jnp.zeros_like(acc_ref)
    lo_half = _lane((tk, 2 * MLA_V)) < MLA_V

    def block(j, after, static_max):
        _scan_step(j, a_ref, b_ref, kk_ref, r_ref, vv_ref, y_ref, s_ref, after)
        start = j * tk
        vblk = v_ref[pl.ds(start, tk), :]
        v_ones = (jnp.where(lo_half, vblk, 1.0), jnp.where(lo_half, 1.0, vblk))
        for hh in range(2):
            q = q_ref[:, hh * HEAD_SLOT:(hh + 1) * HEAD_SLOT]
            kblk = k_ref[pl.ds(start, tk), hh * HEAD_SLOT:(hh + 1) * HEAD_SLOT]
            s = lax.dot_general(q, kblk, (((1,), (1,)), ((), ())), preferred_element_type=F32)
            if static_max:
                new = acc_ref[hh] + _dot(jnp.exp2(s).astype(BF16), v_ones[hh])
            else:
                m_old = m_ref[hh]
                m_new = jnp.maximum(m_old, jnp.max(s, axis=-1, keepdims=True))
                p = jnp.exp2(s - jnp.concatenate([m_new] * (tk // LANES), axis=1))
                new = jnp.exp2(m_old - m_new) * acc_ref[hh] + _dot(p.astype(BF16), v_ones[hh])
                m_ref[hh] = m_new
            acc_ref[hh] = new
        return _zero_after(new)

    def run(static_max):
        after = jnp.zeros((SUBLANES, LANES), F32)
        for j in range(nkv):
            after = block(j, after, static_max)

    use_static = static_ref[0] != 0
    pl.when(use_static)(functools.partial(run, True))
    pl.when(jnp.logical_not(use_static))(functools.partial(run, False))
    for kk in range(s_ref.shape[0]):
        s_ref[kk] = s_ref[kk] * ops_ref[4, 0, pl.ds(kk, 1), :]
    a0, a1 = acc_ref[0], acc_ref[1]
    o = jnp.where(_lane(a0.shape) < MLA_V, a0 / pltpu.roll(a0, MLA_V, 1), a1 / pltpu.roll(a1, MLA_V, 1))
    o_ref[...] = o.astype(BF16)


def _score_bound(gain_q, gain_k):
    def norm2(g):
        return MLA_NOPE * jnp.max(jnp.square(g[:MLA_NOPE])) + MLA_ROPE * jnp.max(jnp.square(g[MLA_NOPE:]))
    q_scale = LOG2E * (MLA_NOPE + MLA_ROPE) ** -0.5
    return 1.02 * q_scale * jnp.sqrt(norm2(gain_q) * norm2(gain_k))


STATIC_SOFTMAX_MAX_BOUND = 50.0


def _attn_scan(q, k, v, ops_k, op_v, bound, *, tq, tk, n_q):
    b, s, qw = q.shape[0], k.shape[1], q.shape[2]
    pairs = qw // (2 * HEAD_SLOT)
    nq_t, nkv = n_q // tq, s // tk
    n_k, nv = ops_k[0].shape[1], op_v.shape[1]
    assert op_v.shape[0] == b * pairs * nq_t * nkv and op_v.shape[2] == LANES

    def step_idx(bb, hp, i):
        return ((bb * pairs + hp) * nq_t + i, 0, 0)

    kspec = pl.BlockSpec((nkv, n_k, LANES), step_idx)
    vspec = pl.BlockSpec((nkv, nv, LANES), step_idx)

    use_static = (bound <= STATIC_SOFTMAX_MAX_BOUND).astype(jnp.int32).reshape(1)
    return pl.pallas_call(
        functools.partial(_attn_scan_kernel, tk=tk),
        grid=(b, pairs, nq_t),
        in_specs=[pl.BlockSpec(memory_space=pltpu.SMEM),
                  pl.BlockSpec((None, tq, 2 * HEAD_SLOT), lambda bb, hp, i: (bb, i, hp)),
                  pl.BlockSpec((None, s, 2 * HEAD_SLOT), lambda bb, hp, i: (bb, 0, hp)),
                  pl.BlockSpec((None, s, 2 * MLA_V), lambda bb, hp, i: (bb, 0, hp))]
        + [kspec] * len(ops_k) + [vspec],
        out_specs=[pl.BlockSpec((None, tq, 2 * MLA_V), lambda bb, hp, i: (bb, i, hp)), vspec],
        out_shape=[jax.ShapeDtypeStruct((b, n_q, pairs * 2 * MLA_V), BF16), jax.ShapeDtypeStruct(op_v.shape, F32)],
        scratch_shapes=[pltpu.VMEM((n_k, nv, LANES), F32), pltpu.VMEM((2, tq, LANES), F32),
                        pltpu.VMEM((2, tq, 2 * MLA_V), F32), pltpu.VMEM((5, nkv, n_k, LANES), F32)],
        compiler_params=_params(("arbitrary", "arbitrary", "arbitrary")),
        name="attn_scan",
    )(use_static, q, k, v, *ops_k, op_v)


def _scan_operands(fwd, bwd, n_steps):
    r, v, nkk, w_f, k_f, b_f = fwd
    r_b, v_b, nkk_b, w_b, k_b, b_b = bwd
    b, s, width = r.shape
    heads = width // RWKV_HEAD
    half = RWKV_HEAD // 2
    nlh = 2 * b * heads
    assert nlh <= LANES // 2

    def pad(x):
        return jnp.pad(x, ((0, n_steps - s), (0, 0), (0, 0), (0, LANES // 2 - nlh))).reshape(n_steps, -1, LANES)

    def k_half(x_f, x_b):
        z = jnp.stack([x_f, x_b]).reshape(2, b, s, heads, RWKV_HEAD)
        return z.transpose(2, 4, 0, 1, 3).reshape(s, RWKV_HEAD, nlh)

    def pair(lo, hi):
        return pad(jnp.stack([lo, hi], axis=2))

    lw = k_half(w_f, w_b)
    ops_k = [pair(k_half(nkk, nkk_b), k_half(b_f, b_b)), pair(k_half(k_f, k_b), k_half(r, r_b)), pair(lw, lw)]
    z = jnp.stack([v, v_b]).reshape(2, b, s, heads, 2, half)
    return ops_k, pad(z.transpose(2, 5, 4, 0, 1, 3).reshape(s, half, 2, nlh))


def _scan_result(y, b, s, width):
    heads = width // RWKV_HEAD
    half = RWKV_HEAD // 2
    y = y[:s].reshape(s, half, 2, LANES // 2)[..., :2 * b * heads]
    return y.reshape(s, half, 2, 2, b, heads).transpose(3, 4, 0, 5, 2, 1).reshape(2, b, s, width)


def _merge_kernel(yf_ref, yb_ref, bonus_ref, gg_ref, op_ref, omc_ref, oml_ref, gate_ref, xc_ref, xl_ref, g1_ref,
                  lnw_ref, lnb_ref, wbp_ref, wbm_ref, wbr_ref, wo_ref, o_ref, *, tl, off):
    y = yf_ref[...] + _rev_rows(yb_ref[...])
    mu = _seg64_sum(y) * (1.0 / RWKV_HEAD)
    yc = y - mu
    var = _seg64_sum(yc * yc) * (1.0 / RWKV_HEAD)
    yn = yc * lax.rsqrt(var + RWKV_GN_EPS) * lnw_ref[...] + lnb_ref[...]
    o_rw = ((yn + bonus_ref[...]) * gg_ref[...]).astype(BF16)
    x = tl.pick(off, xc_ref, xl_ref)
    d = x.shape[-1]
    m = (gate_ref[:, 0:d].astype(F32) * _dot(op_ref[...], wbp_ref[...])
         + gate_ref[:, d:2 * d].astype(F32) * _dot(tl.pick(off, omc_ref, oml_ref), wbm_ref[...])
         + gate_ref[:, 2 * d:3 * d].astype(F32) * _dot(o_rw, wbr_ref[...]))
    o_ref[...] = x + g1_ref[...] * _dot(m.astype(BF16), wo_ref[...])


def _merge(tl, off, y, bonus, gg, o_pool, o_mla_ctx, o_mla_lat, gate, x_ctx, x_lat, lat_off, mod,
           lnw, lnb, wbp, wbm, wbr, wo):
    b, d = x_ctx.shape[0], x_ctx.shape[-1]
    n_tiles = tl.ns - off
    out_spec = pl.BlockSpec((None, tl.tm, d), lambda bb, ss: (bb, ss, 0))
    rw = y.shape[-1]
    yf_spec = pl.BlockSpec((None, None, tl.tm, rw), lambda bb, ss: (0, bb, ss + off, 0))
    yb_spec = pl.BlockSpec((None, None, tl.tm, rw), lambda bb, ss: (1, bb, _rev_tile(tl, ss + off), 0))
    return pl.pallas_call(
        functools.partial(_merge_kernel, tl=tl, off=off),
        grid=tl.grid(off),
        in_specs=[yf_spec, yb_spec, tl.tok(rw, off), tl.tok(rw, off), tl.tok(o_pool.shape[-1], off)]
        + tl.tok_split(o_mla_lat.shape[-1], off, 0) + [tl.tok(gate.shape[-1], off)]
        + tl.tok_split(d, off, lat_off) + [tl.mod(2, off), tl.const(lnw.shape), tl.const(lnb.shape)]
        + [tl.const(w.shape) for w in (wbp, wbm, wbr, wo)],
        out_specs=out_spec,
        out_shape=jax.ShapeDtypeStruct((b, n_tiles * tl.tm, d), F32),
        compiler_params=_params(("parallel", "parallel")),
        name="merge",
    )(y, y, bonus, gg, o_pool, o_mla_ctx, o_mla_lat, gate, x_ctx, x_lat, mod, lnw, lnb, wbp, wbm, wbr, wo)


def _mlp_kernel(x_ref, g_ref, *rest, parts, tm):
    mods, (w1_ref, w2_ref, o_ref, h_ref, acc_ref) = rest[:3 * parts], rest[3 * parts:]
    j = pl.program_id(1)

    @pl.when(j == 0)
    def _():
        for p in range(parts):
            rows = slice(p * tm, (p + 1) * tm)
            h_ref[rows, :] = _norm_mod(x_ref[rows, :], g_ref[...], mods[3 * p][...], mods[3 * p + 1][...]).astype(BF16)
        acc_ref[...] = jnp.zeros_like(acc_ref)

    a = jnp.maximum(_dot(h_ref[...], w1_ref[...]), 0.0)
    acc_ref[...] += _dot((a * a).astype(BF16), w2_ref[...])

    @pl.when(j == pl.num_programs(1) - 1)
    def _():
        for p in range(parts):
            rows = slice(p * tm, (p + 1) * tm)
            o_ref[rows, :] = x_ref[rows, :] + mods[3 * p + 2][...] * acc_ref[rows, :]


def _mlp(tl, x1, mod_off, g, mod, w1, w2):
    b, s1, d = x1.shape
    dff = w1.shape[1]
    tf = 1024 if dff % 1024 == 0 else dff
    tm = tl.tm
    per_batch = s1 // tm
    parts = next(p for p in (4, 2, 1) if (b * per_batch) % p == 0)
    nct, ctx_row = tl.nct, tl.ctx_row

    def mod_spec(j, p):
        def idx(ii, jj):
            sub = ii * parts + p
            bb, ss = sub // per_batch, sub % per_batch
            return (jnp.where(ss + mod_off < nct, ctx_row, bb), j, 0, 0)
        return pl.BlockSpec((None, None, 1, d), idx)

    mod_specs = [mod_spec(j, p) for p in range(parts) for j in (3, 4, 5)]
    x_spec = pl.BlockSpec((parts * tm, d), lambda ii, jj: (ii, 0))
    out = pl.pallas_call(
        functools.partial(_mlp_kernel, parts=parts, tm=tm),
        grid=(b * per_batch // parts, dff // tf),
        in_specs=[x_spec, pl.BlockSpec((1, d), lambda ii, jj: (0, 0))] + mod_specs
        + [pl.BlockSpec((d, tf), lambda ii, jj: (0, jj)), pl.BlockSpec((tf, d), lambda ii, jj: (jj, 0))],
        out_specs=x_spec,
        out_shape=jax.ShapeDtypeStruct((b * s1, d), F32),
        scratch_shapes=[pltpu.VMEM((parts * tm, d), BF16), pltpu.VMEM((parts * tm, d), F32)],
        compiler_params=_params(("parallel", "arbitrary")),
        name="mlp",
    )(x1.reshape(b * s1, d), g, *([mod] * (3 * parts)), w1, w2)
    return out.reshape(b, s1, d)


def _rope_tables(lc, l):
    rows = l // GRID_W
    row = jnp.repeat(jnp.arange(rows), GRID_W).astype(F32)
    col = jnp.tile(jnp.arange(GRID_W), rows).astype(F32)
    n_freq = MLA_ROPE // 4
    inv_freq = jnp.power(ROPE_BASE, -jnp.arange(n_freq, dtype=F32) / n_freq)
    ang = jnp.concatenate([row[:, None] * inv_freq, col[:, None] * inv_freq], axis=-1)
    cos = jnp.concatenate([jnp.ones((lc, MLA_ROPE // 2), F32), jnp.cos(ang)], axis=0)
    sin = jnp.concatenate([jnp.zeros((lc, MLA_ROPE // 2), F32), jnp.sin(ang)], axis=0)
    s = lc + l
    pad = jnp.zeros((s, LANES - MLA_NOPE - MLA_ROPE), F32)
    z16 = jnp.zeros((s, MLA_ROPE // 2), F32)
    zn = jnp.zeros((s, MLA_NOPE), F32)
    tc = jnp.concatenate([jnp.ones((s, MLA_NOPE), F32), cos, cos, pad], axis=1)
    ts1 = jnp.concatenate([zn, -sin, z16, pad], axis=1)
    ts2 = jnp.concatenate([zn, z16, sin, pad], axis=1)
    return tc, ts1, ts2


def _slot_cols(w, per_head):
    k = w.shape[0]
    w = w.reshape(k, MLA_HEADS, per_head)
    return jnp.pad(w, ((0, 0), (0, 0), (0, HEAD_SLOT - per_head))).reshape(k, MLA_HEADS * HEAD_SLOT)


def _block_diag(blocks):
    n = len(blocks)
    rows = []
    for i, blk in enumerate(blocks):
        rows.append(jnp.concatenate(
            [blk if j == i else jnp.zeros((blk.shape[0], blocks[j].shape[1]), blk.dtype) for j in range(n)], axis=1))
    return jnp.concatenate(rows, axis=0)


def _row(x):
    return x.reshape(1, -1).astype(F32)


def kernel(x, c, ctx, c_ctx, norm1_g, norm2_g, w_ada, b_ada, w_in, pool_w, pool_scale, mla_q_norm, mla_w_uq, mla_kv_norm, mla_w_ukv, qk_gain_q, qk_gain_k, rwkv_mu, rwkv_w0, rwkv_w2, rwkv_a0, rwkv_a2, rwkv_ka, rwkv_kk, rwkv_rk, rwkv_g2, rwkv_ln_w, rwkv_ln_b, w_br_pool, w_br_mla, w_br_rwkv, w_o, mlp_w1, mlp_w2):
    b, l, d = x.shape
    lc = ctx.shape[1]
    depth = w_in.shape[0]
    tl = _Tiles(b, lc, l, d)

    pool_width = pool_scale.shape[-1]
    q_rank = mla_q_norm.shape[-1]
    kv_rank = mla_kv_norm.shape[-1]
    rw_width = rwkv_kk.shape[-1]
    rw_in = rwkv_mu.shape[-1]

    rows = -(-(b + 1) // SUBLANES) * SUBLANES
    c_all = jnp.concatenate([c, c_ctx[None, :], jnp.zeros((rows - b - 1, d), F32)], axis=0)
    mod_all = _ada_mod(c_all, w_ada, b_ada).reshape(depth, rows, N_MOD, 1, d)
    tabs = _rope_tables(lc, l)
    seg = jnp.zeros((HEAD_SLOT, HEAD_SLOT), F32)
    seg = seg.at[:MLA_NOPE, :MLA_NOPE].set(1.0 / MLA_NOPE)
    seg = seg.at[MLA_NOPE:MLA_NOPE + MLA_ROPE, MLA_NOPE:MLA_NOPE + MLA_ROPE].set(1.0 / MLA_ROPE)
    seg = _block_diag([seg, seg]).astype(BF16)
    anti = jnp.eye(tl.tm, dtype=BF16)[::-1]

    tk = 256 if tl.s % 256 == 0 else LANES
    tq = 512
    while tq * tk > (MLA_HEADS // 2) * b * l or l % tq:
        tq //= 2
    assert tq >= 16

    x_ctx, x_lat, lat_off = ctx, x, 0
    out = None
    for i in range(depth):
        need_ctx = i < depth - 1
        off = 0 if need_ctx else tl.nct
        mod = mod_all[i]

        wi = w_in[i].astype(BF16)
        o0 = 0
        wp = wi[:, o0:o0 + pool_width]; o0 += pool_width
        wq = wi[:, o0:o0 + q_rank]; o0 += q_rank
        wkv = wi[:, o0:o0 + kv_rank + MLA_ROPE]; o0 += kv_rank + MLA_ROPE
        wkv = jnp.pad(wkv, ((0, 0), (0, LANES - MLA_ROPE)))
        wrw = wi[:, o0:o0 + rw_in]; o0 += rw_in
        wg = wi[:, o0:]

        zp, zq, zkv, zrw, gate = _in_proj(tl, x_ctx, x_lat, lat_off, _row(norm1_g[i]), mod, wp, wq, wkv, wrw, wg)

        pw_bd = _block_diag([pool_w[i, g] for g in range(pool_w.shape[1])]).astype(BF16)
        o_pool = _pool(tl, zp, pw_bd, _row(pool_scale[i]), off)

        wuq = _slot_cols(mla_w_uq[i], MLA_NOPE + MLA_ROPE).astype(BF16)
        wukv = mla_w_ukv[i].reshape(kv_rank, MLA_HEADS, MLA_NOPE + MLA_V)
        wuk = _slot_cols(wukv[:, :, :MLA_NOPE].reshape(kv_rank, -1), MLA_NOPE).astype(BF16)
        wuv = wukv[:, :, MLA_NOPE:].reshape(kv_rank, MLA_HEADS * MLA_V).astype(BF16)
        zpad = jnp.zeros((LANES - MLA_NOPE - MLA_ROPE,), F32)
        gq = _row(jnp.concatenate([qk_gain_q[i], zpad]))
        gk = _row(jnp.concatenate([qk_gain_k[i, :MLA_NOPE], jnp.zeros((LANES - MLA_NOPE,), F32)]))
        gkr = _row(jnp.concatenate([qk_gain_k[i, MLA_NOPE:], jnp.zeros((LANES - MLA_ROPE,), F32)]))
        bound = _score_bound(qk_gain_q[i], qk_gain_k[i])
        spare = (jnp.arange(LANES) == MLA_NOPE + MLA_ROPE).astype(F32)[None, :]
        smalls = (_row(mla_q_norm[i]), _row(mla_kv_norm[i]), gq, gk, gkr, seg, spare, -bound * spare)
        q, k, v = _qkv_up(tl, zq, zkv, tabs, smalls, wuq, wuk, wuv)

        w2cat = _block_diag([rwkv_w2[i, 0], rwkv_w2[i, 1]]).astype(BF16)
        a2cat = _block_diag([rwkv_a2[i, 0], rwkv_a2[i, 1]]).astype(BF16)
        rsmalls = (rwkv_mu[i].astype(F32), _row(rwkv_kk[i]), _row(rwkv_w0[i]), w2cat, _row(rwkv_a0[i]), a2cat,
                   _row(rwkv_ka[i]), _row(rwkv_rk[i]), rwkv_g2[i].astype(BF16), anti)
        r, vv, nkk, gg, bonus, wf, kf, bf, rb, vb, nkkb, wb, kb, bb = _rwkv_prep(tl, zrw, rsmalls)
        n_steps = b * (MLA_HEADS // 2) * (l // tq) * (tl.s // tk)
        ops_k, op_v = _scan_operands((r, vv, nkk, wf, kf, bf), (rb, vb, nkkb, wb, kb, bb), n_steps)
        o_mla_l, y = _attn_scan(q, k, v, ops_k, op_v, bound, tq=tq, tk=tk, n_q=l)
        y = _scan_result(y, b, tl.s, rw_width)
        if need_ctx:
            o_mla_c = _attention(q, k, v, tq=tl.tm, q_off_tiles=l // tl.tm, n_q_tiles=tl.nct, n_keys=lc)
        else:
            o_mla_c = o_mla_l

        x1 = _merge(tl, off, y, bonus, gg, o_pool, o_mla_c, o_mla_l, gate, x_ctx, x_lat, lat_off, mod,
                    _row(rwkv_ln_w[i]), _row(rwkv_ln_b[i]),
                    w_br_pool[i].astype(BF16), w_br_mla[i].astype(BF16), w_br_rwkv[i].astype(BF16),
                    w_o[i].astype(BF16))
        xc_next = _mlp(tl, x1, off, _row(norm2_g[i]), mod, mlp_w1[i].astype(BF16), mlp_w2[i].astype(BF16))
        if need_ctx:
            x_ctx, x_lat, lat_off = xc_next, xc_next, tl.nct
        else:
            out = xc_next
    return out
```

```python
import functools
import math

import jax
import jax.numpy as jnp
from jax import lax
from jax.experimental import pallas as pl
from jax.experimental.pallas import tpu as pltpu

F32 = jnp.float32
BF16 = jnp.bfloat16

NORM_EPS = 1e-6
RWKV_GN_EPS = 64e-5
GRID_W = 64
ROPE_BASE = 10000.0
POOL_HALF_WINDOWS = (1, 2, 4, 8)
N_MOD = 6
MLA_HEADS = 8
MLA_NOPE = 64
MLA_ROPE = 32
MLA_V = 64
RWKV_HEAD = 64
DECAY_RANK = 64
AAA_RANK = 64
GATE_RANK = 128

LANES = 128
SUBLANES = 8
HEAD_SLOT = LANES
HALO = SUBLANES
VMEM_LIMIT = 56 * 1024 * 1024

LOG2E = 1.4426950408889634


def _dot(a, b):
    return jnp.dot(a, b, preferred_element_type=F32)


def _sigmoid(x):
    return 1.0 / (1.0 + jnp.exp(-x))


def _rms(x, width):
    return lax.rsqrt(jnp.sum(x * x, axis=-1, keepdims=True) * (1.0 / width) + NORM_EPS)


def _norm_mod(x, g, shift, scale):
    return (x * _rms(x, x.shape[-1]) * g) * (1.0 + scale) + shift


def _lane(shape):
    return lax.broadcasted_iota(jnp.int32, shape, 1)


def _seg64_sum(x):
    cols = []
    for c in range(x.shape[1] // LANES):
        xc = x[:, c * LANES:(c + 1) * LANES]
        lo_m = _lane(xc.shape) < 64
        lo = jnp.sum(jnp.where(lo_m, xc, 0.0), axis=-1, keepdims=True)
        hi = jnp.sum(jnp.where(lo_m, 0.0, xc), axis=-1, keepdims=True)
        cols.append(jnp.where(lo_m, lo, hi))
    return cols[0] if len(cols) == 1 else jnp.concatenate(cols, axis=1)


def _params(sem, **flags):
    return pltpu.CompilerParams(dimension_semantics=sem, vmem_limit_bytes=VMEM_LIMIT, flags=flags or None)


class _Tiles:
    def __init__(self, batch, lc, l, d):
        self.batch, self.lc, self.l, self.d = batch, lc, l, d
        self.s = lc + l
        self.tm = 256 if (lc % 256 == 0 and l % 256 == 0) else 128
        assert lc % self.tm == 0 and l % self.tm == 0
        self.nct = lc // self.tm
        self.ns = self.s // self.tm
        self.ctx_row = batch

    def grid(self, off):
        return (self.batch, self.ns - off)

    def tok(self, width, off):
        return pl.BlockSpec((None, self.tm, width), lambda b, s: (b, s + off, 0))

    def tok_split(self, width, off, lat_off):
        nct = self.nct
        ctx = pl.BlockSpec((None, self.tm, width), lambda b, s: (b, jnp.minimum(s + off, nct - 1), 0))
        lat = pl.BlockSpec((None, self.tm, width), lambda b, s: (b, jnp.maximum(s + off - nct, 0) + lat_off, 0))
        return [ctx, lat]

    def pick(self, off, ctx_ref, lat_ref):
        return jnp.where(pl.program_id(1) + off < self.nct, ctx_ref[...], lat_ref[...])

    def halo_prev(self, width, off):
        r = self.tm // HALO
        return pl.BlockSpec((None, HALO, width), lambda b, s: (b, jnp.maximum((s + off) * r - 1, 0), 0))

    def halo_next(self, width, off):
        r = self.tm // HALO
        last = self.s // HALO - 1
        return pl.BlockSpec((None, HALO, width), lambda b, s: (b, jnp.minimum((s + off + 1) * r, last), 0))

    def mod(self, j, off):
        nct, ctx_row = self.nct, self.ctx_row
        return pl.BlockSpec((None, None, 1, self.d),
                            lambda b, s: (jnp.where(s + off < nct, ctx_row, b), j, 0, 0))

    def const(self, shape):
        nd = len(shape)
        return pl.BlockSpec(shape, lambda b, s: (0,) * nd)


def _ada_kernel(c_ref, w_ref, b_ref, o_ref):
    c = c_ref[...]
    s = (c * _sigmoid(c)).astype(BF16)
    o_ref[...] = _dot(s, w_ref[...].astype(BF16)) + b_ref[...]


def _ada_mod(c_all, w_ada, b_ada):
    depth, d, n = w_ada.shape
    rows = c_all.shape[0]
    tn = 1024
    return pl.pallas_call(
        _ada_kernel,
        grid=(depth, n // tn),
        in_specs=[pl.BlockSpec((rows, d), lambda i, j: (0, 0)),
                  pl.BlockSpec((None, d, tn), lambda i, j: (i, 0, j)),
                  pl.BlockSpec((None, 1, tn), lambda i, j: (i, 0, j))],
        out_specs=pl.BlockSpec((None, rows, tn), lambda i, j: (i, 0, j)),
        out_shape=jax.ShapeDtypeStruct((depth, rows, n), F32),
        compiler_params=_params(("parallel", "parallel")),
        name="ada_mod",
    )(c_all, w_ada, b_ada.reshape(depth, 1, n))


def _in_proj_kernel(xc_ref, xl_ref, g_ref, sh_ref, sc_ref, wp_ref, wq_ref, wkv_ref, wrw_ref, wg_ref,
                    zp_ref, zq_ref, zkv_ref, zrw_ref, gate_ref, *, tl):
    h = _norm_mod(tl.pick(0, xc_ref, xl_ref), g_ref[...], sh_ref[...], sc_ref[...]).astype(BF16)
    zp_ref[...] = _dot(h, wp_ref[...])
    zq_ref[...] = _dot(h, wq_ref[...])
    zkv_ref[...] = _dot(h, wkv_ref[...])
    zrw_ref[...] = _dot(h, wrw_ref[...])
    d = h.shape[-1]
    for c in range(wg_ref.shape[1] // d):
        gate_ref[:, c * d:(c + 1) * d] = _sigmoid(_dot(h, wg_ref[:, c * d:(c + 1) * d])).astype(BF16)


def _in_proj(tl, x_ctx, x_lat, lat_off, g, mod, wp, wq, wkv, wrw, wg):
    b, d = x_ctx.shape[0], x_ctx.shape[-1]
    widths = (wp.shape[1], wq.shape[1], wkv.shape[1], wrw.shape[1], wg.shape[1])
    dts = (F32, F32, F32, F32, BF16)
    return pl.pallas_call(
        functools.partial(_in_proj_kernel, tl=tl),
        grid=tl.grid(0),
        in_specs=tl.tok_split(d, 0, lat_off) + [tl.const((1, d)), tl.mod(0, 0), tl.mod(1, 0)]
        + [tl.const(w.shape) for w in (wp, wq, wkv, wrw, wg)],
        out_specs=[tl.tok(w, 0) for w in widths],
        out_shape=[jax.ShapeDtypeStruct((b, tl.s, w), dt) for w, dt in zip(widths, dts)],
        compiler_params=_params(("parallel", "parallel")),
        name="in_proj",
    )(x_ctx, x_lat, g, mod, mod, wp, wq, wkv, wrw, wg)


def _seq_flags(tl, off):
    s_abs = pl.program_id(1) + off
    is_ctx = s_abs < tl.nct
    first = jnp.logical_or(s_abs == 0, s_abs == tl.nct)
    last = jnp.logical_or(s_abs == tl.nct - 1, s_abs == tl.ns - 1)
    seq_len = jnp.where(is_ctx, tl.lc, tl.l)
    tile_in_seq = jnp.where(is_ctx, s_abs, s_abs - tl.nct)
    return first, last, seq_len, tile_in_seq


def _pool_kernel(u_ref, up_ref, un_ref, pw_ref, ps_ref, o_ref, *, tl, off):
    first, last, seq_len, tile_in_seq = _seq_flags(tl, off)
    tm = tl.tm
    u = u_ref[...]
    prev = jnp.where(first, 0.0, up_ref[...])
    nxt = jnp.where(last, 0.0, un_ref[...])
    e = jnp.concatenate([prev, u, nxt], axis=0)
    n = tm + 2 * HALO
    w2 = e + pltpu.roll(e, 1, 0)
    w4 = pltpu.roll(w2, n - 1, 0) + pltpu.roll(w2, 1, 0)
    w8 = pltpu.roll(w4, n - 2, 0) + pltpu.roll(w4, 2, 0)
    w16 = pltpu.roll(w8, n - 4, 0) + pltpu.roll(w8, 4, 0)
    sums = [w[HALO:HALO + tm] for w in (w2, w4, w8, w16)]
    width = u.shape[1]
    group = width // len(POOL_HALF_WINDOWS)
    lane = _lane((tm, width))
    pos = tile_in_seq * tm + lax.broadcasted_iota(jnp.int32, (tm, width), 0)
    total = sums[-1]
    half = jnp.full((tm, width), POOL_HALF_WINDOWS[-1], jnp.int32)
    for gi in range(len(POOL_HALF_WINDOWS) - 2, -1, -1):
        sel = lane < (gi + 1) * group
        total = jnp.where(sel, sums[gi], total)
        half = jnp.where(sel, POOL_HALF_WINDOWS[gi], half)
    cnt = jnp.minimum(pos + half, seq_len) - jnp.maximum(pos - half, 0)
    pooled = total / cnt.astype(F32) - u
    o_ref[...] = (_dot(pooled.astype(BF16), pw_ref[...]) * ps_ref[...]).astype(BF16)


def _pool(tl, zp, pw_bd, ps, off):
    b, s, w = zp.shape
    return pl.pallas_call(
        functools.partial(_pool_kernel, tl=tl, off=off),
        grid=tl.grid(off),
        in_specs=[tl.tok(w, off), tl.halo_prev(w, off), tl.halo_next(w, off),
                  tl.const(pw_bd.shape), tl.const(ps.shape)],
        out_specs=tl.tok(w, off),
        out_shape=jax.ShapeDtypeStruct((b, s, w), BF16),
        compiler_params=_params(("parallel", "parallel")),
        name="pool_mixer",
    )(zp, zp, zp, pw_bd, ps)


def _rope(x, tc, ts1, ts2):
    return x * tc + pltpu.roll(x, LANES - MLA_ROPE // 2, 1) * ts1 + pltpu.roll(x, MLA_ROPE // 2, 1) * ts2


def _head_inv_rms(x, seg):
    w = seg.shape[0]
    cols = [lax.rsqrt(_dot((xc * xc).astype(BF16), seg) + NORM_EPS)
            for xc in (x[:, c * w:(c + 1) * w] for c in range(x.shape[1] // w))]
    return jnp.concatenate(cols, axis=1)


def _qkv_kernel(zq_ref, zkv_ref, tc_ref, ts1_ref, ts2_ref, qng_ref, kvng_ref, gq_ref, gk_ref, gkr_ref, seg_ref,
                qpad_ref, kpad_ref, wuq_ref, wuk_ref, wuv_ref, q_ref, k_ref, v_ref):
    tc, ts1, ts2 = tc_ref[...], ts1_ref[...], ts2_ref[...]
    seg = seg_ref[...]

    zq = zq_ref[...]
    qc = (zq * _rms(zq, zq.shape[-1]) * qng_ref[...]).astype(BF16)
    q = _dot(qc, wuq_ref[...])
    q = q * _head_inv_rms(q, seg)
    q_scale = LOG2E * (MLA_NOPE + MLA_ROPE) ** -0.5
    for h in range(MLA_HEADS):
        qh = _rope(q[:, h * HEAD_SLOT:(h + 1) * HEAD_SLOT] * gq_ref[...], tc, ts1, ts2)
        q_ref[:, h * HEAD_SLOT:(h + 1) * HEAD_SLOT] = (qh * q_scale + qpad_ref[...]).astype(BF16)

    zkv = zkv_ref[...]
    kv_w = kvng_ref.shape[-1]
    kvc = zkv[:, :kv_w]
    kvn = (kvc * _rms(kvc, kv_w) * kvng_ref[...]).astype(BF16)
    kr = zkv[:, kv_w:kv_w + LANES]
    kr = kr * lax.rsqrt(jnp.sum(kr * kr, axis=-1, keepdims=True) * (1.0 / MLA_ROPE) + NORM_EPS) * gkr_ref[...]
    kr = _rope(pltpu.roll(kr, MLA_NOPE, 1), tc, ts1, ts2)
    kn = _dot(kvn, wuk_ref[...])
    kn = kn * _head_inv_rms(kn, seg)
    for h in range(MLA_HEADS):
        kh = kn[:, h * HEAD_SLOT:(h + 1) * HEAD_SLOT]
        k_ref[:, h * HEAD_SLOT:(h + 1) * HEAD_SLOT] = (kh * gk_ref[...] + kr + kpad_ref[...]).astype(BF16)
    v_ref[...] = _dot(kvn, wuv_ref[...]).astype(BF16)


def _qkv_up(tl, zq, zkv, tabs, smalls, wuq, wuk, wuv):
    b, s, _ = zq.shape
    tc, ts1, ts2 = tabs
    tab_spec = pl.BlockSpec((tl.tm, LANES), lambda bb, ss: (ss, 0))
    widths = (wuq.shape[1], wuk.shape[1], wuv.shape[1])
    nct, nlt = tl.nct, tl.ns - tl.nct
    q_spec = pl.BlockSpec((None, tl.tm, widths[0]), lambda bb, ss: (bb, jnp.where(ss < nct, nlt + ss, ss - nct), 0))
    return pl.pallas_call(
        _qkv_kernel,
        grid=tl.grid(0),
        in_specs=[tl.tok(zq.shape[-1], 0), tl.tok(zkv.shape[-1], 0), tab_spec, tab_spec, tab_spec]
        + [tl.const(a.shape) for a in smalls] + [tl.const(w.shape) for w in (wuq, wuk, wuv)],
        out_specs=[q_spec, tl.tok(widths[1], 0), tl.tok(widths[2], 0)],
        out_shape=[jax.ShapeDtypeStruct((b, s, w), BF16) for w in widths],
        compiler_params=_params(("parallel", "parallel")),
        name="qkv_up",
    )(zq, zkv, tc, ts1, ts2, *smalls, wuq, wuk, wuv)


def _attn_kernel(q_ref, k_ref, v_ref, o_ref):
    v = v_ref[...]
    outs = []
    for hh in range(2):
        q = q_ref[:, hh * HEAD_SLOT:(hh + 1) * HEAD_SLOT]
        k = k_ref[:, hh * HEAD_SLOT:(hh + 1) * HEAD_SLOT]
        s = lax.dot_general(q, k, (((1,), (1,)), ((), ())), preferred_element_type=F32)
        p = jnp.exp2(s - jnp.max(s, axis=-1, keepdims=True))
        denom = jnp.sum(p, axis=-1, keepdims=True)
        outs.append(_dot(p.astype(BF16), v) / denom)
    o_ref[...] = jnp.where(_lane(outs[0].shape) < MLA_V, outs[0], outs[1]).astype(BF16)


def _attention(q, k, v, *, tq, q_off_tiles, n_q_tiles, n_keys):
    b, _, qw = q.shape
    pairs = qw // (2 * HEAD_SLOT)
    return pl.pallas_call(
        _attn_kernel,
        grid=(b, pairs, n_q_tiles),
        in_specs=[pl.BlockSpec((None, tq, 2 * HEAD_SLOT), lambda bb, hp, i: (bb, i + q_off_tiles, hp)),
                  pl.BlockSpec((None, n_keys, 2 * HEAD_SLOT), lambda bb, hp, i: (bb, 0, hp)),
                  pl.BlockSpec((None, n_keys, 2 * MLA_V), lambda bb, hp, i: (bb, 0, hp))],
        out_specs=pl.BlockSpec((None, tq, 2 * MLA_V), lambda bb, hp, i: (bb, i, hp)),
        out_shape=jax.ShapeDtypeStruct((b, n_q_tiles * tq, pairs * 2 * MLA_V), BF16),
        compiler_params=_params(("parallel", "parallel", "arbitrary")),
        name="attention",
    )(q, k, v)


def _rev_rows(x):
    n = x.shape[0]
    row = lax.broadcasted_iota(jnp.int32, x.shape, 0)
    for sh in (1, 2, 4):
        x = jnp.where((row & sh) == 0, pltpu.roll(x, n - sh, 0), pltpu.roll(x, sh, 0))
    groups = [x[g * SUBLANES:(g + 1) * SUBLANES] for g in range(n // SUBLANES)]
    return jnp.concatenate(groups[::-1], axis=0)


def _rev_tile(tl, s_abs):
    return jnp.where(s_abs < tl.nct, tl.nct - 1 - s_abs, tl.ns - 1 + tl.nct - s_abs)


def _rwkv_prep_kernel(z_ref, zp_ref, zn_ref, mu_ref, kkw_ref, w0_ref, w2_ref, a0_ref, a2_ref, ka_ref,
                      rk_ref, g2_ref, anti_ref,
                      r_o, v_o, nkk_o, gg_o, bonus_o, wf_o, kf_o, bf_o, rb_o, vb_o, nkkb_o, wb_o, kb_o, bb_o,
                      *, tl, off):
    first, last, _, _ = _seq_flags(tl, off)
    tm = tl.tm
    z = z_ref[...]
    row = lax.broadcasted_iota(jnp.int32, z.shape, 0)
    prev_row = jnp.where(first, 0.0, zp_ref[HALO - 1:HALO, :])
    next_row = jnp.where(last, 0.0, zn_ref[0:1, :])
    z_prev = jnp.where(row == 0, prev_row, pltpu.roll(z, 1, 0))
    z_next = jnp.where(row == tm - 1, next_row, pltpu.roll(z, tm - 1, 0))
    zs = z + mu_ref[0:1, :] * (z_prev - z) + mu_ref[1:2, :] * (z_next - z)

    w = kkw_ref.shape[-1]
    r, k, v = zs[:, 0:w], zs[:, w:2 * w], zs[:, 2 * w:3 * w]
    o = 3 * w
    wd = zs[:, o:o + 2 * DECAY_RANK]
    ad = zs[:, o + 2 * DECAY_RANK:o + 2 * DECAY_RANK + 2 * AAA_RANK]
    gd = zs[:, o + 2 * DECAY_RANK + 2 * AAA_RANK:]

    kk = k * kkw_ref[...]
    kk = kk * lax.rsqrt(jnp.maximum(_seg64_sum(kk * kk), 1e-24))
    u = w0_ref[...] + _dot(jnp.tanh(wd).astype(BF16), w2_ref[...])
    log_decay = -math.exp(-0.5) * _sigmoid(u)
    a = _sigmoid(a0_ref[...] + _dot(ad.astype(BF16), a2_ref[...]))
    ka = ka_ref[...]
    k_sum = jnp.zeros_like(k)
    anti = anti_ref[...]

    def put16(o_ref, t, backward):
        t = t.T.astype(BF16)
        o_ref[...] = _dot(t, anti).astype(BF16) if backward else t

    for d, (w_o, k_o, b_o) in enumerate(((wf_o, kf_o, bf_o), (wb_o, kb_o, bb_o))):
        a_d = a[:, d * w:(d + 1) * w]
        k_d = k * (1.0 + (a_d - 1.0) * ka[:, d * w:(d + 1) * w])
        put16(w_o, log_decay[:, d * w:(d + 1) * w], d)
        put16(k_o, k_d, d)
        put16(b_o, kk * a_d, d)
        k_sum = k_sum + k_d
    put16(r_o, r, 0)
    put16(rb_o, r, 1)
    put16(nkk_o, -kk, 0)
    put16(nkkb_o, -kk, 1)
    v_o[...] = v.T
    vb_o[...] = _rev_rows(v).T
    gg_o[...] = _dot(_sigmoid(gd).astype(BF16), g2_ref[...])
    bonus_o[...] = _seg64_sum(r * (0.5 * k_sum) * rk_ref[...]) * v


def _rwkv_prep(tl, zrw, smalls):
    b, s, win = zrw.shape
    w = smalls[1].shape[-1]
    fwd = pl.BlockSpec((None, w, tl.tm), lambda bb, ss: (bb, 0, ss))
    bwd = pl.BlockSpec((None, w, tl.tm), lambda bb, ss: (bb, 0, _rev_tile(tl, ss)))
    tok = tl.tok(w, 0)

    def chan(dt):
        return jax.ShapeDtypeStruct((b, w, s), dt)

    tokf = jax.ShapeDtypeStruct((b, s, w), F32)
    return pl.pallas_call(
        functools.partial(_rwkv_prep_kernel, tl=tl, off=0),
        grid=tl.grid(0),
        in_specs=[tl.tok(win, 0), tl.halo_prev(win, 0), tl.halo_next(win, 0)]
        + [tl.const(a.shape) for a in smalls],
        out_specs=[fwd, fwd, fwd, tok, tok, fwd, fwd, fwd] + [bwd] * 6,
        out_shape=[chan(BF16), chan(F32), chan(BF16), tokf, tokf, chan(BF16), chan(BF16), chan(BF16),
                   chan(BF16), chan(F32), chan(BF16), chan(BF16), chan(BF16), chan(BF16)],
        compiler_params=_params(("parallel", "parallel")),
        name="rwkv_prep",
    )(zrw, zrw, zrw, *smalls)


def _zero_after(x):
    bits = pltpu.bitcast(x[:SUBLANES, :LANES].astype(F32), jnp.uint32)
    return pltpu.bitcast((bits >> 16) >> 16, F32)


def _scan_step(t, a_ref, b_ref, k_ref, r_ref, v_ref, y_ref, s_ref, after):
    n_k = s_ref.shape[0]
    v = v_ref[t] + jnp.concatenate([after] * (v_ref.shape[1] // SUBLANES), axis=0)
    acc = [jnp.zeros_like(v), jnp.zeros_like(v)]
    for kk in range(n_k):
        acc[kk % 2] = acc[kk % 2] + s_ref[kk] * a_ref[t, pl.ds(kk, 1), :]
    sa = acc[0] + acc[1]
    yacc = [jnp.zeros_like(v), jnp.zeros_like(v)]
    for kk in range(n_k):
        sn = s_ref[kk] + sa * b_ref[t, pl.ds(kk, 1), :] + v * k_ref[t, pl.ds(kk, 1), :]
        s_ref[kk] = sn
        yacc[kk % 2] = yacc[kk % 2] + sn * r_ref[t, pl.ds(kk, 1), :]
    y_ref[t] = yacc[0] + yacc[1]


def _attn_scan_kernel(static_ref, q_ref, k_ref, v_ref, ab_ref, kr_ref, lw_ref, vv_ref, o_ref, y_ref,
                      s_ref, m_ref, acc_ref, ops_ref, *, tk):
    first = jnp.logical_and(pl.program_id(0) == 0, jnp.logical_and(pl.program_id(1) == 0, pl.program_id(2) == 0))

    @pl.when(first)
    def _():
        s_ref[...] = jnp.zeros_like(s_ref)

    def unpack_pair(ref):
        x = ref[...].astype(F32)
        swapped = pltpu.roll(x.reshape(-1, LANES), LANES // 2, 1).reshape(x.shape)
        lo = lax.broadcasted_iota(jnp.int32, x.shape, 2) < LANES // 2
        return jnp.where(lo, x, swapped), jnp.where(lo, swapped, x)

    nkv = vv_ref.shape[0]
    a, b = unpack_pair(ab_ref)
    kd, r = unpack_pair(kr_ref)
    log_w = jnp.zeros(lw_ref.shape[1:], F32)
    for t in range(nkv):
        ops_ref[0, t] = a[t] * jnp.exp(log_w)
        log_w = log_w + lw_ref[t].astype(F32)
        inv_w = jnp.exp(-log_w)
        ops_ref[1, t] = b[t] * inv_w
        ops_ref[2, t] = kd[t] * inv_w
        ops_ref[3, t] = r[t] * jnp.exp(log_w)
    ops_ref[4, 0] = jnp.exp(log_w)
    a_ref, b_ref, kk_ref, r_ref = (ops_ref.at[i] for i in range(4))

    m_ref[...] = jnp.full_like(m_ref, -1e30)
    acc_ref[...] = jnp.zeros_like(acc_ref)
    lo_half = _lane((tk, 2 * MLA_V)) < MLA_V

    def block(j, after, static_max):
        _scan_step(j, a_ref, b_ref, kk_ref, r_ref, vv_ref, y_ref, s_ref, after)
        start = j * tk
        vblk = v_ref[pl.ds(start, tk), :]
        v_ones = (jnp.where(lo_half, vblk, 1.0), jnp.where(lo_half, 1.0, vblk))
        for hh in range(2):
            q = q_ref[:, hh * HEAD_SLOT:(hh + 1) * HEAD_SLOT]
            kblk = k_ref[pl.ds(start, tk), hh * HEAD_SLOT:(hh + 1) * HEAD_SLOT]
            s = lax.dot_general(q, kblk, (((1,), (1,)), ((), ())), preferred_element_type=F32)
            if static_max:
                new = acc_ref[hh] + _dot(jnp.exp2(s).astype(BF16), v_ones[hh])
            else:
                m_old = m_ref[hh]
                m_new = jnp.maximum(m_old, jnp.max(s, axis=-1, keepdims=True))
                p = jnp.exp2(s - jnp.concatenate([m_new] * (tk // LANES), axis=1))
                new = jnp.exp2(m_old - m_new) * acc_ref[hh] + _dot(p.astype(BF16), v_ones[hh])
                m_ref[hh] = m_new
            acc_ref[hh] = new
        return _zero_after(new)

    def run(static_max):
        after = jnp.zeros((SUBLANES, LANES), F32)
        for j in range(nkv):
            after = block(j, after, static_max)

    use_static = static_ref[0] != 0
    pl.when(use_static)(functools.partial(run, True))
    pl.when(jnp.logical_not(use_static))(functools.partial(run, False))
    for kk in range(s_ref.shape[0]):
        s_ref[kk] = s_ref[kk] * ops_ref[4, 0, pl.ds(kk, 1), :]
    a0, a1 = acc_ref[0], acc_ref[1]
    o = jnp.where(_lane(a0.shape) < MLA_V, a0 / pltpu.roll(a0, MLA_V, 1), a1 / pltpu.roll(a1, MLA_V, 1))
    o_ref[...] = o.astype(BF16)


def _score_bound(gain_q, gain_k):
    def norm2(g):
        return MLA_NOPE * jnp.max(jnp.square(g[:MLA_NOPE])) + MLA_ROPE * jnp.max(jnp.square(g[MLA_NOPE:]))
    q_scale = LOG2E * (MLA_NOPE + MLA_ROPE) ** -0.5
    return 1.02 * q_scale * jnp.sqrt(norm2(gain_q) * norm2(gain_k))


STATIC_SOFTMAX_MAX_BOUND = 50.0


def _attn_scan(q, k, v, ops_k, op_v, bound, *, tq, tk, n_q):
    b, s, qw = q.shape[0], k.shape[1], q.shape[2]
    pairs = qw // (2 * HEAD_SLOT)
    nq_t, nkv = n_q // tq, s // tk
    n_k, nv = ops_k[0].shape[1], op_v.shape[1]
    assert op_v.shape[0] == b * pairs * nq_t * nkv and op_v.shape[2] == LANES

    def step_idx(bb, hp, i):
        return ((bb * pairs + hp) * nq_t + i, 0, 0)

    kspec = pl.BlockSpec((nkv, n_k, LANES), step_idx)
    vspec = pl.BlockSpec((nkv, nv, LANES), step_idx)

    use_static = (bound <= STATIC_SOFTMAX_MAX_BOUND).astype(jnp.int32).reshape(1)
    return pl.pallas_call(
        functools.partial(_attn_scan_kernel, tk=tk),
        grid=(b, pairs, nq_t),
        in_specs=[pl.BlockSpec(memory_space=pltpu.SMEM),
                  pl.BlockSpec((None, tq, 2 * HEAD_SLOT), lambda bb, hp, i: (bb, i, hp)),
                  pl.BlockSpec((None, s, 2 * HEAD_SLOT), lambda bb, hp, i: (bb, 0, hp)),
                  pl.BlockSpec((None, s, 2 * MLA_V), lambda bb, hp, i: (bb, 0, hp))]
        + [kspec] * len(ops_k) + [vspec],
        out_specs=[pl.BlockSpec((None, tq, 2 * MLA_V), lambda bb, hp, i: (bb, i, hp)), vspec],
        out_shape=[jax.ShapeDtypeStruct((b, n_q, pairs * 2 * MLA_V), BF16), jax.ShapeDtypeStruct(op_v.shape, F32)],
        scratch_shapes=[pltpu.VMEM((n_k, nv, LANES), F32), pltpu.VMEM((2, tq, LANES), F32),
                        pltpu.VMEM((2, tq, 2 * MLA_V), F32), pltpu.VMEM((5, nkv, n_k, LANES), F32)],
        compiler_params=_params(("arbitrary", "arbitrary", "arbitrary")),
        name="attn_scan",
    )(use_static, q, k, v, *ops_k, op_v)


def _scan_operands(fwd, bwd, n_steps):
    r, v, nkk, w_f, k_f, b_f = fwd
    r_b, v_b, nkk_b, w_b, k_b, b_b = bwd
    b, width, s = r.shape
    heads = width // RWKV_HEAD
    half = RWKV_HEAD // 2
    nlh = 2 * b * heads
    assert nlh <= LANES // 2

    def pad(x):
        return jnp.pad(x, ((0, n_steps - s), (0, 0), (0, 0), (0, LANES // 2 - nlh))).reshape(n_steps, -1, LANES)

    def k_half(x_f, x_b):
        z = jnp.stack([x_f, x_b]).reshape(2, b, heads, RWKV_HEAD, s)
        return z.transpose(4, 3, 0, 1, 2).reshape(s, RWKV_HEAD, nlh)

    def pair(lo, hi):
        return pad(jnp.stack([lo, hi], axis=2))

    lw = k_half(w_f, w_b)
    ops_k = [pair(k_half(nkk, nkk_b), k_half(b_f, b_b)), pair(k_half(k_f, k_b), k_half(r, r_b)), pair(lw, lw)]
    z = jnp.stack([v, v_b]).reshape(2, b, heads, 2, half, s)
    return ops_k, pad(z.transpose(5, 4, 3, 0, 1, 2).reshape(s, half, 2, nlh))


def _scan_result(y, b, s, width):
    heads = width // RWKV_HEAD
    half = RWKV_HEAD // 2
    y = y[:s].reshape(s, half, 2, LANES // 2)[..., :2 * b * heads]
    return y.reshape(s, half, 2, 2, b, heads).transpose(3, 4, 0, 5, 2, 1).reshape(2, b, s, width)


def _merge_kernel(yf_ref, yb_ref, bonus_ref, gg_ref, op_ref, omc_ref, oml_ref, gate_ref, xc_ref, xl_ref, g1_ref,
                  lnw_ref, lnb_ref, wbp_ref, wbm_ref, wbr_ref, wo_ref, o_ref, *, tl, off):
    y = yf_ref[...] + _rev_rows(yb_ref[...])
    mu = _seg64_sum(y) * (1.0 / RWKV_HEAD)
    yc = y - mu
    var = _seg64_sum(yc * yc) * (1.0 / RWKV_HEAD)
    yn = yc * lax.rsqrt(var + RWKV_GN_EPS) * lnw_ref[...] + lnb_ref[...]
    o_rw = ((yn + bonus_ref[...]) * gg_ref[...]).astype(BF16)
    x = tl.pick(off, xc_ref, xl_ref)
    d = x.shape[-1]
    m = (gate_ref[:, 0:d].astype(F32) * _dot(op_ref[...], wbp_ref[...])
         + gate_ref[:, d:2 * d].astype(F32) * _dot(tl.pick(off, omc_ref, oml_ref), wbm_ref[...])
         + gate_ref[:, 2 * d:3 * d].astype(F32) * _dot(o_rw, wbr_ref[...]))
    o_ref[...] = x + g1_ref[...] * _dot(m.astype(BF16), wo_ref[...])


def _merge(tl, off, y, bonus, gg, o_pool, o_mla_ctx, o_mla_lat, gate, x_ctx, x_lat, lat_off, mod,
           lnw, lnb, wbp, wbm, wbr, wo):
    b, d = x_ctx.shape[0], x_ctx.shape[-1]
    n_tiles = tl.ns - off
    out_spec = pl.BlockSpec((None, tl.tm, d), lambda bb, ss: (bb, ss, 0))
    rw = y.shape[-1]
    yf_spec = pl.BlockSpec((None, None, tl.tm, rw), lambda bb, ss: (0, bb, ss + off, 0))
    yb_spec = pl.BlockSpec((None, None, tl.tm, rw), lambda bb, ss: (1, bb, _rev_tile(tl, ss + off), 0))
    return pl.pallas_call(
        functools.partial(_merge_kernel, tl=tl, off=off),
        grid=tl.grid(off),
        in_specs=[yf_spec, yb_spec, tl.tok(rw, off), tl.tok(rw, off), tl.tok(o_pool.shape[-1], off)]
        + tl.tok_split(o_mla_lat.shape[-1], off, 0) + [tl.tok(gate.shape[-1], off)]
        + tl.tok_split(d, off, lat_off) + [tl.mod(2, off), tl.const(lnw.shape), tl.const(lnb.shape)]
        + [tl.const(w.shape) for w in (wbp, wbm, wbr, wo)],
        out_specs=out_spec,
        out_shape=jax.ShapeDtypeStruct((b, n_tiles * tl.tm, d), F32),
        compiler_params=_params(("parallel", "parallel")),
        name="merge",
    )(y, y, bonus, gg, o_pool, o_mla_ctx, o_mla_lat, gate, x_ctx, x_lat, mod, lnw, lnb, wbp, wbm, wbr, wo)


def _mlp_kernel(x_ref, g_ref, *rest, parts, tm):
    mods, (w1_ref, w2_ref, o_ref, h_ref, acc_ref) = rest[:3 * parts], rest[3 * parts:]
    j = pl.program_id(1)

    @pl.when(j == 0)
    def _():
        for p in range(parts):
            rows = slice(p * tm, (p + 1) * tm)
            h_ref[rows, :] = _norm_mod(x_ref[rows, :], g_ref[...], mods[3 * p][...], mods[3 * p + 1][...]).astype(BF16)
        acc_ref[...] = jnp.zeros_like(acc_ref)

    a = jnp.maximum(_dot(h_ref[...], w1_ref[...]), 0.0)
    acc_ref[...] += _dot((a * a).astype(BF16), w2_ref[...])

    @pl.when(j == pl.num_programs(1) - 1)
    def _():
        for p in range(parts):
            rows = slice(p * tm, (p + 1) * tm)
            o_ref[rows, :] = x_ref[rows, :] + mods[3 * p + 2][...] * acc_ref[rows, :]


def _mlp(tl, x1, mod_off, g, mod, w1, w2):
    b, s1, d = x1.shape
    dff = w1.shape[1]
    tf = 1024 if dff % 1024 == 0 else dff
    tm = tl.tm
    per_batch = s1 // tm
    parts = next(p for p in (4, 2, 1) if (b * per_batch) % p == 0)
    nct, ctx_row = tl.nct, tl.ctx_row

    def mod_spec(j, p):
        def idx(ii, jj):
            sub = ii * parts + p
            bb, ss = sub // per_batch, sub % per_batch
            return (jnp.where(ss + mod_off < nct, ctx_row, bb), j, 0, 0)
        return pl.BlockSpec((None, None, 1, d), idx)

    mod_specs = [mod_spec(j, p) for p in range(parts) for j in (3, 4, 5)]
    x_spec = pl.BlockSpec((parts * tm, d), lambda ii, jj: (ii, 0))
    out = pl.pallas_call(
        functools.partial(_mlp_kernel, parts=parts, tm=tm),
        grid=(b * per_batch // parts, dff // tf),
        in_specs=[x_spec, pl.BlockSpec((1, d), lambda ii, jj: (0, 0))] + mod_specs
        + [pl.BlockSpec((d, tf), lambda ii, jj: (0, jj)), pl.BlockSpec((tf, d), lambda ii, jj: (jj, 0))],
        out_specs=x_spec,
        out_shape=jax.ShapeDtypeStruct((b * s1, d), F32),
        scratch_shapes=[pltpu.VMEM((parts * tm, d), BF16), pltpu.VMEM((parts * tm, d), F32)],
        compiler_params=_params(("parallel", "arbitrary")),
        name="mlp",
    )(x1.reshape(b * s1, d), g, *([mod] * (3 * parts)), w1, w2)
    return out.reshape(b, s1, d)


def _rope_tables(lc, l):
    rows = l // GRID_W
    row = jnp.repeat(jnp.arange(rows), GRID_W).astype(F32)
    col = jnp.tile(jnp.arange(GRID_W), rows).astype(F32)
    n_freq = MLA_ROPE // 4
    inv_freq = jnp.power(ROPE_BASE, -jnp.arange(n_freq, dtype=F32) / n_freq)
    ang = jnp.concatenate([row[:, None] * inv_freq, col[:, None] * inv_freq], axis=-1)
    cos = jnp.concatenate([jnp.ones((lc, MLA_ROPE // 2), F32), jnp.cos(ang)], axis=0)
    sin = jnp.concatenate([jnp.zeros((lc, MLA_ROPE // 2), F32), jnp.sin(ang)], axis=0)
    s = lc + l
    pad = jnp.zeros((s, LANES - MLA_NOPE - MLA_ROPE), F32)
    z16 = jnp.zeros((s, MLA_ROPE // 2), F32)
    zn = jnp.zeros((s, MLA_NOPE), F32)
    tc = jnp.concatenate([jnp.ones((s, MLA_NOPE), F32), cos, cos, pad], axis=1)
    ts1 = jnp.concatenate([zn, -sin, z16, pad], axis=1)
    ts2 = jnp.concatenate([zn, z16, sin, pad], axis=1)
    return tc, ts1, ts2


def _slot_cols(w, per_head):
    k = w.shape[0]
    w = w.reshape(k, MLA_HEADS, per_head)
    return jnp.pad(w, ((0, 0), (0, 0), (0, HEAD_SLOT - per_head))).reshape(k, MLA_HEADS * HEAD_SLOT)


def _block_diag(blocks):
    n = len(blocks)
    rows = []
    for i, blk in enumerate(blocks):
        rows.append(jnp.concatenate(
            [blk if j == i else jnp.zeros((blk.shape[0], blocks[j].shape[1]), blk.dtype) for j in range(n)], axis=1))
    return jnp.concatenate(rows, axis=0)


def _row(x):
    return x.reshape(1, -1).astype(F32)


def kernel(x, c, ctx, c_ctx, norm1_g, norm2_g, w_ada, b_ada, w_in, pool_w, pool_scale, mla_q_norm, mla_w_uq, mla_kv_norm, mla_w_ukv, qk_gain_q, qk_gain_k, rwkv_mu, rwkv_w0, rwkv_w2, rwkv_a0, rwkv_a2, rwkv_ka, rwkv_kk, rwkv_rk, rwkv_g2, rwkv_ln_w, rwkv_ln_b, w_br_pool, w_br_mla, w_br_rwkv, w_o, mlp_w1, mlp_w2):
    b, l, d = x.shape
    lc = ctx.shape[1]
    depth = w_in.shape[0]
    tl = _Tiles(b, lc, l, d)

    pool_width = pool_scale.shape[-1]
    q_rank = mla_q_norm.shape[-1]
    kv_rank = mla_kv_norm.shape[-1]
    rw_width = rwkv_kk.shape[-1]
    rw_in = rwkv_mu.shape[-1]

    rows = -(-(b + 1) // SUBLANES) * SUBLANES
    c_all = jnp.concatenate([c, c_ctx[None, :], jnp.zeros((rows - b - 1, d), F32)], axis=0)
    mod_all = _ada_mod(c_all, w_ada, b_ada).reshape(depth, rows, N_MOD, 1, d)
    tabs = _rope_tables(lc, l)
    seg = jnp.zeros((HEAD_SLOT, HEAD_SLOT), F32)
    seg = seg.at[:MLA_NOPE, :MLA_NOPE].set(1.0 / MLA_NOPE)
    seg = seg.at[MLA_NOPE:MLA_NOPE + MLA_ROPE, MLA_NOPE:MLA_NOPE + MLA_ROPE].set(1.0 / MLA_ROPE)
    seg = _block_diag([seg, seg]).astype(BF16)
    anti = jnp.eye(tl.tm, dtype=BF16)[::-1]

    tk = 256 if tl.s % 256 == 0 else LANES
    tq = 512
    while tq * tk > (MLA_HEADS // 2) * b * l or l % tq:
        tq //= 2
    assert tq >= 16

    x_ctx, x_lat, lat_off = ctx, x, 0
    out = None
    for i in range(depth):
        need_ctx = i < depth - 1
        off = 0 if need_ctx else tl.nct
        mod = mod_all[i]

        wi = w_in[i].astype(BF16)
        o0 = 0
        wp = wi[:, o0:o0 + pool_width]; o0 += pool_width
        wq = wi[:, o0:o0 + q_rank]; o0 += q_rank
        wkv = wi[:, o0:o0 + kv_rank + MLA_ROPE]; o0 += kv_rank + MLA_ROPE
        wkv = jnp.pad(wkv, ((0, 0), (0, LANES - MLA_ROPE)))
        wrw = wi[:, o0:o0 + rw_in]; o0 += rw_in
        wg = wi[:, o0:]

        zp, zq, zkv, zrw, gate = _in_proj(tl, x_ctx, x_lat, lat_off, _row(norm1_g[i]), mod, wp, wq, wkv, wrw, wg)

        pw_bd = _block_diag([pool_w[i, g] for g in range(pool_w.shape[1])]).astype(BF16)
        o_pool = _pool(tl, zp, pw_bd, _row(pool_scale[i]), off)

        wuq = _slot_cols(mla_w_uq[i], MLA_NOPE + MLA_ROPE).astype(BF16)
        wukv = mla_w_ukv[i].reshape(kv_rank, MLA_HEADS, MLA_NOPE + MLA_V)
        wuk = _slot_cols(wukv[:, :, :MLA_NOPE].reshape(kv_rank, -1), MLA_NOPE).astype(BF16)
        wuv = wukv[:, :, MLA_NOPE:].reshape(kv_rank, MLA_HEADS * MLA_V).astype(BF16)
        zpad = jnp.zeros((LANES - MLA_NOPE - MLA_ROPE,), F32)
        gq = _row(jnp.concatenate([qk_gain_q[i], zpad]))
        gk = _row(jnp.concatenate([qk_gain_k[i, :MLA_NOPE], jnp.zeros((LANES - MLA_NOPE,), F32)]))
        gkr = _row(jnp.concatenate([qk_gain_k[i, MLA_NOPE:], jnp.zeros((LANES - MLA_ROPE,), F32)]))
        bound = _score_bound(qk_gain_q[i], qk_gain_k[i])
        spare = (jnp.arange(LANES) == MLA_NOPE + MLA_ROPE).astype(F32)[None, :]
        smalls = (_row(mla_q_norm[i]), _row(mla_kv_norm[i]), gq, gk, gkr, seg, spare, -bound * spare)
        q, k, v = _qkv_up(tl, zq, zkv, tabs, smalls, wuq, wuk, wuv)

        w2cat = _block_diag([rwkv_w2[i, 0], rwkv_w2[i, 1]]).astype(BF16)
        a2cat = _block_diag([rwkv_a2[i, 0], rwkv_a2[i, 1]]).astype(BF16)
        rsmalls = (rwkv_mu[i].astype(F32), _row(rwkv_kk[i]), _row(rwkv_w0[i]), w2cat, _row(rwkv_a0[i]), a2cat,
                   _row(rwkv_ka[i]), _row(rwkv_rk[i]), rwkv_g2[i].astype(BF16), anti)
        r, vv, nkk, gg, bonus, wf, kf, bf, rb, vb, nkkb, wb, kb, bb = _rwkv_prep(tl, zrw, rsmalls)
        n_steps = b * (MLA_HEADS // 2) * (l // tq) * (tl.s // tk)
        ops_k, op_v = _scan_operands((r, vv, nkk, wf, kf, bf), (rb, vb, nkkb, wb, kb, bb), n_steps)
        o_mla_l, y = _attn_scan(q, k, v, ops_k, op_v, bound, tq=tq, tk=tk, n_q=l)
        y = _scan_result(y, b, tl.s, rw_width)
        if need_ctx:
            o_mla_c = _attention(q, k, v, tq=tl.tm, q_off_tiles=l // tl.tm, n_q_tiles=tl.nct, n_keys=lc)
        else:
            o_mla_c = o_mla_l

        x1 = _merge(tl, off, y, bonus, gg, o_pool, o_mla_c, o_mla_l, gate, x_ctx, x_lat, lat_off, mod,
                    _row(rwkv_ln_w[i]), _row(rwkv_ln_b[i]),
                    w_br_pool[i].astype(BF16), w_br_mla[i].astype(BF16), w_br_rwkv[i].astype(BF16),
                    w_o[i].astype(BF16))
        xc_next = _mlp(tl, x1, off, _row(norm2_g[i]), mod, mlp_w1[i].astype(BF16), mlp_w2[i].astype(BF16))
        if need_ctx:
            x_ctx, x_lat, lat_off = xc_next, xc_next, tl.nct
        else:
            out = xc_next
    return out
```

```python
import functools
import math

import jax
import jax.numpy as jnp
from jax import lax
from jax.experimental import pallas as pl
from jax.experimental.pallas import tpu as pltpu

F32 = jnp.float32
BF16 = jnp.bfloat16

NORM_EPS = 1e-6
RWKV_GN_EPS = 64e-5
GRID_W = 64
ROPE_BASE = 10000.0
POOL_HALF_WINDOWS = (1, 2, 4, 8)
N_MOD = 6
MLA_HEADS = 8
MLA_NOPE = 64
MLA_ROPE = 32
MLA_V = 64
RWKV_HEAD = 64
DECAY_RANK = 64
AAA_RANK = 64
GATE_RANK = 128

LANES = 128
SUBLANES = 8
HEAD_SLOT = LANES
HALO = SUBLANES
VMEM_LIMIT = 56 * 1024 * 1024

LOG2E = 1.4426950408889634


def _dot(a, b):
    return jnp.dot(a, b, preferred_element_type=F32)


def _sigmoid(x):
    return 1.0 / (1.0 + jnp.exp(-x))


def _rms(x, width):
    return lax.rsqrt(jnp.sum(x * x, axis=-1, keepdims=True) * (1.0 / width) + NORM_EPS)


def _norm_mod(x, g, shift, scale):
    return (x * _rms(x, x.shape[-1]) * g) * (1.0 + scale) + shift


def _lane(shape):
    return lax.broadcasted_iota(jnp.int32, shape, 1)


def _seg64_sum(x):
    cols = []
    for c in range(x.shape[1] // LANES):
        xc = x[:, c * LANES:(c + 1) * LANES]
        lo_m = _lane(xc.shape) < 64
        lo = jnp.sum(jnp.where(lo_m, xc, 0.0), axis=-1, keepdims=True)
        hi = jnp.sum(jnp.where(lo_m, 0.0, xc), axis=-1, keepdims=True)
        cols.append(jnp.where(lo_m, lo, hi))
    return cols[0] if len(cols) == 1 else jnp.concatenate(cols, axis=1)


def _params(sem, **flags):
    return pltpu.CompilerParams(dimension_semantics=sem, vmem_limit_bytes=VMEM_LIMIT, flags=flags or None)


class _Tiles:
    def __init__(self, batch, lc, l, d):
        self.batch, self.lc, self.l, self.d = batch, lc, l, d
        self.s = lc + l
        self.tm = 256 if (lc % 256 == 0 and l % 256 == 0) else 128
        assert lc % self.tm == 0 and l % self.tm == 0
        self.nct = lc // self.tm
        self.ns = self.s // self.tm
        self.ctx_row = batch

    def grid(self, off):
        return (self.batch, self.ns - off)

    def tok(self, width, off):
        return pl.BlockSpec((None, self.tm, width), lambda b, s: (b, s + off, 0))

    def tok_split(self, width, off, lat_off):
        nct = self.nct
        ctx = pl.BlockSpec((None, self.tm, width), lambda b, s: (b, jnp.minimum(s + off, nct - 1), 0))
        lat = pl.BlockSpec((None, self.tm, width), lambda b, s: (b, jnp.maximum(s + off - nct, 0) + lat_off, 0))
        return [ctx, lat]

    def pick(self, off, ctx_ref, lat_ref):
        return jnp.where(pl.program_id(1) + off < self.nct, ctx_ref[...], lat_ref[...])

    def halo_prev(self, width, off):
        r = self.tm // HALO
        return pl.BlockSpec((None, HALO, width), lambda b, s: (b, jnp.maximum((s + off) * r - 1, 0), 0))

    def halo_next(self, width, off):
        r = self.tm // HALO
        last = self.s // HALO - 1
        return pl.BlockSpec((None, HALO, width), lambda b, s: (b, jnp.minimum((s + off + 1) * r, last), 0))

    def mod(self, j, off):
        nct, ctx_row = self.nct, self.ctx_row
        return pl.BlockSpec((None, None, 1, self.d),
                            lambda b, s: (jnp.where(s + off < nct, ctx_row, b), j, 0, 0))

    def const(self, shape):
        nd = len(shape)
        return pl.BlockSpec(shape, lambda b, s: (0,) * nd)


def _ada_kernel(c_ref, w_ref, b_ref, o_ref):
    c = c_ref[...]
    s = (c * _sigmoid(c)).astype(BF16)
    o_ref[...] = _dot(s, w_ref[...].astype(BF16)) + b_ref[...]


def _ada_mod(c_all, w_ada, b_ada):
    depth, d, n = w_ada.shape
    rows = c_all.shape[0]
    tn = 1024
    return pl.pallas_call(
        _ada_kernel,
        grid=(depth, n // tn),
        in_specs=[pl.BlockSpec((rows, d), lambda i, j: (0, 0)),
                  pl.BlockSpec((None, d, tn), lambda i, j: (i, 0, j)),
                  pl.BlockSpec((None, 1, tn), lambda i, j: (i, 0, j))],
        out_specs=pl.BlockSpec((None, rows, tn), lambda i, j: (i, 0, j)),
        out_shape=jax.ShapeDtypeStruct((depth, rows, n), F32),
        compiler_params=_params(("parallel", "parallel")),
        name="ada_mod",
    )(c_all, w_ada, b_ada.reshape(depth, 1, n))


def _in_proj_kernel(xc_ref, xl_ref, g_ref, sh_ref, sc_ref, wp_ref, wq_ref, wkv_ref, wrw_ref, wg_ref,
                    zp_ref, zq_ref, zkv_ref, zrw_ref, gate_ref, *, tl):
    h = _norm_mod(tl.pick(0, xc_ref, xl_ref), g_ref[...], sh_ref[...], sc_ref[...]).astype(BF16)
    zp_ref[...] = _dot(h, wp_ref[...])
    zq_ref[...] = _dot(h, wq_ref[...])
    zkv_ref[...] = _dot(h, wkv_ref[...])
    zrw_ref[...] = _dot(h, wrw_ref[...])
    d = h.shape[-1]
    for c in range(wg_ref.shape[1] // d):
        gate_ref[:, c * d:(c + 1) * d] = _sigmoid(_dot(h, wg_ref[:, c * d:(c + 1) * d])).astype(BF16)


def _in_proj(tl, x_ctx, x_lat, lat_off, g, mod, wp, wq, wkv, wrw, wg):
    b, d = x_ctx.shape[0], x_ctx.shape[-1]
    widths = (wp.shape[1], wq.shape[1], wkv.shape[1], wrw.shape[1], wg.shape[1])
    dts = (F32, F32, F32, F32, BF16)
    return pl.pallas_call(
        functools.partial(_in_proj_kernel, tl=tl),
        grid=tl.grid(0),
        in_specs=tl.tok_split(d, 0, lat_off) + [tl.const((1, d)), tl.mod(0, 0), tl.mod(1, 0)]
        + [tl.const(w.shape) for w in (wp, wq, wkv, wrw, wg)],
        out_specs=[tl.tok(w, 0) for w in widths],
        out_shape=[jax.ShapeDtypeStruct((b, tl.s, w), dt) for w, dt in zip(widths, dts)],
        compiler_params=_params(("parallel", "parallel")),
        name="in_proj",
    )(x_ctx, x_lat, g, mod, mod, wp, wq, wkv, wrw, wg)


def _seq_flags(tl, off):
    s_abs = pl.program_id(1) + off
    is_ctx = s_abs < tl.nct
    first = jnp.logical_or(s_abs == 0, s_abs == tl.nct)
    last = jnp.logical_or(s_abs == tl.nct - 1, s_abs == tl.ns - 1)
    seq_len = jnp.where(is_ctx, tl.lc, tl.l)
    tile_in_seq = jnp.where(is_ctx, s_abs, s_abs - tl.nct)
    return first, last, seq_len, tile_in_seq


def _pool_kernel(u_ref, up_ref, un_ref, pw_ref, ps_ref, o_ref, *, tl, off):
    first, last, seq_len, tile_in_seq = _seq_flags(tl, off)
    tm = tl.tm
    u = u_ref[...]
    prev = jnp.where(first, 0.0, up_ref[...])
    nxt = jnp.where(last, 0.0, un_ref[...])
    e = jnp.concatenate([prev, u, nxt], axis=0)
    n = tm + 2 * HALO
    w2 = e + pltpu.roll(e, 1, 0)
    w4 = pltpu.roll(w2, n - 1, 0) + pltpu.roll(w2, 1, 0)
    w8 = pltpu.roll(w4, n - 2, 0) + pltpu.roll(w4, 2, 0)
    w16 = pltpu.roll(w8, n - 4, 0) + pltpu.roll(w8, 4, 0)
    sums = [w[HALO:HALO + tm] for w in (w2, w4, w8, w16)]
    width = u.shape[1]
    group = width // len(POOL_HALF_WINDOWS)
    lane = _lane((tm, width))
    pos = tile_in_seq * tm + lax.broadcasted_iota(jnp.int32, (tm, width), 0)
    total = sums[-1]
    half = jnp.full((tm, width), POOL_HALF_WINDOWS[-1], jnp.int32)
    for gi in range(len(POOL_HALF_WINDOWS) - 2, -1, -1):
        sel = lane < (gi + 1) * group
        total = jnp.where(sel, sums[gi], total)
        half = jnp.where(sel, POOL_HALF_WINDOWS[gi], half)
    cnt = jnp.minimum(pos + half, seq_len) - jnp.maximum(pos - half, 0)
    pooled = total / cnt.astype(F32) - u
    o_ref[...] = (_dot(pooled.astype(BF16), pw_ref[...]) * ps_ref[...]).astype(BF16)


def _pool(tl, zp, pw_bd, ps, off):
    b, s, w = zp.shape
    return pl.pallas_call(
        functools.partial(_pool_kernel, tl=tl, off=off),
        grid=tl.grid(off),
        in_specs=[tl.tok(w, off), tl.halo_prev(w, off), tl.halo_next(w, off),
                  tl.const(pw_bd.shape), tl.const(ps.shape)],
        out_specs=tl.tok(w, off),
        out_shape=jax.ShapeDtypeStruct((b, s, w), BF16),
        compiler_params=_params(("parallel", "parallel")),
        name="pool_mixer",
    )(zp, zp, zp, pw_bd, ps)


def _rope(x, tc, ts1, ts2):
    return x * tc + pltpu.roll(x, LANES - MLA_ROPE // 2, 1) * ts1 + pltpu.roll(x, MLA_ROPE // 2, 1) * ts2


def _head_inv_rms(x, seg):
    w = seg.shape[0]
    cols = [lax.rsqrt(_dot((xc * xc).astype(BF16), seg) + NORM_EPS)
            for xc in (x[:, c * w:(c + 1) * w] for c in range(x.shape[1] // w))]
    return jnp.concatenate(cols, axis=1)


def _qkv_kernel(zq_ref, zkv_ref, tc_ref, ts1_ref, ts2_ref, qng_ref, kvng_ref, gq_ref, gk_ref, gkr_ref, seg_ref,
                qpad_ref, kpad_ref, wuq_ref, wuk_ref, wuv_ref, q_ref, k_ref, v_ref):
    tc, ts1, ts2 = tc_ref[...], ts1_ref[...], ts2_ref[...]
    seg = seg_ref[...]

    zq = zq_ref[...]
    qc = (zq * _rms(zq, zq.shape[-1]) * qng_ref[...]).astype(BF16)
    q = _dot(qc, wuq_ref[...])
    q = q * _head_inv_rms(q, seg)
    q_scale = LOG2E * (MLA_NOPE + MLA_ROPE) ** -0.5
    for h in range(MLA_HEADS):
        qh = _rope(q[:, h * HEAD_SLOT:(h + 1) * HEAD_SLOT] * gq_ref[...], tc, ts1, ts2)
        q_ref[:, h * HEAD_SLOT:(h + 1) * HEAD_SLOT] = (qh * q_scale + qpad_ref[...]).astype(BF16)

    zkv = zkv_ref[...]
    kv_w = kvng_ref.shape[-1]
    kvc = zkv[:, :kv_w]
    kvn = (kvc * _rms(kvc, kv_w) * kvng_ref[...]).astype(BF16)
    kr = zkv[:, kv_w:kv_w + LANES]
    kr = kr * lax.rsqrt(jnp.sum(kr * kr, axis=-1, keepdims=True) * (1.0 / MLA_ROPE) + NORM_EPS) * gkr_ref[...]
    kr = _rope(pltpu.roll(kr, MLA_NOPE, 1), tc, ts1, ts2)
    kn = _dot(kvn, wuk_ref[...])
    kn = kn * _head_inv_rms(kn, seg)
    for h in range(MLA_HEADS):
        kh = kn[:, h * HEAD_SLOT:(h + 1) * HEAD_SLOT]
        k_ref[:, h * HEAD_SLOT:(h + 1) * HEAD_SLOT] = (kh * gk_ref[...] + kr + kpad_ref[...]).astype(BF16)
    v_ref[...] = _dot(kvn, wuv_ref[...]).astype(BF16)


def _qkv_up(tl, zq, zkv, tabs, smalls, wuq, wuk, wuv):
    b, s, _ = zq.shape
    tc, ts1, ts2 = tabs
    tab_spec = pl.BlockSpec((tl.tm, LANES), lambda bb, ss: (ss, 0))
    widths = (wuq.shape[1], wuk.shape[1], wuv.shape[1])
    nct, nlt = tl.nct, tl.ns - tl.nct
    q_spec = pl.BlockSpec((None, tl.tm, widths[0]), lambda bb, ss: (bb, jnp.where(ss < nct, nlt + ss, ss - nct), 0))
    return pl.pallas_call(
        _qkv_kernel,
        grid=tl.grid(0),
        in_specs=[tl.tok(zq.shape[-1], 0), tl.tok(zkv.shape[-1], 0), tab_spec, tab_spec, tab_spec]
        + [tl.const(a.shape) for a in smalls] + [tl.const(w.shape) for w in (wuq, wuk, wuv)],
        out_specs=[q_spec, tl.tok(widths[1], 0), tl.tok(widths[2], 0)],
        out_shape=[jax.ShapeDtypeStruct((b, s, w), BF16) for w in widths],
        compiler_params=_params(("parallel", "parallel")),
        name="qkv_up",
    )(zq, zkv, tc, ts1, ts2, *smalls, wuq, wuk, wuv)


def _attn_kernel(q_ref, k_ref, v_ref, o_ref):
    v = v_ref[...]
    outs = []
    for hh in range(2):
        q = q_ref[:, hh * HEAD_SLOT:(hh + 1) * HEAD_SLOT]
        k = k_ref[:, hh * HEAD_SLOT:(hh + 1) * HEAD_SLOT]
        s = lax.dot_general(q, k, (((1,), (1,)), ((), ())), preferred_element_type=F32)
        p = jnp.exp2(s - jnp.max(s, axis=-1, keepdims=True))
        denom = jnp.sum(p, axis=-1, keepdims=True)
        outs.append(_dot(p.astype(BF16), v) / denom)
    o_ref[...] = jnp.where(_lane(outs[0].shape) < MLA_V, outs[0], outs[1]).astype(BF16)


def _attention(q, k, v, *, tq, q_off_tiles, n_q_tiles, n_keys):
    b, _, qw = q.shape
    pairs = qw // (2 * HEAD_SLOT)
    return pl.pallas_call(
        _attn_kernel,
        grid=(b, pairs, n_q_tiles),
        in_specs=[pl.BlockSpec((None, tq, 2 * HEAD_SLOT), lambda bb, hp, i: (bb, i + q_off_tiles, hp)),
                  pl.BlockSpec((None, n_keys, 2 * HEAD_SLOT), lambda bb, hp, i: (bb, 0, hp)),
                  pl.BlockSpec((None, n_keys, 2 * MLA_V), lambda bb, hp, i: (bb, 0, hp))],
        out_specs=pl.BlockSpec((None, tq, 2 * MLA_V), lambda bb, hp, i: (bb, i, hp)),
        out_shape=jax.ShapeDtypeStruct((b, n_q_tiles * tq, pairs * 2 * MLA_V), BF16),
        compiler_params=_params(("parallel", "parallel", "arbitrary")),
        name="attention",
    )(q, k, v)


def _rev_rows(x):
    n = x.shape[0]
    row = lax.broadcasted_iota(jnp.int32, x.shape, 0)
    for sh in (1, 2, 4):
        x = jnp.where((row & sh) == 0, pltpu.roll(x, n - sh, 0), pltpu.roll(x, sh, 0))
    groups = [x[g * SUBLANES:(g + 1) * SUBLANES] for g in range(n // SUBLANES)]
    return jnp.concatenate(groups[::-1], axis=0)


def _rev_tile(tl, s_abs):
    return jnp.where(s_abs < tl.nct, tl.nct - 1 - s_abs, tl.ns - 1 + tl.nct - s_abs)


def _rwkv_prep_kernel(z_ref, zp_ref, zn_ref, mu_ref, kkw_ref, w0_ref, w2_ref, a0_ref, a2_ref, ka_ref,
                      rk_ref, g2_ref, anti_ref,
                      r_o, v_o, nkk_o, gg_o, bonus_o, wf_o, kf_o, bf_o, rb_o, vb_o, nkkb_o, wb_o, kb_o, bb_o,
                      *, tl, off):
    first, last, _, _ = _seq_flags(tl, off)
    tm = tl.tm
    z = z_ref[...]
    row = lax.broadcasted_iota(jnp.int32, z.shape, 0)
    prev_row = jnp.where(first, 0.0, zp_ref[HALO - 1:HALO, :])
    next_row = jnp.where(last, 0.0, zn_ref[0:1, :])
    z_prev = jnp.where(row == 0, prev_row, pltpu.roll(z, 1, 0))
    z_next = jnp.where(row == tm - 1, next_row, pltpu.roll(z, tm - 1, 0))
    zs = z + mu_ref[0:1, :] * (z_prev - z) + mu_ref[1:2, :] * (z_next - z)

    w = kkw_ref.shape[-1]
    r, k, v = zs[:, 0:w], zs[:, w:2 * w], zs[:, 2 * w:3 * w]
    o = 3 * w
    wd = zs[:, o:o + 2 * DECAY_RANK]
    ad = zs[:, o + 2 * DECAY_RANK:o + 2 * DECAY_RANK + 2 * AAA_RANK]
    gd = zs[:, o + 2 * DECAY_RANK + 2 * AAA_RANK:]

    kk = k * kkw_ref[...]
    kk = kk * lax.rsqrt(jnp.maximum(_seg64_sum(kk * kk), 1e-24))
    u = w0_ref[...] + _dot(jnp.tanh(wd).astype(BF16), w2_ref[...])
    log_decay = -math.exp(-0.5) * _sigmoid(u)
    a = _sigmoid(a0_ref[...] + _dot(ad.astype(BF16), a2_ref[...]))
    ka = ka_ref[...]
    k_sum = jnp.zeros_like(k)
    anti = anti_ref[...]

    def put16(o_ref, t, backward):
        t = t.T.astype(BF16)
        o_ref[...] = _dot(t, anti).astype(BF16) if backward else t

    for d, (w_o, k_o, b_o) in enumerate(((wf_o, kf_o, bf_o), (wb_o, kb_o, bb_o))):
        a_d = a[:, d * w:(d + 1) * w]
        k_d = k * (1.0 + (a_d - 1.0) * ka[:, d * w:(d + 1) * w])
        put16(w_o, log_decay[:, d * w:(d + 1) * w], d)
        put16(k_o, k_d, d)
        put16(b_o, kk * a_d, d)
        k_sum = k_sum + k_d
    put16(r_o, r, 0)
    put16(rb_o, r, 1)
    put16(nkk_o, -kk, 0)
    put16(nkkb_o, -kk, 1)
    put16(v_o, v, 0)
    put16(vb_o, v, 1)
    gg_o[...] = _dot(_sigmoid(gd).astype(BF16), g2_ref[...])
    bonus_o[...] = _seg64_sum(r * (0.5 * k_sum) * rk_ref[...]) * v


def _rwkv_prep(tl, zrw, smalls):
    b, s, win = zrw.shape
    w = smalls[1].shape[-1]
    fwd = pl.BlockSpec((None, w, tl.tm), lambda bb, ss: (bb, 0, ss))
    bwd = pl.BlockSpec((None, w, tl.tm), lambda bb, ss: (bb, 0, _rev_tile(tl, ss)))
    tok = tl.tok(w, 0)

    def chan(dt):
        return jax.ShapeDtypeStruct((b, w, s), dt)

    tokf = jax.ShapeDtypeStruct((b, s, w), F32)
    return pl.pallas_call(
        functools.partial(_rwkv_prep_kernel, tl=tl, off=0),
        grid=tl.grid(0),
        in_specs=[tl.tok(win, 0), tl.halo_prev(win, 0), tl.halo_next(win, 0)]
        + [tl.const(a.shape) for a in smalls],
        out_specs=[fwd, fwd, fwd, tok, tok, fwd, fwd, fwd] + [bwd] * 6,
        out_shape=[chan(BF16), chan(BF16), chan(BF16), tokf, tokf] + [chan(BF16)] * 9,
        compiler_params=_params(("parallel", "parallel")),
        name="rwkv_prep",
    )(zrw, zrw, zrw, *smalls)


def _zero_after(x):
    bits = pltpu.bitcast(x[:SUBLANES, :LANES].astype(F32), jnp.uint32)
    return pltpu.bitcast((bits >> 16) >> 16, F32)


def _scan_step(t, a_ref, b_ref, k_ref, r_ref, v_ref, y_ref, s_ref, after):
    n_k = s_ref.shape[0]
    v = v_ref[t].astype(F32) + jnp.concatenate([after] * (v_ref.shape[1] // SUBLANES), axis=0)
    acc = [jnp.zeros_like(v), jnp.zeros_like(v)]
    for kk in range(n_k):
        acc[kk % 2] = acc[kk % 2] + s_ref[kk] * a_ref[t, pl.ds(kk, 1), :]
    sa = acc[0] + acc[1]
    yacc = [jnp.zeros_like(v), jnp.zeros_like(v)]
    for kk in range(n_k):
        sn = s_ref[kk] + sa * b_ref[t, pl.ds(kk, 1), :] + v * k_ref[t, pl.ds(kk, 1), :]
        s_ref[kk] = sn
        yacc[kk % 2] = yacc[kk % 2] + sn * r_ref[t, pl.ds(kk, 1), :]
    y_ref[t] = yacc[0] + yacc[1]


def _attn_scan_kernel(static_ref, q_ref, k_ref, v_ref, ab_ref, kr_ref, lw_ref, vv_ref, o_ref, y_ref,
                      s_ref, m_ref, acc_ref, ops_ref, *, tk):
    first = jnp.logical_and(pl.program_id(0) == 0, jnp.logical_and(pl.program_id(1) == 0, pl.program_id(2) == 0))

    @pl.when(first)
    def _():
        s_ref[...] = jnp.zeros_like(s_ref)

    def unpack_pair(ref):
        x = ref[...].astype(F32)
        swapped = pltpu.roll(x.reshape(-1, LANES), LANES // 2, 1).reshape(x.shape)
        lo = lax.broadcasted_iota(jnp.int32, x.shape, 2) < LANES // 2
        return jnp.where(lo, x, swapped), jnp.where(lo, swapped, x)

    nkv = vv_ref.shape[0]
    a, b = unpack_pair(ab_ref)
    kd, r = unpack_pair(kr_ref)
    lw = jnp.concatenate(unpack_pair(lw_ref), axis=1)
    log_w = jnp.zeros(lw.shape[1:], F32)
    for t in range(nkv):
        ops_ref[0, t] = a[t] * jnp.exp(log_w)
        log_w = log_w + lw[t]
        inv_w = jnp.exp(-log_w)
        ops_ref[1, t] = b[t] * inv_w
        ops_ref[2, t] = kd[t] * inv_w
        ops_ref[3, t] = r[t] * jnp.exp(log_w)
    ops_ref[4, 0] = jnp.exp(log_w)
    a_ref, b_ref, kk_ref, r_ref = (ops_ref.at[i] for i in range(4))

    m_ref[...] = jnp.full_like(m_ref, -1e30)
    acc_ref[...] = jnp.zeros_like(acc_ref)
    lo_half = _lane((tk, 2 * MLA_V)) < MLA_V

    def block(j, after, static_max):
        _scan_step(j, a_ref, b_ref, kk_ref, r_ref, vv_ref, y_ref, s_ref, after)
        start = j * tk
        vblk = v_ref[pl.ds(start, tk), :]
        v_ones = (jnp.where(lo_half, vblk, 1.0), jnp.where(lo_half, 1.0, vblk))
        for hh in range(2):
            q = q_ref[:, hh * HEAD_SLOT:(hh + 1) * HEAD_SLOT]
            kblk = k_ref[pl.ds(start, tk), hh * HEAD_SLOT:(hh + 1) * HEAD_SLOT]
            s = lax.dot_general(q, kblk, (((1,), (1,)), ((), ())), preferred_element_type=F32)
            if static_max:
                new = acc_ref[hh] + _dot(jnp.exp2(s).astype(BF16), v_ones[hh])
            else:
                m_old = m_ref[hh]
                m_new = jnp.maximum(m_old, jnp.max(s, axis=-1, keepdims=True))
                p = jnp.exp2(s - jnp.concatenate([m_new] * (tk // LANES), axis=1))
                new = jnp.exp2(m_old - m_new) * acc_ref[hh] + _dot(p.astype(BF16), v_ones[hh])
                m_ref[hh] = m_new
            acc_ref[hh] = new
        return _zero_after(new)

    def run(static_max):
        after = jnp.zeros((SUBLANES, LANES), F32)
        for j in range(nkv):
            after = block(j, after, static_max)

    use_static = static_ref[0] != 0
    pl.when(use_static)(functools.partial(run, True))
    pl.when(jnp.logical_not(use_static))(functools.partial(run, False))
    for kk in range(s_ref.shape[0]):
        s_ref[kk] = s_ref[kk] * ops_ref[4, 0, pl.ds(kk, 1), :]
    a0, a1 = acc_ref[0], acc_ref[1]
    o = jnp.where(_lane(a0.shape) < MLA_V, a0 / pltpu.roll(a0, MLA_V, 1), a1 / pltpu.roll(a1, MLA_V, 1))
    o_ref[...] = o.astype(BF16)


def _score_bound(gain_q, gain_k):
    def norm2(g):
        return MLA_NOPE * jnp.max(jnp.square(g[:MLA_NOPE])) + MLA_ROPE * jnp.max(jnp.square(g[MLA_NOPE:]))
    q_scale = LOG2E * (MLA_NOPE + MLA_ROPE) ** -0.5
    return 1.02 * q_scale * jnp.sqrt(norm2(gain_q) * norm2(gain_k))


STATIC_SOFTMAX_MAX_BOUND = 50.0


def _attn_scan(q, k, v, ops_k, op_v, bound, *, tq, tk, n_q):
    b, s, qw = q.shape[0], k.shape[1], q.shape[2]
    pairs = qw // (2 * HEAD_SLOT)
    nq_t, nkv = n_q // tq, s // tk
    n_k, nv = ops_k[0].shape[1], op_v.shape[1]
    assert op_v.shape[0] == b * pairs * nq_t * nkv and op_v.shape[2] == LANES

    def step_idx(bb, hp, i):
        return ((bb * pairs + hp) * nq_t + i, 0, 0)

    step_specs = [pl.BlockSpec((nkv, x.shape[1], LANES), step_idx) for x in (*ops_k, op_v)]
    vspec = step_specs[-1]

    use_static = (bound <= STATIC_SOFTMAX_MAX_BOUND).astype(jnp.int32).reshape(1)
    return pl.pallas_call(
        functools.partial(_attn_scan_kernel, tk=tk),
        grid=(b, pairs, nq_t),
        in_specs=[pl.BlockSpec(memory_space=pltpu.SMEM),
                  pl.BlockSpec((None, tq, 2 * HEAD_SLOT), lambda bb, hp, i: (bb, i, hp)),
                  pl.BlockSpec((None, s, 2 * HEAD_SLOT), lambda bb, hp, i: (bb, 0, hp)),
                  pl.BlockSpec((None, s, 2 * MLA_V), lambda bb, hp, i: (bb, 0, hp))] + step_specs,
        out_specs=[pl.BlockSpec((None, tq, 2 * MLA_V), lambda bb, hp, i: (bb, i, hp)), vspec],
        out_shape=[jax.ShapeDtypeStruct((b, n_q, pairs * 2 * MLA_V), BF16), jax.ShapeDtypeStruct(op_v.shape, F32)],
        scratch_shapes=[pltpu.VMEM((n_k, nv, LANES), F32), pltpu.VMEM((2, tq, LANES), F32),
                        pltpu.VMEM((2, tq, 2 * MLA_V), F32), pltpu.VMEM((5, nkv, n_k, LANES), F32)],
        compiler_params=_params(("arbitrary", "arbitrary", "arbitrary")),
        name="attn_scan",
    )(use_static, q, k, v, *ops_k, op_v)


def _scan_operands(fwd, bwd, n_steps):
    r, v, nkk, w_f, k_f, b_f = fwd
    r_b, v_b, nkk_b, w_b, k_b, b_b = bwd
    b, width, s = r.shape
    heads = width // RWKV_HEAD
    half = RWKV_HEAD // 2
    nlh = 2 * b * heads
    assert nlh <= LANES // 2

    def pad(x):
        return jnp.pad(x, ((0, n_steps - s), (0, 0), (0, 0), (0, LANES // 2 - nlh))).reshape(n_steps, -1, LANES)

    def k_half(x_f, x_b):
        z = jnp.stack([x_f, x_b]).reshape(2, b, heads, RWKV_HEAD, s)
        return z.transpose(4, 3, 0, 1, 2).reshape(s, RWKV_HEAD, nlh)

    def pair(lo, hi):
        return pad(jnp.stack([lo, hi], axis=2))

    lw = k_half(w_f, w_b)
    ops_k = [pair(k_half(nkk, nkk_b), k_half(b_f, b_b)), pair(k_half(k_f, k_b), k_half(r, r_b)),
             pair(lw[:, :RWKV_HEAD // 2], lw[:, RWKV_HEAD // 2:])]
    z = jnp.stack([v, v_b]).reshape(2, b, heads, 2, half, s)
    return ops_k, pad(z.transpose(5, 4, 3, 0, 1, 2).reshape(s, half, 2, nlh))


def _scan_result(y, b, s, width):
    heads = width // RWKV_HEAD
    half = RWKV_HEAD // 2
    y = y[:s].reshape(s, half, 2, LANES // 2)[..., :2 * b * heads]
    return y.reshape(s, half, 2, 2, b, heads).transpose(3, 4, 0, 5, 2, 1).reshape(2, b, s, width)


def _merge_kernel(yf_ref, yb_ref, bonus_ref, gg_ref, op_ref, omc_ref, oml_ref, gate_ref, xc_ref, xl_ref, g1_ref,
                  lnw_ref, lnb_ref, wbp_ref, wbm_ref, wbr_ref, wo_ref, o_ref, *, tl, off):
    y = yf_ref[...] + _rev_rows(yb_ref[...])
    mu = _seg64_sum(y) * (1.0 / RWKV_HEAD)
    yc = y - mu
    var = _seg64_sum(yc * yc) * (1.0 / RWKV_HEAD)
    yn = yc * lax.rsqrt(var + RWKV_GN_EPS) * lnw_ref[...] + lnb_ref[...]
    o_rw = ((yn + bonus_ref[...]) * gg_ref[...]).astype(BF16)
    x = tl.pick(off, xc_ref, xl_ref)
    d = x.shape[-1]
    m = (gate_ref[:, 0:d].astype(F32) * _dot(op_ref[...], wbp_ref[...])
         + gate_ref[:, d:2 * d].astype(F32) * _dot(tl.pick(off, omc_ref, oml_ref), wbm_ref[...])
         + gate_ref[:, 2 * d:3 * d].astype(F32) * _dot(o_rw, wbr_ref[...]))
    o_ref[...] = x + g1_ref[...] * _dot(m.astype(BF16), wo_ref[...])


def _merge(tl, off, y, bonus, gg, o_pool, o_mla_ctx, o_mla_lat, gate, x_ctx, x_lat, lat_off, mod,
           lnw, lnb, wbp, wbm, wbr, wo):
    b, d = x_ctx.shape[0], x_ctx.shape[-1]
    n_tiles = tl.ns - off
    out_spec = pl.BlockSpec((None, tl.tm, d), lambda bb, ss: (bb, ss, 0))
    rw = y.shape[-1]
    yf_spec = pl.BlockSpec((None, None, tl.tm, rw), lambda bb, ss: (0, bb, ss + off, 0))
    yb_spec = pl.BlockSpec((None, None, tl.tm, rw), lambda bb, ss: (1, bb, _rev_tile(tl, ss + off), 0))
    return pl.pallas_call(
        functools.partial(_merge_kernel, tl=tl, off=off),
        grid=tl.grid(off),
        in_specs=[yf_spec, yb_spec, tl.tok(rw, off), tl.tok(rw, off), tl.tok(o_pool.shape[-1], off)]
        + tl.tok_split(o_mla_lat.shape[-1], off, 0) + [tl.tok(gate.shape[-1], off)]
        + tl.tok_split(d, off, lat_off) + [tl.mod(2, off), tl.const(lnw.shape), tl.const(lnb.shape)]
        + [tl.const(w.shape) for w in (wbp, wbm, wbr, wo)],
        out_specs=out_spec,
        out_shape=jax.ShapeDtypeStruct((b, n_tiles * tl.tm, d), F32),
        compiler_params=_params(("parallel", "parallel")),
        name="merge",
    )(y, y, bonus, gg, o_pool, o_mla_ctx, o_mla_lat, gate, x_ctx, x_lat, mod, lnw, lnb, wbp, wbm, wbr, wo)


def _mlp_kernel(x_ref, g_ref, *rest, parts, tm):
    mods, (w1_ref, w2_ref, o_ref, h_ref, acc_ref) = rest[:3 * parts], rest[3 * parts:]
    j = pl.program_id(1)

    @pl.when(j == 0)
    def _():
        for p in range(parts):
            rows = slice(p * tm, (p + 1) * tm)
            h_ref[rows, :] = _norm_mod(x_ref[rows, :], g_ref[...], mods[3 * p][...], mods[3 * p + 1][...]).astype(BF16)
        acc_ref[...] = jnp.zeros_like(acc_ref)

    a = jnp.maximum(_dot(h_ref[...], w1_ref[...]), 0.0)
    acc_ref[...] += _dot((a * a).astype(BF16), w2_ref[...])

    @pl.when(j == pl.num_programs(1) - 1)
    def _():
        for p in range(parts):
            rows = slice(p * tm, (p + 1) * tm)
            o_ref[rows, :] = x_ref[rows, :] + mods[3 * p + 2][...] * acc_ref[rows, :]


def _mlp(tl, x1, mod_off, g, mod, w1, w2):
    b, s1, d = x1.shape
    dff = w1.shape[1]
    tf = 1024 if dff % 1024 == 0 else dff
    tm = tl.tm
    per_batch = s1 // tm
    parts = next(p for p in (4, 2, 1) if (b * per_batch) % p == 0)
    nct, ctx_row = tl.nct, tl.ctx_row

    def mod_spec(j, p):
        def idx(ii, jj):
            sub = ii * parts + p
            bb, ss = sub // per_batch, sub % per_batch
            return (jnp.where(ss + mod_off < nct, ctx_row, bb), j, 0, 0)
        return pl.BlockSpec((None, None, 1, d), idx)

    mod_specs = [mod_spec(j, p) for p in range(parts) for j in (3, 4, 5)]
    x_spec = pl.BlockSpec((parts * tm, d), lambda ii, jj: (ii, 0))
    out = pl.pallas_call(
        functools.partial(_mlp_kernel, parts=parts, tm=tm),
        grid=(b * per_batch // parts, dff // tf),
        in_specs=[x_spec, pl.BlockSpec((1, d), lambda ii, jj: (0, 0))] + mod_specs
        + [pl.BlockSpec((d, tf), lambda ii, jj: (0, jj)), pl.BlockSpec((tf, d), lambda ii, jj: (jj, 0))],
        out_specs=x_spec,
        out_shape=jax.ShapeDtypeStruct((b * s1, d), F32),
        scratch_shapes=[pltpu.VMEM((parts * tm, d), BF16), pltpu.VMEM((parts * tm, d), F32)],
        compiler_params=_params(("parallel", "arbitrary")),
        name="mlp",
    )(x1.reshape(b * s1, d), g, *([mod] * (3 * parts)), w1, w2)
    return out.reshape(b, s1, d)


def _rope_tables(lc, l):
    rows = l // GRID_W
    row = jnp.repeat(jnp.arange(rows), GRID_W).astype(F32)
    col = jnp.tile(jnp.arange(GRID_W), rows).astype(F32)
    n_freq = MLA_ROPE // 4
    inv_freq = jnp.power(ROPE_BASE, -jnp.arange(n_freq, dtype=F32) / n_freq)
    ang = jnp.concatenate([row[:, None] * inv_freq, col[:, None] * inv_freq], axis=-1)
    cos = jnp.concatenate([jnp.ones((lc, MLA_ROPE // 2), F32), jnp.cos(ang)], axis=0)
    sin = jnp.concatenate([jnp.zeros((lc, MLA_ROPE // 2), F32), jnp.sin(ang)], axis=0)
    s = lc + l
    pad = jnp.zeros((s, LANES - MLA_NOPE - MLA_ROPE), F32)
    z16 = jnp.zeros((s, MLA_ROPE // 2), F32)
    zn = jnp.zeros((s, MLA_NOPE), F32)
    tc = jnp.concatenate([jnp.ones((s, MLA_NOPE), F32), cos, cos, pad], axis=1)
    ts1 = jnp.concatenate([zn, -sin, z16, pad], axis=1)
    ts2 = jnp.concatenate([zn, z16, sin, pad], axis=1)
    return tc, ts1, ts2


def _slot_cols(w, per_head):
    k = w.shape[0]
    w = w.reshape(k, MLA_HEADS, per_head)
    return jnp.pad(w, ((0, 0), (0, 0), (0, HEAD_SLOT - per_head))).reshape(k, MLA_HEADS * HEAD_SLOT)


def _block_diag(blocks):
    n = len(blocks)
    rows = []
    for i, blk in enumerate(blocks):
        rows.append(jnp.concatenate(
            [blk if j == i else jnp.zeros((blk.shape[0], blocks[j].shape[1]), blk.dtype) for j in range(n)], axis=1))
    return jnp.concatenate(rows, axis=0)


def _row(x):
    return x.reshape(1, -1).astype(F32)


def kernel(x, c, ctx, c_ctx, norm1_g, norm2_g, w_ada, b_ada, w_in, pool_w, pool_scale, mla_q_norm, mla_w_uq, mla_kv_norm, mla_w_ukv, qk_gain_q, qk_gain_k, rwkv_mu, rwkv_w0, rwkv_w2, rwkv_a0, rwkv_a2, rwkv_ka, rwkv_kk, rwkv_rk, rwkv_g2, rwkv_ln_w, rwkv_ln_b, w_br_pool, w_br_mla, w_br_rwkv, w_o, mlp_w1, mlp_w2):
    b, l, d = x.shape
    lc = ctx.shape[1]
    depth = w_in.shape[0]
    tl = _Tiles(b, lc, l, d)

    pool_width = pool_scale.shape[-1]
    q_rank = mla_q_norm.shape[-1]
    kv_rank = mla_kv_norm.shape[-1]
    rw_width = rwkv_kk.shape[-1]
    rw_in = rwkv_mu.shape[-1]

    rows = -(-(b + 1) // SUBLANES) * SUBLANES
    c_all = jnp.concatenate([c, c_ctx[None, :], jnp.zeros((rows - b - 1, d), F32)], axis=0)
    mod_all = _ada_mod(c_all, w_ada, b_ada).reshape(depth, rows, N_MOD, 1, d)
    tabs = _rope_tables(lc, l)
    seg = jnp.zeros((HEAD_SLOT, HEAD_SLOT), F32)
    seg = seg.at[:MLA_NOPE, :MLA_NOPE].set(1.0 / MLA_NOPE)
    seg = seg.at[MLA_NOPE:MLA_NOPE + MLA_ROPE, MLA_NOPE:MLA_NOPE + MLA_ROPE].set(1.0 / MLA_ROPE)
    seg = _block_diag([seg, seg]).astype(BF16)
    anti = jnp.eye(tl.tm, dtype=BF16)[::-1]

    tk = 256 if tl.s % 256 == 0 else LANES
    tq = 512
    while tq * tk > (MLA_HEADS // 2) * b * l or l % tq:
        tq //= 2
    assert tq >= 16

    x_ctx, x_lat, lat_off = ctx, x, 0
    out = None
    for i in range(depth):
        need_ctx = i < depth - 1
        off = 0 if need_ctx else tl.nct
        mod = mod_all[i]

        wi = w_in[i].astype(BF16)
        o0 = 0
        wp = wi[:, o0:o0 + pool_width]; o0 += pool_width
        wq = wi[:, o0:o0 + q_rank]; o0 += q_rank
        wkv = wi[:, o0:o0 + kv_rank + MLA_ROPE]; o0 += kv_rank + MLA_ROPE
        wkv = jnp.pad(wkv, ((0, 0), (0, LANES - MLA_ROPE)))
        wrw = wi[:, o0:o0 + rw_in]; o0 += rw_in
        wg = wi[:, o0:]

        zp, zq, zkv, zrw, gate = _in_proj(tl, x_ctx, x_lat, lat_off, _row(norm1_g[i]), mod, wp, wq, wkv, wrw, wg)

        pw_bd = _block_diag([pool_w[i, g] for g in range(pool_w.shape[1])]).astype(BF16)
        o_pool = _pool(tl, zp, pw_bd, _row(pool_scale[i]), off)

        wuq = _slot_cols(mla_w_uq[i], MLA_NOPE + MLA_ROPE).astype(BF16)
        wukv = mla_w_ukv[i].reshape(kv_rank, MLA_HEADS, MLA_NOPE + MLA_V)
        wuk = _slot_cols(wukv[:, :, :MLA_NOPE].reshape(kv_rank, -1), MLA_NOPE).astype(BF16)
        wuv = wukv[:, :, MLA_NOPE:].reshape(kv_rank, MLA_HEADS * MLA_V).astype(BF16)
        zpad = jnp.zeros((LANES - MLA_NOPE - MLA_ROPE,), F32)
        gq = _row(jnp.concatenate([qk_gain_q[i], zpad]))
        gk = _row(jnp.concatenate([qk_gain_k[i, :MLA_NOPE], jnp.zeros((LANES - MLA_NOPE,), F32)]))
        gkr = _row(jnp.concatenate([qk_gain_k[i, MLA_NOPE:], jnp.zeros((LANES - MLA_ROPE,), F32)]))
        bound = _score_bound(qk_gain_q[i], qk_gain_k[i])
        spare = (jnp.arange(LANES) == MLA_NOPE + MLA_ROPE).astype(F32)[None, :]
        smalls = (_row(mla_q_norm[i]), _row(mla_kv_norm[i]), gq, gk, gkr, seg, spare, -bound * spare)
        q, k, v = _qkv_up(tl, zq, zkv, tabs, smalls, wuq, wuk, wuv)

        w2cat = _block_diag([rwkv_w2[i, 0], rwkv_w2[i, 1]]).astype(BF16)
        a2cat = _block_diag([rwkv_a2[i, 0], rwkv_a2[i, 1]]).astype(BF16)
        rsmalls = (rwkv_mu[i].astype(F32), _row(rwkv_kk[i]), _row(rwkv_w0[i]), w2cat, _row(rwkv_a0[i]), a2cat,
                   _row(rwkv_ka[i]), _row(rwkv_rk[i]), rwkv_g2[i].astype(BF16), anti)
        r, vv, nkk, gg, bonus, wf, kf, bf, rb, vb, nkkb, wb, kb, bb = _rwkv_prep(tl, zrw, rsmalls)
        n_steps = b * (MLA_HEADS // 2) * (l // tq) * (tl.s // tk)
        ops_k, op_v = _scan_operands((r, vv, nkk, wf, kf, bf), (rb, vb, nkkb, wb, kb, bb), n_steps)
        o_mla_l, y = _attn_scan(q, k, v, ops_k, op_v, bound, tq=tq, tk=tk, n_q=l)
        y = _scan_result(y, b, tl.s, rw_width)
        if need_ctx:
            o_mla_c = _attention(q, k, v, tq=tl.tm, q_off_tiles=l // tl.tm, n_q_tiles=tl.nct, n_keys=lc)
        else:
            o_mla_c = o_mla_l

        x1 = _merge(tl, off, y, bonus, gg, o_pool, o_mla_c, o_mla_l, gate, x_ctx, x_lat, lat_off, mod,
                    _row(rwkv_ln_w[i]), _row(rwkv_ln_b[i]),
                    w_br_pool[i].astype(BF16), w_br_mla[i].astype(BF16), w_br_rwkv[i].astype(BF16),
                    w_o[i].astype(BF16))
        xc_next = _mlp(tl, x1, off, _row(norm2_g[i]), mod, mlp_w1[i].astype(BF16), mlp_w2[i].astype(BF16))
        if need_ctx:
            x_ctx, x_lat, lat_off = xc_next, xc_next, tl.nct
        else:
            out = xc_next
    return out
```

```python
import functools
import math

import jax
import jax.numpy as jnp
from jax import lax
from jax.experimental import pallas as pl
from jax.experimental.pallas import tpu as pltpu

F32 = jnp.float32
BF16 = jnp.bfloat16

NORM_EPS = 1e-6
RWKV_GN_EPS = 64e-5
GRID_W = 64
ROPE_BASE = 10000.0
POOL_HALF_WINDOWS = (1, 2, 4, 8)
N_MOD = 6
MLA_HEADS = 8
MLA_NOPE = 64
MLA_ROPE = 32
MLA_V = 64
RWKV_HEAD = 64
DECAY_RANK = 64
AAA_RANK = 64
GATE_RANK = 128

LANES = 128
SUBLANES = 8
HEAD_SLOT = LANES
HALO = SUBLANES
VMEM_LIMIT = 56 * 1024 * 1024

LOG2E = 1.4426950408889634


def _dot(a, b):
    return jnp.dot(a, b, preferred_element_type=F32)


def _sigmoid(x):
    return 1.0 / (1.0 + jnp.exp(-x))


def _rms(x, width):
    return lax.rsqrt(jnp.sum(x * x, axis=-1, keepdims=True) * (1.0 / width) + NORM_EPS)


def _norm_mod(x, g, shift, scale):
    return (x * _rms(x, x.shape[-1]) * g) * (1.0 + scale) + shift


def _lane(shape):
    return lax.broadcasted_iota(jnp.int32, shape, 1)


def _seg64_sum(x):
    cols = []
    for c in range(x.shape[1] // LANES):
        xc = x[:, c * LANES:(c + 1) * LANES]
        lo_m = _lane(xc.shape) < 64
        lo = jnp.sum(jnp.where(lo_m, xc, 0.0), axis=-1, keepdims=True)
        hi = jnp.sum(jnp.where(lo_m, 0.0, xc), axis=-1, keepdims=True)
        cols.append(jnp.where(lo_m, lo, hi))
    return cols[0] if len(cols) == 1 else jnp.concatenate(cols, axis=1)


def _params(sem, **flags):
    return pltpu.CompilerParams(dimension_semantics=sem, vmem_limit_bytes=VMEM_LIMIT, flags=flags or None)


class _Tiles:
    def __init__(self, batch, lc, l, d):
        self.batch, self.lc, self.l, self.d = batch, lc, l, d
        self.s = lc + l
        self.tm = 256 if (lc % 256 == 0 and l % 256 == 0) else 128
        assert lc % self.tm == 0 and l % self.tm == 0
        self.nct = lc // self.tm
        self.ns = self.s // self.tm
        self.ctx_row = batch

    def grid(self, off):
        return (self.batch, self.ns - off)

    def tok(self, width, off):
        return pl.BlockSpec((None, self.tm, width), lambda b, s: (b, s + off, 0))

    def tok_split(self, width, off, lat_off):
        nct = self.nct
        ctx = pl.BlockSpec((None, self.tm, width), lambda b, s: (b, jnp.minimum(s + off, nct - 1), 0))
        lat = pl.BlockSpec((None, self.tm, width), lambda b, s: (b, jnp.maximum(s + off - nct, 0) + lat_off, 0))
        return [ctx, lat]

    def pick(self, off, ctx_ref, lat_ref):
        return jnp.where(pl.program_id(1) + off < self.nct, ctx_ref[...], lat_ref[...])

    def halo_prev(self, width, off):
        r = self.tm // HALO
        return pl.BlockSpec((None, HALO, width), lambda b, s: (b, jnp.maximum((s + off) * r - 1, 0), 0))

    def halo_next(self, width, off):
        r = self.tm // HALO
        last = self.s // HALO - 1
        return pl.BlockSpec((None, HALO, width), lambda b, s: (b, jnp.minimum((s + off + 1) * r, last), 0))

    def mod(self, j, off):
        nct, ctx_row = self.nct, self.ctx_row
        return pl.BlockSpec((None, None, 1, self.d),
                            lambda b, s: (jnp.where(s + off < nct, ctx_row, b), j, 0, 0))

    def const(self, shape):
        nd = len(shape)
        return pl.BlockSpec(shape, lambda b, s: (0,) * nd)


def _ada_kernel(c_ref, w_ref, b_ref, o_ref):
    c = c_ref[...]
    s = (c * _sigmoid(c)).astype(BF16)
    o_ref[...] = _dot(s, w_ref[...].astype(BF16)) + b_ref[...]


def _ada_mod(c_all, w_ada, b_ada):
    depth, d, n = w_ada.shape
    rows = c_all.shape[0]
    tn = 1024
    return pl.pallas_call(
        _ada_kernel,
        grid=(depth, n // tn),
        in_specs=[pl.BlockSpec((rows, d), lambda i, j: (0, 0)),
                  pl.BlockSpec((None, d, tn), lambda i, j: (i, 0, j)),
                  pl.BlockSpec((None, 1, tn), lambda i, j: (i, 0, j))],
        out_specs=pl.BlockSpec((None, rows, tn), lambda i, j: (i, 0, j)),
        out_shape=jax.ShapeDtypeStruct((depth, rows, n), F32),
        compiler_params=_params(("parallel", "parallel")),
        name="ada_mod",
    )(c_all, w_ada, b_ada.reshape(depth, 1, n))


def _in_proj_kernel(xc_ref, xl_ref, g_ref, sh_ref, sc_ref, wp_ref, wq_ref, wkv_ref, wrw_ref, wg_ref,
                    zp_ref, zq_ref, zkv_ref, zrw_ref, gate_ref, *, tl):
    h = _norm_mod(tl.pick(0, xc_ref, xl_ref), g_ref[...], sh_ref[...], sc_ref[...]).astype(BF16)
    zp_ref[...] = _dot(h, wp_ref[...])
    zq_ref[...] = _dot(h, wq_ref[...])
    zkv_ref[...] = _dot(h, wkv_ref[...])
    zrw_ref[...] = _dot(h, wrw_ref[...])
    d = h.shape[-1]
    for c in range(wg_ref.shape[1] // d):
        gate_ref[:, c * d:(c + 1) * d] = _sigmoid(_dot(h, wg_ref[:, c * d:(c + 1) * d])).astype(BF16)


def _in_proj(tl, x_ctx, x_lat, lat_off, g, mod, wp, wq, wkv, wrw, wg):
    b, d = x_ctx.shape[0], x_ctx.shape[-1]
    widths = (wp.shape[1], wq.shape[1], wkv.shape[1], wrw.shape[1], wg.shape[1])
    dts = (F32, F32, F32, F32, BF16)
    return pl.pallas_call(
        functools.partial(_in_proj_kernel, tl=tl),
        grid=tl.grid(0),
        in_specs=tl.tok_split(d, 0, lat_off) + [tl.const((1, d)), tl.mod(0, 0), tl.mod(1, 0)]
        + [tl.const(w.shape) for w in (wp, wq, wkv, wrw, wg)],
        out_specs=[tl.tok(w, 0) for w in widths],
        out_shape=[jax.ShapeDtypeStruct((b, tl.s, w), dt) for w, dt in zip(widths, dts)],
        compiler_params=_params(("parallel", "parallel")),
        name="in_proj",
    )(x_ctx, x_lat, g, mod, mod, wp, wq, wkv, wrw, wg)


def _seq_flags(tl, off):
    s_abs = pl.program_id(1) + off
    is_ctx = s_abs < tl.nct
    first = jnp.logical_or(s_abs == 0, s_abs == tl.nct)
    last = jnp.logical_or(s_abs == tl.nct - 1, s_abs == tl.ns - 1)
    seq_len = jnp.where(is_ctx, tl.lc, tl.l)
    tile_in_seq = jnp.where(is_ctx, s_abs, s_abs - tl.nct)
    return first, last, seq_len, tile_in_seq


def _pool_kernel(u_ref, up_ref, un_ref, pw_ref, ps_ref, o_ref, *, tl, off):
    first, last, seq_len, tile_in_seq = _seq_flags(tl, off)
    tm = tl.tm
    u = u_ref[...]
    prev = jnp.where(first, 0.0, up_ref[...])
    nxt = jnp.where(last, 0.0, un_ref[...])
    e = jnp.concatenate([prev, u, nxt], axis=0)
    n = tm + 2 * HALO
    w2 = e + pltpu.roll(e, 1, 0)
    w4 = pltpu.roll(w2, n - 1, 0) + pltpu.roll(w2, 1, 0)
    w8 = pltpu.roll(w4, n - 2, 0) + pltpu.roll(w4, 2, 0)
    w16 = pltpu.roll(w8, n - 4, 0) + pltpu.roll(w8, 4, 0)
    sums = [w[HALO:HALO + tm] for w in (w2, w4, w8, w16)]
    width = u.shape[1]
    group = width // len(POOL_HALF_WINDOWS)
    lane = _lane((tm, width))
    pos = tile_in_seq * tm + lax.broadcasted_iota(jnp.int32, (tm, width), 0)
    total = sums[-1]
    half = jnp.full((tm, width), POOL_HALF_WINDOWS[-1], jnp.int32)
    for gi in range(len(POOL_HALF_WINDOWS) - 2, -1, -1):
        sel = lane < (gi + 1) * group
        total = jnp.where(sel, sums[gi], total)
        half = jnp.where(sel, POOL_HALF_WINDOWS[gi], half)
    cnt = jnp.minimum(pos + half, seq_len) - jnp.maximum(pos - half, 0)
    pooled = total / cnt.astype(F32) - u
    o_ref[...] = (_dot(pooled.astype(BF16), pw_ref[...]) * ps_ref[...]).astype(BF16)


def _pool(tl, zp, pw_bd, ps, off):
    b, s, w = zp.shape
    return pl.pallas_call(
        functools.partial(_pool_kernel, tl=tl, off=off),
        grid=tl.grid(off),
        in_specs=[tl.tok(w, off), tl.halo_prev(w, off), tl.halo_next(w, off),
                  tl.const(pw_bd.shape), tl.const(ps.shape)],
        out_specs=tl.tok(w, off),
        out_shape=jax.ShapeDtypeStruct((b, s, w), BF16),
        compiler_params=_params(("parallel", "parallel")),
        name="pool_mixer",
    )(zp, zp, zp, pw_bd, ps)


def _rope(x, tc, ts1, ts2):
    return x * tc + pltpu.roll(x, LANES - MLA_ROPE // 2, 1) * ts1 + pltpu.roll(x, MLA_ROPE // 2, 1) * ts2


def _head_inv_rms(x, seg):
    w = seg.shape[0]
    cols = [lax.rsqrt(_dot((xc * xc).astype(BF16), seg) + NORM_EPS)
            for xc in (x[:, c * w:(c + 1) * w] for c in range(x.shape[1] // w))]
    return jnp.concatenate(cols, axis=1)


def _qkv_kernel(zq_ref, zkv_ref, tc_ref, ts1_ref, ts2_ref, qng_ref, kvng_ref, gq_ref, gk_ref, gkr_ref, seg_ref,
                qpad_ref, kpad_ref, wuq_ref, wuk_ref, wuv_ref, q_ref, k_ref, v_ref):
    tc, ts1, ts2 = tc_ref[...], ts1_ref[...], ts2_ref[...]
    seg = seg_ref[...]

    zq = zq_ref[...]
    qc = (zq * _rms(zq, zq.shape[-1]) * qng_ref[...]).astype(BF16)
    q = _dot(qc, wuq_ref[...])
    q = q * _head_inv_rms(q, seg)
    q_scale = LOG2E * (MLA_NOPE + MLA_ROPE) ** -0.5
    for h in range(MLA_HEADS):
        qh = _rope(q[:, h * HEAD_SLOT:(h + 1) * HEAD_SLOT] * gq_ref[...], tc, ts1, ts2)
        q_ref[:, h * HEAD_SLOT:(h + 1) * HEAD_SLOT] = (qh * q_scale + qpad_ref[...]).astype(BF16)

    zkv = zkv_ref[...]
    kv_w = kvng_ref.shape[-1]
    kvc = zkv[:, :kv_w]
    kvn = (kvc * _rms(kvc, kv_w) * kvng_ref[...]).astype(BF16)
    kr = zkv[:, kv_w:kv_w + LANES]
    kr = kr * lax.rsqrt(jnp.sum(kr * kr, axis=-1, keepdims=True) * (1.0 / MLA_ROPE) + NORM_EPS) * gkr_ref[...]
    kr = _rope(pltpu.roll(kr, MLA_NOPE, 1), tc, ts1, ts2)
    kn = _dot(kvn, wuk_ref[...])
    kn = kn * _head_inv_rms(kn, seg)
    for h in range(MLA_HEADS):
        kh = kn[:, h * HEAD_SLOT:(h + 1) * HEAD_SLOT]
        k_ref[:, h * HEAD_SLOT:(h + 1) * HEAD_SLOT] = (kh * gk_ref[...] + kr + kpad_ref[...]).astype(BF16)
    v_ref[...] = _dot(kvn, wuv_ref[...]).astype(BF16)


def _qkv_up(tl, zq, zkv, tabs, smalls, wuq, wuk, wuv):
    b, s, _ = zq.shape
    tc, ts1, ts2 = tabs
    tab_spec = pl.BlockSpec((tl.tm, LANES), lambda bb, ss: (ss, 0))
    widths = (wuq.shape[1], wuk.shape[1], wuv.shape[1])
    nct, nlt = tl.nct, tl.ns - tl.nct
    q_spec = pl.BlockSpec((None, tl.tm, widths[0]), lambda bb, ss: (bb, jnp.where(ss < nct, nlt + ss, ss - nct), 0))
    return pl.pallas_call(
        _qkv_kernel,
        grid=tl.grid(0),
        in_specs=[tl.tok(zq.shape[-1], 0), tl.tok(zkv.shape[-1], 0), tab_spec, tab_spec, tab_spec]
        + [tl.const(a.shape) for a in smalls] + [tl.const(w.shape) for w in (wuq, wuk, wuv)],
        out_specs=[q_spec, tl.tok(widths[1], 0), tl.tok(widths[2], 0)],
        out_shape=[jax.ShapeDtypeStruct((b, s, w), BF16) for w in widths],
        compiler_params=_params(("parallel", "parallel")),
        name="qkv_up",
    )(zq, zkv, tc, ts1, ts2, *smalls, wuq, wuk, wuv)


def _attn_kernel(q_ref, k_ref, v_ref, o_ref):
    v = v_ref[...]
    outs = []
    for hh in range(2):
        q = q_ref[:, hh * HEAD_SLOT:(hh + 1) * HEAD_SLOT]
        k = k_ref[:, hh * HEAD_SLOT:(hh + 1) * HEAD_SLOT]
        s = lax.dot_general(q, k, (((1,), (1,)), ((), ())), preferred_element_type=F32)
        p = jnp.exp2(s - jnp.max(s, axis=-1, keepdims=True))
        denom = jnp.sum(p, axis=-1, keepdims=True)
        outs.append(_dot(p.astype(BF16), v) / denom)
    o_ref[...] = jnp.where(_lane(outs[0].shape) < MLA_V, outs[0], outs[1]).astype(BF16)


def _attention(q, k, v, *, tq, q_off_tiles, n_q_tiles, n_keys):
    b, _, qw = q.shape
    pairs = qw // (2 * HEAD_SLOT)
    return pl.pallas_call(
        _attn_kernel,
        grid=(b, pairs, n_q_tiles),
        in_specs=[pl.BlockSpec((None, tq, 2 * HEAD_SLOT), lambda bb, hp, i: (bb, i + q_off_tiles, hp)),
                  pl.BlockSpec((None, n_keys, 2 * HEAD_SLOT), lambda bb, hp, i: (bb, 0, hp)),
                  pl.BlockSpec((None, n_keys, 2 * MLA_V), lambda bb, hp, i: (bb, 0, hp))],
        out_specs=pl.BlockSpec((None, tq, 2 * MLA_V), lambda bb, hp, i: (bb, i, hp)),
        out_shape=jax.ShapeDtypeStruct((b, n_q_tiles * tq, pairs * 2 * MLA_V), BF16),
        compiler_params=_params(("parallel", "parallel", "arbitrary")),
        name="attention",
    )(q, k, v)


def _rev_rows(x):
    n = x.shape[0]
    row = lax.broadcasted_iota(jnp.int32, x.shape, 0)
    for sh in (1, 2, 4):
        x = jnp.where((row & sh) == 0, pltpu.roll(x, n - sh, 0), pltpu.roll(x, sh, 0))
    groups = [x[g * SUBLANES:(g + 1) * SUBLANES] for g in range(n // SUBLANES)]
    return jnp.concatenate(groups[::-1], axis=0)


def _rev_tile(tl, s_abs):
    return jnp.where(s_abs < tl.nct, tl.nct - 1 - s_abs, tl.ns - 1 + tl.nct - s_abs)


def _rwkv_prep_kernel(z_ref, zp_ref, zn_ref, mu_ref, kkw_ref, w0_ref, w2_ref, a0_ref, a2_ref, ka_ref,
                      rk_ref, g2_ref, anti_ref,
                      r_o, v_o, nkk_o, gg_o, bonus_o, wf_o, kf_o, bf_o, rb_o, vb_o, nkkb_o, wb_o, kb_o, bb_o,
                      *, tl, off):
    first, last, _, _ = _seq_flags(tl, off)
    tm = tl.tm
    z = z_ref[...]
    row = lax.broadcasted_iota(jnp.int32, z.shape, 0)
    prev_row = jnp.where(first, 0.0, zp_ref[HALO - 1:HALO, :])
    next_row = jnp.where(last, 0.0, zn_ref[0:1, :])
    z_prev = jnp.where(row == 0, prev_row, pltpu.roll(z, 1, 0))
    z_next = jnp.where(row == tm - 1, next_row, pltpu.roll(z, tm - 1, 0))
    zs = z + mu_ref[0:1, :] * (z_prev - z) + mu_ref[1:2, :] * (z_next - z)

    w = kkw_ref.shape[-1]
    r, k, v = zs[:, 0:w], zs[:, w:2 * w], zs[:, 2 * w:3 * w]
    o = 3 * w
    wd = zs[:, o:o + 2 * DECAY_RANK]
    ad = zs[:, o + 2 * DECAY_RANK:o + 2 * DECAY_RANK + 2 * AAA_RANK]
    gd = zs[:, o + 2 * DECAY_RANK + 2 * AAA_RANK:]

    kk = k * kkw_ref[...]
    kk = kk * lax.rsqrt(jnp.maximum(_seg64_sum(kk * kk), 1e-24))
    u = w0_ref[...] + _dot(jnp.tanh(wd).astype(BF16), w2_ref[...])
    log_decay = -math.exp(-0.5) * _sigmoid(u)
    a = _sigmoid(a0_ref[...] + _dot(ad.astype(BF16), a2_ref[...]))
    ka = ka_ref[...]
    k_sum = jnp.zeros_like(k)
    anti = anti_ref[...]

    def put16(o_ref, t, backward):
        t = t.T.astype(BF16)
        o_ref[...] = _dot(t, anti).astype(BF16) if backward else t

    for d, (w_o, k_o, b_o) in enumerate(((wf_o, kf_o, bf_o), (wb_o, kb_o, bb_o))):
        a_d = a[:, d * w:(d + 1) * w]
        k_d = k * (1.0 + (a_d - 1.0) * ka[:, d * w:(d + 1) * w])
        put16(w_o, log_decay[:, d * w:(d + 1) * w], d)
        put16(k_o, k_d, d)
        put16(b_o, kk * a_d, d)
        k_sum = k_sum + k_d
    put16(r_o, r, 0)
    put16(rb_o, r, 1)
    put16(nkk_o, -kk, 0)
    put16(nkkb_o, -kk, 1)
    put16(v_o, v, 0)
    put16(vb_o, v, 1)
    gg_o[...] = _dot(_sigmoid(gd).astype(BF16), g2_ref[...])
    bonus_o[...] = _seg64_sum(r * (0.5 * k_sum) * rk_ref[...]) * v


def _rwkv_prep(tl, zrw, smalls):
    b, s, win = zrw.shape
    w = smalls[1].shape[-1]
    fwd = pl.BlockSpec((None, w, tl.tm), lambda bb, ss: (bb, 0, ss))
    bwd = pl.BlockSpec((None, w, tl.tm), lambda bb, ss: (bb, 0, _rev_tile(tl, ss)))
    tok = tl.tok(w, 0)

    def chan(dt):
        return jax.ShapeDtypeStruct((b, w, s), dt)

    tokf = jax.ShapeDtypeStruct((b, s, w), F32)
    return pl.pallas_call(
        functools.partial(_rwkv_prep_kernel, tl=tl, off=0),
        grid=tl.grid(0),
        in_specs=[tl.tok(win, 0), tl.halo_prev(win, 0), tl.halo_next(win, 0)]
        + [tl.const(a.shape) for a in smalls],
        out_specs=[fwd, fwd, fwd, tok, tok, fwd, fwd, fwd] + [bwd] * 6,
        out_shape=[chan(BF16), chan(BF16), chan(BF16), tokf, tokf] + [chan(BF16)] * 9,
        compiler_params=_params(("parallel", "parallel")),
        name="rwkv_prep",
    )(zrw, zrw, zrw, *smalls)


def _zero_after(x):
    bits = pltpu.bitcast(x[:SUBLANES, :LANES].astype(F32), jnp.uint32)
    return pltpu.bitcast((bits >> 16) >> 16, F32)


def _scan_step(t, a_ref, b_ref, k_ref, r_ref, v_ref, y_ref, s_ref, after):
    n_k = s_ref.shape[0]
    v = v_ref[t].astype(F32) + jnp.concatenate([after] * (v_ref.shape[1] // SUBLANES), axis=0)
    sa = s_ref[0] * a_ref[t, pl.ds(0, 1), :]
    for kk in range(1, n_k):
        sa = sa + s_ref[kk] * a_ref[t, pl.ds(kk, 1), :]
    y = None
    for kk in range(n_k):
        sn = s_ref[kk] + sa * b_ref[t, pl.ds(kk, 1), :] + v * k_ref[t, pl.ds(kk, 1), :]
        s_ref[kk] = sn
        yk = sn * r_ref[t, pl.ds(kk, 1), :]
        y = yk if y is None else y + yk
    y_ref[t] = y


def _attn_scan_kernel(static_ref, q_ref, k_ref, v_ref, ab_ref, kr_ref, lw_ref, vv_ref, o_ref, y_ref,
                      s_ref, m_ref, acc_ref, ops_ref, *, tk):
    first = jnp.logical_and(pl.program_id(0) == 0, jnp.logical_and(pl.program_id(1) == 0, pl.program_id(2) == 0))

    @pl.when(first)
    def _():
        s_ref[...] = jnp.zeros_like(s_ref)

    def unpack_pair(ref):
        x = ref[...].astype(F32)
        swapped = pltpu.roll(x.reshape(-1, LANES), LANES // 2, 1).reshape(x.shape)
        lo = lax.broadcasted_iota(jnp.int32, x.shape, 2) < LANES // 2
        return jnp.where(lo, x, swapped), jnp.where(lo, swapped, x)

    nkv = vv_ref.shape[0]
    a_ref, b_ref, kk_ref, r_ref = (ops_ref.at[i] for i in range(4))
    lo_half = _lane((tk, 2 * MLA_V)) < MLA_V

    def begin():
        a, b = unpack_pair(ab_ref)
        kd, r = unpack_pair(kr_ref)
        lw = jnp.concatenate(unpack_pair(lw_ref), axis=1)
        log_w = jnp.zeros(lw.shape[1:], F32)
        for t in range(nkv):
            ops_ref[0, t] = a[t] * jnp.exp(log_w)
            log_w = log_w + lw[t]
            inv_w = jnp.exp(-log_w)
            ops_ref[1, t] = b[t] * inv_w
            ops_ref[2, t] = kd[t] * inv_w
            ops_ref[3, t] = r[t] * jnp.exp(log_w)
        ops_ref[4, 0] = jnp.exp(log_w)
        m_ref[...] = jnp.full_like(m_ref, -1e30)
        acc_ref[...] = jnp.zeros_like(acc_ref)

    def finish():
        for kk in range(s_ref.shape[0]):
            s_ref[kk] = s_ref[kk] * ops_ref[4, 0, pl.ds(kk, 1), :]
        a0, a1 = acc_ref[0], acc_ref[1]
        o = jnp.where(_lane(a0.shape) < MLA_V, a0 / pltpu.roll(a0, MLA_V, 1), a1 / pltpu.roll(a1, MLA_V, 1))
        o_ref[...] = o.astype(BF16)

    def block(j, after, static_max):
        _scan_step(j, a_ref, b_ref, kk_ref, r_ref, vv_ref, y_ref, s_ref, after)
        start = j * tk
        vblk = v_ref[pl.ds(start, tk), :]
        v_ones = (jnp.where(lo_half, vblk, 1.0), jnp.where(lo_half, 1.0, vblk))
        for hh in range(2):
            q = q_ref[:, hh * HEAD_SLOT:(hh + 1) * HEAD_SLOT]
            kblk = k_ref[pl.ds(start, tk), hh * HEAD_SLOT:(hh + 1) * HEAD_SLOT]
            s = lax.dot_general(q, kblk, (((1,), (1,)), ((), ())), preferred_element_type=F32)
            if static_max:
                new = acc_ref[hh] + _dot(jnp.exp2(s).astype(BF16), v_ones[hh])
            else:
                m_old = m_ref[hh]
                m_new = jnp.maximum(m_old, jnp.max(s, axis=-1, keepdims=True))
                p = jnp.exp2(s - jnp.concatenate([m_new] * (tk // LANES), axis=1))
                new = jnp.exp2(m_old - m_new) * acc_ref[hh] + _dot(p.astype(BF16), v_ones[hh])
                m_ref[hh] = m_new
            acc_ref[hh] = new
        return _zero_after(new)

    def run(static_max):
        after = jnp.zeros((SUBLANES, LANES), F32)
        for j in range(nkv):
            after = block(j, after, static_max)

    begin()
    use_static = static_ref[0] != 0
    pl.when(use_static)(functools.partial(run, True))
    pl.when(jnp.logical_not(use_static))(functools.partial(run, False))
    finish()


def _score_bound(gain_q, gain_k):
    def norm2(g):
        return MLA_NOPE * jnp.max(jnp.square(g[:MLA_NOPE])) + MLA_ROPE * jnp.max(jnp.square(g[MLA_NOPE:]))
    q_scale = LOG2E * (MLA_NOPE + MLA_ROPE) ** -0.5
    return 1.02 * q_scale * jnp.sqrt(norm2(gain_q) * norm2(gain_k))


STATIC_SOFTMAX_MAX_BOUND = 50.0


def _attn_scan(q, k, v, ops_k, op_v, bound, *, tq, tk, n_q):
    b, s, qw = q.shape[0], k.shape[1], q.shape[2]
    pairs = qw // (2 * HEAD_SLOT)
    nq_t, nkv = n_q // tq, s // tk
    n_k, nv = ops_k[0].shape[1], op_v.shape[1]
    assert op_v.shape[0] == b * pairs * nq_t * nkv and op_v.shape[2] == LANES

    def step_idx(bb, hp, i):
        return ((bb * pairs + hp) * nq_t + i, 0, 0)

    step_specs = [pl.BlockSpec((nkv, x.shape[1], LANES), step_idx) for x in (*ops_k, op_v)]
    vspec = step_specs[-1]

    use_static = (bound <= STATIC_SOFTMAX_MAX_BOUND).astype(jnp.int32).reshape(1)
    return pl.pallas_call(
        functools.partial(_attn_scan_kernel, tk=tk),
        grid=(b, pairs, nq_t),
        in_specs=[pl.BlockSpec(memory_space=pltpu.SMEM),
                  pl.BlockSpec((None, tq, 2 * HEAD_SLOT), lambda bb, hp, i: (bb, i, hp)),
                  pl.BlockSpec((None, s, 2 * HEAD_SLOT), lambda bb, hp, i: (bb, 0, hp)),
                  pl.BlockSpec((None, s, 2 * MLA_V), lambda bb, hp, i: (bb, 0, hp))] + step_specs,
        out_specs=[pl.BlockSpec((None, tq, 2 * MLA_V), lambda bb, hp, i: (bb, i, hp)), vspec],
        out_shape=[jax.ShapeDtypeStruct((b, n_q, pairs * 2 * MLA_V), BF16), jax.ShapeDtypeStruct(op_v.shape, F32)],
        scratch_shapes=[pltpu.VMEM((n_k, nv, LANES), F32), pltpu.VMEM((2, tq, LANES), F32),
                        pltpu.VMEM((2, tq, 2 * MLA_V), F32), pltpu.VMEM((5, nkv, n_k, LANES), F32)],
        compiler_params=_params(("arbitrary", "arbitrary", "arbitrary")),
        name="attn_scan",
    )(use_static, q, k, v, *ops_k, op_v)


def _scan_operands(fwd, bwd, n_steps):
    r, v, nkk, w_f, k_f, b_f = fwd
    r_b, v_b, nkk_b, w_b, k_b, b_b = bwd
    b, width, s = r.shape
    heads = width // RWKV_HEAD
    half = RWKV_HEAD // 2
    nlh = 2 * b * heads
    assert nlh <= LANES // 2

    def pad(x):
        return jnp.pad(x, ((0, n_steps - s), (0, 0), (0, 0), (0, LANES // 2 - nlh))).reshape(n_steps, -1, LANES)

    def k_half(x_f, x_b):
        z = jnp.stack([x_f, x_b]).reshape(2, b, heads, RWKV_HEAD, s)
        return z.transpose(4, 3, 0, 1, 2).reshape(s, RWKV_HEAD, nlh)

    def pair(lo, hi):
        return pad(jnp.stack([lo, hi], axis=2))

    lw = k_half(w_f, w_b)
    ops_k = [pair(k_half(nkk, nkk_b), k_half(b_f, b_b)), pair(k_half(k_f, k_b), k_half(r, r_b)),
             pair(lw[:, :RWKV_HEAD // 2], lw[:, RWKV_HEAD // 2:])]
    z = jnp.stack([v, v_b]).reshape(2, b, heads, 2, half, s)
    return ops_k, pad(z.transpose(5, 4, 3, 0, 1, 2).reshape(s, half, 2, nlh))


def _scan_result(y, b, s, width):
    heads = width // RWKV_HEAD
    half = RWKV_HEAD // 2
    y = y[:s].reshape(s, half, 2, LANES // 2)[..., :2 * b * heads]
    return y.reshape(s, half, 2, 2, b, heads).transpose(3, 4, 0, 5, 2, 1).reshape(2, b, s, width)


def _merge_kernel(yf_ref, yb_ref, bonus_ref, gg_ref, op_ref, omc_ref, oml_ref, gate_ref, xc_ref, xl_ref, g1_ref,
                  lnw_ref, lnb_ref, wbp_ref, wbm_ref, wbr_ref, wo_ref, o_ref, *, tl, off):
    y = yf_ref[...] + _rev_rows(yb_ref[...])
    mu = _seg64_sum(y) * (1.0 / RWKV_HEAD)
    yc = y - mu
    var = _seg64_sum(yc * yc) * (1.0 / RWKV_HEAD)
    yn = yc * lax.rsqrt(var + RWKV_GN_EPS) * lnw_ref[...] + lnb_ref[...]
    o_rw = ((yn + bonus_ref[...]) * gg_ref[...]).astype(BF16)
    x = tl.pick(off, xc_ref, xl_ref)
    d = x.shape[-1]
    m = (gate_ref[:, 0:d].astype(F32) * _dot(op_ref[...], wbp_ref[...])
         + gate_ref[:, d:2 * d].astype(F32) * _dot(tl.pick(off, omc_ref, oml_ref), wbm_ref[...])
         + gate_ref[:, 2 * d:3 * d].astype(F32) * _dot(o_rw, wbr_ref[...]))
    o_ref[...] = x + g1_ref[...] * _dot(m.astype(BF16), wo_ref[...])


def _merge(tl, off, y, bonus, gg, o_pool, o_mla_ctx, o_mla_lat, gate, x_ctx, x_lat, lat_off, mod,
           lnw, lnb, wbp, wbm, wbr, wo):
    b, d = x_ctx.shape[0], x_ctx.shape[-1]
    n_tiles = tl.ns - off
    out_spec = pl.BlockSpec((None, tl.tm, d), lambda bb, ss: (bb, ss, 0))
    rw = y.shape[-1]
    yf_spec = pl.BlockSpec((None, None, tl.tm, rw), lambda bb, ss: (0, bb, ss + off, 0))
    yb_spec = pl.BlockSpec((None, None, tl.tm, rw), lambda bb, ss: (1, bb, _rev_tile(tl, ss + off), 0))
    return pl.pallas_call(
        functools.partial(_merge_kernel, tl=tl, off=off),
        grid=tl.grid(off),
        in_specs=[yf_spec, yb_spec, tl.tok(rw, off), tl.tok(rw, off), tl.tok(o_pool.shape[-1], off)]
        + tl.tok_split(o_mla_lat.shape[-1], off, 0) + [tl.tok(gate.shape[-1], off)]
        + tl.tok_split(d, off, lat_off) + [tl.mod(2, off), tl.const(lnw.shape), tl.const(lnb.shape)]
        + [tl.const(w.shape) for w in (wbp, wbm, wbr, wo)],
        out_specs=out_spec,
        out_shape=jax.ShapeDtypeStruct((b, n_tiles * tl.tm, d), F32),
        compiler_params=_params(("parallel", "parallel")),
        name="merge",
    )(y, y, bonus, gg, o_pool, o_mla_ctx, o_mla_lat, gate, x_ctx, x_lat, mod, lnw, lnb, wbp, wbm, wbr, wo)


def _mlp_kernel(x_ref, g_ref, *rest, parts, tm):
    mods, (w1_ref, w2_ref, o_ref, h_ref, acc_ref) = rest[:3 * parts], rest[3 * parts:]
    j = pl.program_id(1)

    @pl.when(j == 0)
    def _():
        for p in range(parts):
            rows = slice(p * tm, (p + 1) * tm)
            h_ref[rows, :] = _norm_mod(x_ref[rows, :], g_ref[...], mods[3 * p][...], mods[3 * p + 1][...]).astype(BF16)
        acc_ref[...] = jnp.zeros_like(acc_ref)

    a = jnp.maximum(_dot(h_ref[...], w1_ref[...]), 0.0)
    acc_ref[...] += _dot((a * a).astype(BF16), w2_ref[...])

    @pl.when(j == pl.num_programs(1) - 1)
    def _():
        for p in range(parts):
            rows = slice(p * tm, (p + 1) * tm)
            o_ref[rows, :] = x_ref[rows, :] + mods[3 * p + 2][...] * acc_ref[rows, :]


def _mlp(tl, x1, mod_off, g, mod, w1, w2):
    b, s1, d = x1.shape
    dff = w1.shape[1]
    tf = 1024 if dff % 1024 == 0 else dff
    tm = tl.tm
    per_batch = s1 // tm
    parts = next(p for p in (4, 2, 1) if (b * per_batch) % p == 0)
    nct, ctx_row = tl.nct, tl.ctx_row

    def mod_spec(j, p):
        def idx(ii, jj):
            sub = ii * parts + p
            bb, ss = sub // per_batch, sub % per_batch
            return (jnp.where(ss + mod_off < nct, ctx_row, bb), j, 0, 0)
        return pl.BlockSpec((None, None, 1, d), idx)

    mod_specs = [mod_spec(j, p) for p in range(parts) for j in (3, 4, 5)]
    x_spec = pl.BlockSpec((parts * tm, d), lambda ii, jj: (ii, 0))
    out = pl.pallas_call(
        functools.partial(_mlp_kernel, parts=parts, tm=tm),
        grid=(b * per_batch // parts, dff // tf),
        in_specs=[x_spec, pl.BlockSpec((1, d), lambda ii, jj: (0, 0))] + mod_specs
        + [pl.BlockSpec((d, tf), lambda ii, jj: (0, jj)), pl.BlockSpec((tf, d), lambda ii, jj: (jj, 0))],
        out_specs=x_spec,
        out_shape=jax.ShapeDtypeStruct((b * s1, d), F32),
        scratch_shapes=[pltpu.VMEM((parts * tm, d), BF16), pltpu.VMEM((parts * tm, d), F32)],
        compiler_params=_params(("parallel", "arbitrary")),
        name="mlp",
    )(x1.reshape(b * s1, d), g, *([mod] * (3 * parts)), w1, w2)
    return out.reshape(b, s1, d)


def _rope_tables(lc, l):
    rows = l // GRID_W
    row = jnp.repeat(jnp.arange(rows), GRID_W).astype(F32)
    col = jnp.tile(jnp.arange(GRID_W), rows).astype(F32)
    n_freq = MLA_ROPE // 4
    inv_freq = jnp.power(ROPE_BASE, -jnp.arange(n_freq, dtype=F32) / n_freq)
    ang = jnp.concatenate([row[:, None] * inv_freq, col[:, None] * inv_freq], axis=-1)
    cos = jnp.concatenate([jnp.ones((lc, MLA_ROPE // 2), F32), jnp.cos(ang)], axis=0)
    sin = jnp.concatenate([jnp.zeros((lc, MLA_ROPE // 2), F32), jnp.sin(ang)], axis=0)
    s = lc + l
    pad = jnp.zeros((s, LANES - MLA_NOPE - MLA_ROPE), F32)
    z16 = jnp.zeros((s, MLA_ROPE // 2), F32)
    zn = jnp.zeros((s, MLA_NOPE), F32)
    tc = jnp.concatenate([jnp.ones((s, MLA_NOPE), F32), cos, cos, pad], axis=1)
    ts1 = jnp.concatenate([zn, -sin, z16, pad], axis=1)
    ts2 = jnp.concatenate([zn, z16, sin, pad], axis=1)
    return tc, ts1, ts2


def _slot_cols(w, per_head):
    k = w.shape[0]
    w = w.reshape(k, MLA_HEADS, per_head)
    return jnp.pad(w, ((0, 0), (0, 0), (0, HEAD_SLOT - per_head))).reshape(k, MLA_HEADS * HEAD_SLOT)


def _block_diag(blocks):
    n = len(blocks)
    rows = []
    for i, blk in enumerate(blocks):
        rows.append(jnp.concatenate(
            [blk if j == i else jnp.zeros((blk.shape[0], blocks[j].shape[1]), blk.dtype) for j in range(n)], axis=1))
    return jnp.concatenate(rows, axis=0)


def _row(x):
    return x.reshape(1, -1).astype(F32)


def kernel(x, c, ctx, c_ctx, norm1_g, norm2_g, w_ada, b_ada, w_in, pool_w, pool_scale, mla_q_norm, mla_w_uq, mla_kv_norm, mla_w_ukv, qk_gain_q, qk_gain_k, rwkv_mu, rwkv_w0, rwkv_w2, rwkv_a0, rwkv_a2, rwkv_ka, rwkv_kk, rwkv_rk, rwkv_g2, rwkv_ln_w, rwkv_ln_b, w_br_pool, w_br_mla, w_br_rwkv, w_o, mlp_w1, mlp_w2):
    b, l, d = x.shape
    lc = ctx.shape[1]
    depth = w_in.shape[0]
    tl = _Tiles(b, lc, l, d)

    pool_width = pool_scale.shape[-1]
    q_rank = mla_q_norm.shape[-1]
    kv_rank = mla_kv_norm.shape[-1]
    rw_width = rwkv_kk.shape[-1]
    rw_in = rwkv_mu.shape[-1]

    rows = -(-(b + 1) // SUBLANES) * SUBLANES
    c_all = jnp.concatenate([c, c_ctx[None, :], jnp.zeros((rows - b - 1, d), F32)], axis=0)
    mod_all = _ada_mod(c_all, w_ada, b_ada).reshape(depth, rows, N_MOD, 1, d)
    tabs = _rope_tables(lc, l)
    seg = jnp.zeros((HEAD_SLOT, HEAD_SLOT), F32)
    seg = seg.at[:MLA_NOPE, :MLA_NOPE].set(1.0 / MLA_NOPE)
    seg = seg.at[MLA_NOPE:MLA_NOPE + MLA_ROPE, MLA_NOPE:MLA_NOPE + MLA_ROPE].set(1.0 / MLA_ROPE)
    seg = _block_diag([seg, seg]).astype(BF16)
    anti = jnp.eye(tl.tm, dtype=BF16)[::-1]

    tk = 256 if tl.s % 256 == 0 else LANES
    tq = 512
    while tq * tk > (MLA_HEADS // 2) * b * l or l % tq:
        tq //= 2
    assert tq >= 16

    x_ctx, x_lat, lat_off = ctx, x, 0
    out = None
    for i in range(depth):
        need_ctx = i < depth - 1
        off = 0 if need_ctx else tl.nct
        mod = mod_all[i]

        wi = w_in[i].astype(BF16)
        o0 = 0
        wp = wi[:, o0:o0 + pool_width]; o0 += pool_width
        wq = wi[:, o0:o0 + q_rank]; o0 += q_rank
        wkv = wi[:, o0:o0 + kv_rank + MLA_ROPE]; o0 += kv_rank + MLA_ROPE
        wkv = jnp.pad(wkv, ((0, 0), (0, LANES - MLA_ROPE)))
        wrw = wi[:, o0:o0 + rw_in]; o0 += rw_in
        wg = wi[:, o0:]

        zp, zq, zkv, zrw, gate = _in_proj(tl, x_ctx, x_lat, lat_off, _row(norm1_g[i]), mod, wp, wq, wkv, wrw, wg)

        pw_bd = _block_diag([pool_w[i, g] for g in range(pool_w.shape[1])]).astype(BF16)
        o_pool = _pool(tl, zp, pw_bd, _row(pool_scale[i]), off)

        wuq = _slot_cols(mla_w_uq[i], MLA_NOPE + MLA_ROPE).astype(BF16)
        wukv = mla_w_ukv[i].reshape(kv_rank, MLA_HEADS, MLA_NOPE + MLA_V)
        wuk = _slot_cols(wukv[:, :, :MLA_NOPE].reshape(kv_rank, -1), MLA_NOPE).astype(BF16)
        wuv = wukv[:, :, MLA_NOPE:].reshape(kv_rank, MLA_HEADS * MLA_V).astype(BF16)
        zpad = jnp.zeros((LANES - MLA_NOPE - MLA_ROPE,), F32)
        gq = _row(jnp.concatenate([qk_gain_q[i], zpad]))
        gk = _row(jnp.concatenate([qk_gain_k[i, :MLA_NOPE], jnp.zeros((LANES - MLA_NOPE,), F32)]))
        gkr = _row(jnp.concatenate([qk_gain_k[i, MLA_NOPE:], jnp.zeros((LANES - MLA_ROPE,), F32)]))
        bound = _score_bound(qk_gain_q[i], qk_gain_k[i])
        spare = (jnp.arange(LANES) == MLA_NOPE + MLA_ROPE).astype(F32)[None, :]
        smalls = (_row(mla_q_norm[i]), _row(mla_kv_norm[i]), gq, gk, gkr, seg, spare, -bound * spare)
        q, k, v = _qkv_up(tl, zq, zkv, tabs, smalls, wuq, wuk, wuv)

        w2cat = _block_diag([rwkv_w2[i, 0], rwkv_w2[i, 1]]).astype(BF16)
        a2cat = _block_diag([rwkv_a2[i, 0], rwkv_a2[i, 1]]).astype(BF16)
        rsmalls = (rwkv_mu[i].astype(F32), _row(rwkv_kk[i]), _row(rwkv_w0[i]), w2cat, _row(rwkv_a0[i]), a2cat,
                   _row(rwkv_ka[i]), _row(rwkv_rk[i]), rwkv_g2[i].astype(BF16), anti)
        r, vv, nkk, gg, bonus, wf, kf, bf, rb, vb, nkkb, wb, kb, bb = _rwkv_prep(tl, zrw, rsmalls)
        n_steps = b * (MLA_HEADS // 2) * (l // tq) * (tl.s // tk)
        ops_k, op_v = _scan_operands((r, vv, nkk, wf, kf, bf), (rb, vb, nkkb, wb, kb, bb), n_steps)
        o_mla_l, y = _attn_scan(q, k, v, ops_k, op_v, bound, tq=tq, tk=tk, n_q=l)
        y = _scan_result(y, b, tl.s, rw_width)
        if need_ctx:
            o_mla_c = _attention(q, k, v, tq=tl.tm, q_off_tiles=l // tl.tm, n_q_tiles=tl.nct, n_keys=lc)
        else:
            o_mla_c = o_mla_l

        x1 = _merge(tl, off, y, bonus, gg, o_pool, o_mla_c, o_mla_l, gate, x_ctx, x_lat, lat_off, mod,
                    _row(rwkv_ln_w[i]), _row(rwkv_ln_b[i]),
                    w_br_pool[i].astype(BF16), w_br_mla[i].astype(BF16), w_br_rwkv[i].astype(BF16),
                    w_o[i].astype(BF16))
        xc_next = _mlp(tl, x1, off, _row(norm2_g[i]), mod, mlp_w1[i].astype(BF16), mlp_w2[i].astype(BF16))
        if need_ctx:
            x_ctx, x_lat, lat_off = xc_next, xc_next, tl.nct
        else:
            out = xc_next
    return out
```

```python
import functools
import math

import jax
import jax.numpy as jnp
from jax import lax
from jax.experimental import pallas as pl
from jax.experimental.pallas import tpu as pltpu

F32 = jnp.float32
BF16 = jnp.bfloat16

NORM_EPS = 1e-6
RWKV_GN_EPS = 64e-5
GRID_W = 64
ROPE_BASE = 10000.0
POOL_HALF_WINDOWS = (1, 2, 4, 8)
N_MOD = 6
MLA_HEADS = 8
MLA_NOPE = 64
MLA_ROPE = 32
MLA_V = 64
RWKV_HEAD = 64
DECAY_RANK = 64
AAA_RANK = 64
GATE_RANK = 128

LANES = 128
SUBLANES = 8
HEAD_SLOT = LANES
HALO = SUBLANES
VMEM_LIMIT = 56 * 1024 * 1024

LOG2E = 1.4426950408889634


def _dot(a, b):
    return jnp.dot(a, b, preferred_element_type=F32)


def _sigmoid(x):
    return 1.0 / (1.0 + jnp.exp(-x))


def _rms(x, width):
    return lax.rsqrt(jnp.sum(x * x, axis=-1, keepdims=True) * (1.0 / width) + NORM_EPS)


def _norm_mod(x, g, shift, scale):
    return (x * _rms(x, x.shape[-1]) * g) * (1.0 + scale) + shift


def _lane(shape):
    return lax.broadcasted_iota(jnp.int32, shape, 1)


def _seg64_sum(x):
    cols = []
    for c in range(x.shape[1] // LANES):
        xc = x[:, c * LANES:(c + 1) * LANES]
        lo_m = _lane(xc.shape) < 64
        lo = jnp.sum(jnp.where(lo_m, xc, 0.0), axis=-1, keepdims=True)
        hi = jnp.sum(jnp.where(lo_m, 0.0, xc), axis=-1, keepdims=True)
        cols.append(jnp.where(lo_m, lo, hi))
    return cols[0] if len(cols) == 1 else jnp.concatenate(cols, axis=1)


def _params(sem, **flags):
    return pltpu.CompilerParams(dimension_semantics=sem, vmem_limit_bytes=VMEM_LIMIT, flags=flags or None)


class _Tiles:
    def __init__(self, batch, lc, l, d):
        self.batch, self.lc, self.l, self.d = batch, lc, l, d
        self.s = lc + l
        self.tm = 256 if (lc % 256 == 0 and l % 256 == 0) else 128
        assert lc % self.tm == 0 and l % self.tm == 0
        self.nct = lc // self.tm
        self.ns = self.s // self.tm
        self.ctx_row = batch

    def grid(self, off):
        return (self.batch, self.ns - off)

    def tok(self, width, off):
        return pl.BlockSpec((None, self.tm, width), lambda b, s: (b, s + off, 0))

    def tok_split(self, width, off, lat_off):
        nct = self.nct
        ctx = pl.BlockSpec((None, self.tm, width), lambda b, s: (b, jnp.minimum(s + off, nct - 1), 0))
        lat = pl.BlockSpec((None, self.tm, width), lambda b, s: (b, jnp.maximum(s + off - nct, 0) + lat_off, 0))
        return [ctx, lat]

    def pick(self, off, ctx_ref, lat_ref):
        return jnp.where(pl.program_id(1) + off < self.nct, ctx_ref[...], lat_ref[...])

    def halo_prev(self, width, off):
        r = self.tm // HALO
        return pl.BlockSpec((None, HALO, width), lambda b, s: (b, jnp.maximum((s + off) * r - 1, 0), 0))

    def halo_next(self, width, off):
        r = self.tm // HALO
        last = self.s // HALO - 1
        return pl.BlockSpec((None, HALO, width), lambda b, s: (b, jnp.minimum((s + off + 1) * r, last), 0))

    def mod(self, j, off):
        nct, ctx_row = self.nct, self.ctx_row
        return pl.BlockSpec((None, None, 1, self.d),
                            lambda b, s: (jnp.where(s + off < nct, ctx_row, b), j, 0, 0))

    def const(self, shape):
        nd = len(shape)
        return pl.BlockSpec(shape, lambda b, s: (0,) * nd)


def _ada_kernel(c_ref, w_ref, b_ref, o_ref):
    c = c_ref[...]
    s = (c * _sigmoid(c)).astype(BF16)
    o_ref[...] = _dot(s, w_ref[...].astype(BF16)) + b_ref[...]


def _ada_mod(c_all, w_ada, b_ada):
    depth, d, n = w_ada.shape
    rows = c_all.shape[0]
    tn = 1024
    return pl.pallas_call(
        _ada_kernel,
        grid=(depth, n // tn),
        in_specs=[pl.BlockSpec((rows, d), lambda i, j: (0, 0)),
                  pl.BlockSpec((None, d, tn), lambda i, j: (i, 0, j)),
                  pl.BlockSpec((None, 1, tn), lambda i, j: (i, 0, j))],
        out_specs=pl.BlockSpec((None, rows, tn), lambda i, j: (i, 0, j)),
        out_shape=jax.ShapeDtypeStruct((depth, rows, n), F32),
        compiler_params=_params(("parallel", "parallel")),
        name="ada_mod",
    )(c_all, w_ada, b_ada.reshape(depth, 1, n))


def _in_proj_kernel(xc_ref, xl_ref, g_ref, sh_ref, sc_ref, wp_ref, wq_ref, wkv_ref, wrw_ref, wg_ref,
                    zp_ref, zq_ref, zkv_ref, zrw_ref, gate_ref, *, tl):
    h = _norm_mod(tl.pick(0, xc_ref, xl_ref), g_ref[...], sh_ref[...], sc_ref[...]).astype(BF16)
    zp_ref[...] = _dot(h, wp_ref[...])
    zq_ref[...] = _dot(h, wq_ref[...])
    zkv_ref[...] = _dot(h, wkv_ref[...])
    zrw_ref[...] = _dot(h, wrw_ref[...])
    d = h.shape[-1]
    for c in range(wg_ref.shape[1] // d):
        gate_ref[:, c * d:(c + 1) * d] = _sigmoid(_dot(h, wg_ref[:, c * d:(c + 1) * d])).astype(BF16)


def _in_proj(tl, x_ctx, x_lat, lat_off, g, mod, wp, wq, wkv, wrw, wg):
    b, d = x_ctx.shape[0], x_ctx.shape[-1]
    widths = (wp.shape[1], wq.shape[1], wkv.shape[1], wrw.shape[1], wg.shape[1])
    dts = (F32, F32, F32, F32, BF16)
    return pl.pallas_call(
        functools.partial(_in_proj_kernel, tl=tl),
        grid=tl.grid(0),
        in_specs=tl.tok_split(d, 0, lat_off) + [tl.const((1, d)), tl.mod(0, 0), tl.mod(1, 0)]
        + [tl.const(w.shape) for w in (wp, wq, wkv, wrw, wg)],
        out_specs=[tl.tok(w, 0) for w in widths],
        out_shape=[jax.ShapeDtypeStruct((b, tl.s, w), dt) for w, dt in zip(widths, dts)],
        compiler_params=_params(("parallel", "parallel")),
        name="in_proj",
    )(x_ctx, x_lat, g, mod, mod, wp, wq, wkv, wrw, wg)


def _seq_flags(tl, off):
    s_abs = pl.program_id(1) + off
    is_ctx = s_abs < tl.nct
    first = jnp.logical_or(s_abs == 0, s_abs == tl.nct)
    last = jnp.logical_or(s_abs == tl.nct - 1, s_abs == tl.ns - 1)
    seq_len = jnp.where(is_ctx, tl.lc, tl.l)
    tile_in_seq = jnp.where(is_ctx, s_abs, s_abs - tl.nct)
    return first, last, seq_len, tile_in_seq


def _pool_kernel(u_ref, up_ref, un_ref, pw_ref, ps_ref, o_ref, *, tl, off):
    first, last, seq_len, tile_in_seq = _seq_flags(tl, off)
    tm = tl.tm
    u = u_ref[...]
    prev = jnp.where(first, 0.0, up_ref[...])
    nxt = jnp.where(last, 0.0, un_ref[...])
    e = jnp.concatenate([prev, u, nxt], axis=0)
    n = tm + 2 * HALO
    w2 = e + pltpu.roll(e, 1, 0)
    w4 = pltpu.roll(w2, n - 1, 0) + pltpu.roll(w2, 1, 0)
    w8 = pltpu.roll(w4, n - 2, 0) + pltpu.roll(w4, 2, 0)
    w16 = pltpu.roll(w8, n - 4, 0) + pltpu.roll(w8, 4, 0)
    sums = [w[HALO:HALO + tm] for w in (w2, w4, w8, w16)]
    width = u.shape[1]
    group = width // len(POOL_HALF_WINDOWS)
    lane = _lane((tm, width))
    pos = tile_in_seq * tm + lax.broadcasted_iota(jnp.int32, (tm, width), 0)
    total = sums[-1]
    half = jnp.full((tm, width), POOL_HALF_WINDOWS[-1], jnp.int32)
    for gi in range(len(POOL_HALF_WINDOWS) - 2, -1, -1):
        sel = lane < (gi + 1) * group
        total = jnp.where(sel, sums[gi], total)
        half = jnp.where(sel, POOL_HALF_WINDOWS[gi], half)
    cnt = jnp.minimum(pos + half, seq_len) - jnp.maximum(pos - half, 0)
    pooled = total / cnt.astype(F32) - u
    o_ref[...] = (_dot(pooled.astype(BF16), pw_ref[...]) * ps_ref[...]).astype(BF16)


def _rope(x, tc, ts1, ts2):
    return x * tc + pltpu.roll(x, LANES - MLA_ROPE // 2, 1) * ts1 + pltpu.roll(x, MLA_ROPE // 2, 1) * ts2


def _head_inv_rms(x, seg):
    w = seg.shape[0]
    cols = [lax.rsqrt(_dot((xc * xc).astype(BF16), seg) + NORM_EPS)
            for xc in (x[:, c * w:(c + 1) * w] for c in range(x.shape[1] // w))]
    return jnp.concatenate(cols, axis=1)


def _qkv_kernel(zq_ref, zkv_ref, tc_ref, ts1_ref, ts2_ref, qng_ref, kvng_ref, gq_ref, gk_ref, gkr_ref, seg_ref,
                qpad_ref, kpad_ref, wuq_ref, wuk_ref, wuv_ref, u_ref, up_ref, un_ref, pw_ref, ps_ref,
                q_ref, k_ref, v_ref, op_ref, *, tl):
    _pool_kernel(u_ref, up_ref, un_ref, pw_ref, ps_ref, op_ref, tl=tl, off=0)
    tc, ts1, ts2 = tc_ref[...], ts1_ref[...], ts2_ref[...]
    seg = seg_ref[...]

    zq = zq_ref[...]
    qc = (zq * _rms(zq, zq.shape[-1]) * qng_ref[...]).astype(BF16)
    q = _dot(qc, wuq_ref[...])
    q = q * _head_inv_rms(q, seg)
    q_scale = LOG2E * (MLA_NOPE + MLA_ROPE) ** -0.5
    for h in range(MLA_HEADS):
        qh = _rope(q[:, h * HEAD_SLOT:(h + 1) * HEAD_SLOT] * gq_ref[...], tc, ts1, ts2)
        q_ref[:, h * HEAD_SLOT:(h + 1) * HEAD_SLOT] = (qh * q_scale + qpad_ref[...]).astype(BF16)

    zkv = zkv_ref[...]
    kv_w = kvng_ref.shape[-1]
    kvc = zkv[:, :kv_w]
    kvn = (kvc * _rms(kvc, kv_w) * kvng_ref[...]).astype(BF16)
    kr = zkv[:, kv_w:kv_w + LANES]
    kr = kr * lax.rsqrt(jnp.sum(kr * kr, axis=-1, keepdims=True) * (1.0 / MLA_ROPE) + NORM_EPS) * gkr_ref[...]
    kr = _rope(pltpu.roll(kr, MLA_NOPE, 1), tc, ts1, ts2)
    kn = _dot(kvn, wuk_ref[...])
    kn = kn * _head_inv_rms(kn, seg)
    for h in range(MLA_HEADS):
        kh = kn[:, h * HEAD_SLOT:(h + 1) * HEAD_SLOT]
        k_ref[:, h * HEAD_SLOT:(h + 1) * HEAD_SLOT] = (kh * gk_ref[...] + kr + kpad_ref[...]).astype(BF16)
    v_ref[...] = _dot(kvn, wuv_ref[...]).astype(BF16)


def _qkv_up(tl, zq, zkv, zp, pw_bd, ps, tabs, smalls, wuq, wuk, wuv):
    b, s, _ = zq.shape
    tc, ts1, ts2 = tabs
    tab_spec = pl.BlockSpec((tl.tm, LANES), lambda bb, ss: (ss, 0))
    widths = (wuq.shape[1], wuk.shape[1], wuv.shape[1], zp.shape[-1])
    pw = zp.shape[-1]
    nct, nlt = tl.nct, tl.ns - tl.nct
    q_spec = pl.BlockSpec((None, tl.tm, widths[0]), lambda bb, ss: (bb, jnp.where(ss < nct, nlt + ss, ss - nct), 0))
    return pl.pallas_call(
        functools.partial(_qkv_kernel, tl=tl),
        grid=tl.grid(0),
        in_specs=[tl.tok(zq.shape[-1], 0), tl.tok(zkv.shape[-1], 0), tab_spec, tab_spec, tab_spec]
        + [tl.const(a.shape) for a in smalls] + [tl.const(w.shape) for w in (wuq, wuk, wuv)]
        + [tl.tok(pw, 0), tl.halo_prev(pw, 0), tl.halo_next(pw, 0), tl.const(pw_bd.shape), tl.const(ps.shape)],
        out_specs=[q_spec, tl.tok(widths[1], 0), tl.tok(widths[2], 0), tl.tok(pw, 0)],
        out_shape=[jax.ShapeDtypeStruct((b, s, w), BF16) for w in widths],
        compiler_params=_params(("parallel", "parallel")),
        name="qkv_up",
    )(zq, zkv, tc, ts1, ts2, *smalls, wuq, wuk, wuv, zp, zp, zp, pw_bd, ps)


def _attn_kernel(q_ref, k_ref, v_ref, o_ref):
    v = v_ref[...]
    outs = []
    for hh in range(2):
        q = q_ref[:, hh * HEAD_SLOT:(hh + 1) * HEAD_SLOT]
        k = k_ref[:, hh * HEAD_SLOT:(hh + 1) * HEAD_SLOT]
        s = lax.dot_general(q, k, (((1,), (1,)), ((), ())), preferred_element_type=F32)
        p = jnp.exp2(s - jnp.max(s, axis=-1, keepdims=True))
        denom = jnp.sum(p, axis=-1, keepdims=True)
        outs.append(_dot(p.astype(BF16), v) / denom)
    o_ref[...] = jnp.where(_lane(outs[0].shape) < MLA_V, outs[0], outs[1]).astype(BF16)


def _attention(q, k, v, *, tq, q_off_tiles, n_q_tiles, n_keys):
    b, _, qw = q.shape
    pairs = qw // (2 * HEAD_SLOT)
    return pl.pallas_call(
        _attn_kernel,
        grid=(b, pairs, n_q_tiles),
        in_specs=[pl.BlockSpec((None, tq, 2 * HEAD_SLOT), lambda bb, hp, i: (bb, i + q_off_tiles, hp)),
                  pl.BlockSpec((None, n_keys, 2 * HEAD_SLOT), lambda bb, hp, i: (bb, 0, hp)),
                  pl.BlockSpec((None, n_keys, 2 * MLA_V), lambda bb, hp, i: (bb, 0, hp))],
        out_specs=pl.BlockSpec((None, tq, 2 * MLA_V), lambda bb, hp, i: (bb, i, hp)),
        out_shape=jax.ShapeDtypeStruct((b, n_q_tiles * tq, pairs * 2 * MLA_V), BF16),
        compiler_params=_params(("parallel", "parallel", "arbitrary")),
        name="attention",
    )(q, k, v)


def _rev_rows(x):
    n = x.shape[0]
    row = lax.broadcasted_iota(jnp.int32, x.shape, 0)
    for sh in (1, 2, 4):
        x = jnp.where((row & sh) == 0, pltpu.roll(x, n - sh, 0), pltpu.roll(x, sh, 0))
    groups = [x[g * SUBLANES:(g + 1) * SUBLANES] for g in range(n // SUBLANES)]
    return jnp.concatenate(groups[::-1], axis=0)


def _rev_tile(tl, s_abs):
    return jnp.where(s_abs < tl.nct, tl.nct - 1 - s_abs, tl.ns - 1 + tl.nct - s_abs)


def _rwkv_prep_kernel(z_ref, zp_ref, zn_ref, mu_ref, kkw_ref, w0_ref, w2_ref, a0_ref, a2_ref, ka_ref,
                      rk_ref, g2_ref, anti_ref,
                      r_o, v_o, nkk_o, gg_o, bonus_o, wf_o, kf_o, bf_o, rb_o, vb_o, nkkb_o, wb_o, kb_o, bb_o,
                      *, tl, off):
    first, last, _, _ = _seq_flags(tl, off)
    tm = tl.tm
    z = z_ref[...]
    row = lax.broadcasted_iota(jnp.int32, z.shape, 0)
    prev_row = jnp.where(first, 0.0, zp_ref[HALO - 1:HALO, :])
    next_row = jnp.where(last, 0.0, zn_ref[0:1, :])
    z_prev = jnp.where(row == 0, prev_row, pltpu.roll(z, 1, 0))
    z_next = jnp.where(row == tm - 1, next_row, pltpu.roll(z, tm - 1, 0))
    zs = z + mu_ref[0:1, :] * (z_prev - z) + mu_ref[1:2, :] * (z_next - z)

    w = kkw_ref.shape[-1]
    r, k, v = zs[:, 0:w], zs[:, w:2 * w], zs[:, 2 * w:3 * w]
    o = 3 * w
    wd = zs[:, o:o + 2 * DECAY_RANK]
    ad = zs[:, o + 2 * DECAY_RANK:o + 2 * DECAY_RANK + 2 * AAA_RANK]
    gd = zs[:, o + 2 * DECAY_RANK + 2 * AAA_RANK:]

    kk = k * kkw_ref[...]
    kk = kk * lax.rsqrt(jnp.maximum(_seg64_sum(kk * kk), 1e-24))
    u = w0_ref[...] + _dot(jnp.tanh(wd).astype(BF16), w2_ref[...])
    log_decay = -math.exp(-0.5) * _sigmoid(u)
    a = _sigmoid(a0_ref[...] + _dot(ad.astype(BF16), a2_ref[...]))
    ka = ka_ref[...]
    k_sum = jnp.zeros_like(k)
    anti = anti_ref[...]

    def put16(o_ref, t, backward):
        t = t.T.astype(BF16)
        o_ref[...] = _dot(t, anti).astype(BF16) if backward else t

    for d, (w_o, k_o, b_o) in enumerate(((wf_o, kf_o, bf_o), (wb_o, kb_o, bb_o))):
        a_d = a[:, d * w:(d + 1) * w]
        k_d = k * (1.0 + (a_d - 1.0) * ka[:, d * w:(d + 1) * w])
        put16(w_o, log_decay[:, d * w:(d + 1) * w], d)
        put16(k_o, k_d, d)
        put16(b_o, kk * a_d, d)
        k_sum = k_sum + k_d
    put16(r_o, r, 0)
    put16(rb_o, r, 1)
    put16(nkk_o, -kk, 0)
    put16(nkkb_o, -kk, 1)
    put16(v_o, v, 0)
    put16(vb_o, v, 1)
    gg_o[...] = _dot(_sigmoid(gd).astype(BF16), g2_ref[...])
    bonus_o[...] = _seg64_sum(r * (0.5 * k_sum) * rk_ref[...]) * v


def _rwkv_prep(tl, zrw, smalls):
    b, s, win = zrw.shape
    w = smalls[1].shape[-1]
    fwd = pl.BlockSpec((None, w, tl.tm), lambda bb, ss: (bb, 0, ss))
    bwd = pl.BlockSpec((None, w, tl.tm), lambda bb, ss: (bb, 0, _rev_tile(tl, ss)))
    tok = tl.tok(w, 0)

    def chan(dt):
        return jax.ShapeDtypeStruct((b, w, s), dt)

    tokf = jax.ShapeDtypeStruct((b, s, w), F32)
    return pl.pallas_call(
        functools.partial(_rwkv_prep_kernel, tl=tl, off=0),
        grid=tl.grid(0),
        in_specs=[tl.tok(win, 0), tl.halo_prev(win, 0), tl.halo_next(win, 0)]
        + [tl.const(a.shape) for a in smalls],
        out_specs=[fwd, fwd, fwd, tok, tok, fwd, fwd, fwd] + [bwd] * 6,
        out_shape=[chan(BF16), chan(BF16), chan(BF16), tokf, tokf] + [chan(BF16)] * 9,
        compiler_params=_params(("parallel", "parallel")),
        name="rwkv_prep",
    )(zrw, zrw, zrw, *smalls)


def _zero_after(x):
    bits = pltpu.bitcast(x[:SUBLANES, :LANES].astype(F32), jnp.uint32)
    return pltpu.bitcast((bits >> 16) >> 16, F32)


def _scan_step(t, a_ref, b_ref, k_ref, r_ref, v_ref, y_ref, s_ref, after):
    n_k = s_ref.shape[0]
    v = v_ref[t].astype(F32) + jnp.concatenate([after] * (v_ref.shape[1] // SUBLANES), axis=0)
    acc = [jnp.zeros_like(v), jnp.zeros_like(v)]
    for kk in range(n_k):
        acc[kk % 2] = acc[kk % 2] + s_ref[kk] * a_ref[t, pl.ds(kk, 1), :]
    sa = acc[0] + acc[1]
    yacc = [jnp.zeros_like(v), jnp.zeros_like(v)]
    for kk in range(n_k):
        sn = s_ref[kk] + sa * b_ref[t, pl.ds(kk, 1), :] + v * k_ref[t, pl.ds(kk, 1), :]
        s_ref[kk] = sn
        yacc[kk % 2] = yacc[kk % 2] + sn * r_ref[t, pl.ds(kk, 1), :]
    y_ref[t] = yacc[0] + yacc[1]


def _attn_scan_kernel(static_ref, q_ref, k_ref, v_ref, ab_ref, kr_ref, lw_ref, vv_ref, o_ref, y_ref,
                      s_ref, m_ref, acc_ref, ops_ref, *, tk):
    first = jnp.logical_and(pl.program_id(0) == 0, jnp.logical_and(pl.program_id(1) == 0, pl.program_id(2) == 0))

    @pl.when(first)
    def _():
        s_ref[...] = jnp.zeros_like(s_ref)

    def unpack_pair(ref):
        x = ref[...].astype(F32)
        swapped = pltpu.roll(x.reshape(-1, LANES), LANES // 2, 1).reshape(x.shape)
        lo = lax.broadcasted_iota(jnp.int32, x.shape, 2) < LANES // 2
        return jnp.where(lo, x, swapped), jnp.where(lo, swapped, x)

    nkv = vv_ref.shape[0]
    a, b = unpack_pair(ab_ref)
    kd, r = unpack_pair(kr_ref)
    lw = jnp.concatenate(unpack_pair(lw_ref), axis=1)
    log_w = jnp.zeros(lw.shape[1:], F32)
    for t in range(nkv):
        ops_ref[0, t] = a[t] * jnp.exp(log_w)
        log_w = log_w + lw[t]
        inv_w = jnp.exp(-log_w)
        ops_ref[1, t] = b[t] * inv_w
        ops_ref[2, t] = kd[t] * inv_w
        ops_ref[3, t] = r[t] * jnp.exp(log_w)
    ops_ref[4, 0] = jnp.exp(log_w)
    a_ref, b_ref, kk_ref, r_ref = (ops_ref.at[i] for i in range(4))

    m_ref[...] = jnp.full_like(m_ref, -1e30)
    acc_ref[...] = jnp.zeros_like(acc_ref)
    lo_half = _lane((tk, 2 * MLA_V)) < MLA_V

    def block(j, after, static_max):
        _scan_step(j, a_ref, b_ref, kk_ref, r_ref, vv_ref, y_ref, s_ref, after)
        start = j * tk
        vblk = v_ref[pl.ds(start, tk), :]
        v_ones = (jnp.where(lo_half, vblk, 1.0), jnp.where(lo_half, 1.0, vblk))
        for hh in range(2):
            q = q_ref[:, hh * HEAD_SLOT:(hh + 1) * HEAD_SLOT]
            kblk = k_ref[pl.ds(start, tk), hh * HEAD_SLOT:(hh + 1) * HEAD_SLOT]
            s = lax.dot_general(q, kblk, (((1,), (1,)), ((), ())), preferred_element_type=F32)
            if static_max:
                new = acc_ref[hh] + _dot(jnp.exp2(s).astype(BF16), v_ones[hh])
            else:
                m_old = m_ref[hh]
                m_new = jnp.maximum(m_old, jnp.max(s, axis=-1, keepdims=True))
                p = jnp.exp2(s - jnp.concatenate([m_new] * (tk // LANES), axis=1))
                new = jnp.exp2(m_old - m_new) * acc_ref[hh] + _dot(p.astype(BF16), v_ones[hh])
                m_ref[hh] = m_new
            acc_ref[hh] = new
        return _zero_after(new)

    def run(static_max):
        after = jnp.zeros((SUBLANES, LANES), F32)
        for j in range(nkv):
            after = block(j, after, static_max)

    use_static = static_ref[0] != 0
    pl.when(use_static)(functools.partial(run, True))
    pl.when(jnp.logical_not(use_static))(functools.partial(run, False))
    for kk in range(s_ref.shape[0]):
        s_ref[kk] = s_ref[kk] * ops_ref[4, 0, pl.ds(kk, 1), :]
    a0, a1 = acc_ref[0], acc_ref[1]
    o = jnp.where(_lane(a0.shape) < MLA_V, a0 / pltpu.roll(a0, MLA_V, 1), a1 / pltpu.roll(a1, MLA_V, 1))
    o_ref[...] = o.astype(BF16)


def _score_bound(gain_q, gain_k):
    def norm2(g):
        return MLA_NOPE * jnp.max(jnp.square(g[:MLA_NOPE])) + MLA_ROPE * jnp.max(jnp.square(g[MLA_NOPE:]))
    q_scale = LOG2E * (MLA_NOPE + MLA_ROPE) ** -0.5
    return 1.02 * q_scale * jnp.sqrt(norm2(gain_q) * norm2(gain_k))


STATIC_SOFTMAX_MAX_BOUND = 50.0


def _attn_scan(q, k, v, ops_k, op_v, bound, *, tq, tk, n_q):
    b, s, qw = q.shape[0], k.shape[1], q.shape[2]
    pairs = qw // (2 * HEAD_SLOT)
    nq_t, nkv = n_q // tq, s // tk
    n_k, nv = ops_k[0].shape[1], op_v.shape[1]
    assert op_v.shape[0] == b * pairs * nq_t * nkv and op_v.shape[2] == LANES

    def step_idx(bb, hp, i):
        return ((bb * pairs + hp) * nq_t + i, 0, 0)

    step_specs = [pl.BlockSpec((nkv, x.shape[1], LANES), step_idx) for x in (*ops_k, op_v)]
    vspec = step_specs[-1]

    use_static = (bound <= STATIC_SOFTMAX_MAX_BOUND).astype(jnp.int32).reshape(1)
    return pl.pallas_call(
        functools.partial(_attn_scan_kernel, tk=tk),
        grid=(b, pairs, nq_t),
        in_specs=[pl.BlockSpec(memory_space=pltpu.SMEM),
                  pl.BlockSpec((None, tq, 2 * HEAD_SLOT), lambda bb, hp, i: (bb, i, hp)),
                  pl.BlockSpec((None, s, 2 * HEAD_SLOT), lambda bb, hp, i: (bb, 0, hp)),
                  pl.BlockSpec((None, s, 2 * MLA_V), lambda bb, hp, i: (bb, 0, hp))] + step_specs,
        out_specs=[pl.BlockSpec((None, tq, 2 * MLA_V), lambda bb, hp, i: (bb, i, hp)), vspec],
        out_shape=[jax.ShapeDtypeStruct((b, n_q, pairs * 2 * MLA_V), BF16), jax.ShapeDtypeStruct(op_v.shape, F32)],
        scratch_shapes=[pltpu.VMEM((n_k, nv, LANES), F32), pltpu.VMEM((2, tq, LANES), F32),
                        pltpu.VMEM((2, tq, 2 * MLA_V), F32), pltpu.VMEM((5, nkv, n_k, LANES), F32)],
        compiler_params=_params(("arbitrary", "arbitrary", "arbitrary")),
        name="attn_scan",
    )(use_static, q, k, v, *ops_k, op_v)


def _scan_operands(fwd, bwd, n_steps):
    r, v, nkk, w_f, k_f, b_f = fwd
    r_b, v_b, nkk_b, w_b, k_b, b_b = bwd
    b, width, s = r.shape
    heads = width // RWKV_HEAD
    half = RWKV_HEAD // 2
    nlh = 2 * b * heads
    assert nlh <= LANES // 2

    def pad(x):
        return jnp.pad(x, ((0, n_steps - s), (0, 0), (0, 0), (0, LANES // 2 - nlh))).reshape(n_steps, -1, LANES)

    def k_half(x_f, x_b):
        z = jnp.stack([x_f, x_b]).reshape(2, b, heads, RWKV_HEAD, s)
        return z.transpose(4, 3, 0, 1, 2).reshape(s, RWKV_HEAD, nlh)

    def pair(lo, hi):
        return pad(jnp.stack([lo, hi], axis=2))

    lw = k_half(w_f, w_b)
    ops_k = [pair(k_half(nkk, nkk_b), k_half(b_f, b_b)), pair(k_half(k_f, k_b), k_half(r, r_b)),
             pair(lw[:, :RWKV_HEAD // 2], lw[:, RWKV_HEAD // 2:])]
    z = jnp.stack([v, v_b]).reshape(2, b, heads, 2, half, s)
    return ops_k, pad(z.transpose(5, 4, 3, 0, 1, 2).reshape(s, half, 2, nlh))


def _scan_result(y, b, s, width):
    heads = width // RWKV_HEAD
    half = RWKV_HEAD // 2
    y = y[:s].reshape(s, half, 2, LANES // 2)[..., :2 * b * heads]
    return y.reshape(s, half, 2, 2, b, heads).transpose(3, 4, 0, 5, 2, 1).reshape(2, b, s, width)


def _merge_kernel(yf_ref, yb_ref, bonus_ref, gg_ref, op_ref, omc_ref, oml_ref, gate_ref, xc_ref, xl_ref, g1_ref,
                  lnw_ref, lnb_ref, wbp_ref, wbm_ref, wbr_ref, wo_ref, o_ref, *, tl, off):
    y = yf_ref[...] + _rev_rows(yb_ref[...])
    mu = _seg64_sum(y) * (1.0 / RWKV_HEAD)
    yc = y - mu
    var = _seg64_sum(yc * yc) * (1.0 / RWKV_HEAD)
    yn = yc * lax.rsqrt(var + RWKV_GN_EPS) * lnw_ref[...] + lnb_ref[...]
    o_rw = ((yn + bonus_ref[...]) * gg_ref[...]).astype(BF16)
    x = tl.pick(off, xc_ref, xl_ref)
    d = x.shape[-1]
    m = (gate_ref[:, 0:d].astype(F32) * _dot(op_ref[...], wbp_ref[...])
         + gate_ref[:, d:2 * d].astype(F32) * _dot(tl.pick(off, omc_ref, oml_ref), wbm_ref[...])
         + gate_ref[:, 2 * d:3 * d].astype(F32) * _dot(o_rw, wbr_ref[...]))
    o_ref[...] = x + g1_ref[...] * _dot(m.astype(BF16), wo_ref[...])


def _merge(tl, off, y, bonus, gg, o_pool, o_mla_ctx, o_mla_lat, gate, x_ctx, x_lat, lat_off, mod,
           lnw, lnb, wbp, wbm, wbr, wo):
    b, d = x_ctx.shape[0], x_ctx.shape[-1]
    n_tiles = tl.ns - off
    out_spec = pl.BlockSpec((None, tl.tm, d), lambda bb, ss: (bb, ss, 0))
    rw = y.shape[-1]
    yf_spec = pl.BlockSpec((None, None, tl.tm, rw), lambda bb, ss: (0, bb, ss + off, 0))
    yb_spec = pl.BlockSpec((None, None, tl.tm, rw), lambda bb, ss: (1, bb, _rev_tile(tl, ss + off), 0))
    return pl.pallas_call(
        functools.partial(_merge_kernel, tl=tl, off=off),
        grid=tl.grid(off),
        in_specs=[yf_spec, yb_spec, tl.tok(rw, off), tl.tok(rw, off), tl.tok(o_pool.shape[-1], off)]
        + tl.tok_split(o_mla_lat.shape[-1], off, 0) + [tl.tok(gate.shape[-1], off)]
        + tl.tok_split(d, off, lat_off) + [tl.mod(2, off), tl.const(lnw.shape), tl.const(lnb.shape)]
        + [tl.const(w.shape) for w in (wbp, wbm, wbr, wo)],
        out_specs=out_spec,
        out_shape=jax.ShapeDtypeStruct((b, n_tiles * tl.tm, d), F32),
        compiler_params=_params(("parallel", "parallel")),
        name="merge",
    )(y, y, bonus, gg, o_pool, o_mla_ctx, o_mla_lat, gate, x_ctx, x_lat, mod, lnw, lnb, wbp, wbm, wbr, wo)


def _mlp_kernel(x_ref, g_ref, *rest, parts, tm):
    mods, (w1_ref, w2_ref, o_ref, h_ref, acc_ref) = rest[:3 * parts], rest[3 * parts:]
    j = pl.program_id(1)

    @pl.when(j == 0)
    def _():
        for p in range(parts):
            rows = slice(p * tm, (p + 1) * tm)
            h_ref[rows, :] = _norm_mod(x_ref[rows, :], g_ref[...], mods[3 * p][...], mods[3 * p + 1][...]).astype(BF16)
        acc_ref[...] = jnp.zeros_like(acc_ref)

    a = jnp.maximum(_dot(h_ref[...], w1_ref[...]), 0.0)
    acc_ref[...] += _dot((a * a).astype(BF16), w2_ref[...])

    @pl.when(j == pl.num_programs(1) - 1)
    def _():
        for p in range(parts):
            rows = slice(p * tm, (p + 1) * tm)
            o_ref[rows, :] = x_ref[rows, :] + mods[3 * p + 2][...] * acc_ref[rows, :]


def _mlp(tl, x1, mod_off, g, mod, w1, w2):
    b, s1, d = x1.shape
    dff = w1.shape[1]
    tf = 1024 if dff % 1024 == 0 else dff
    tm = tl.tm
    per_batch = s1 // tm
    parts = next(p for p in (4, 2, 1) if (b * per_batch) % p == 0)
    nct, ctx_row = tl.nct, tl.ctx_row

    def mod_spec(j, p):
        def idx(ii, jj):
            sub = ii * parts + p
            bb, ss = sub // per_batch, sub % per_batch
            return (jnp.where(ss + mod_off < nct, ctx_row, bb), j, 0, 0)
        return pl.BlockSpec((None, None, 1, d), idx)

    mod_specs = [mod_spec(j, p) for p in range(parts) for j in (3, 4, 5)]
    x_spec = pl.BlockSpec((parts * tm, d), lambda ii, jj: (ii, 0))
    out = pl.pallas_call(
        functools.partial(_mlp_kernel, parts=parts, tm=tm),
        grid=(b * per_batch // parts, dff // tf),
        in_specs=[x_spec, pl.BlockSpec((1, d), lambda ii, jj: (0, 0))] + mod_specs
        + [pl.BlockSpec((d, tf), lambda ii, jj: (0, jj)), pl.BlockSpec((tf, d), lambda ii, jj: (jj, 0))],
        out_specs=x_spec,
        out_shape=jax.ShapeDtypeStruct((b * s1, d), F32),
        scratch_shapes=[pltpu.VMEM((parts * tm, d), BF16), pltpu.VMEM((parts * tm, d), F32)],
        compiler_params=_params(("parallel", "arbitrary")),
        name="mlp",
    )(x1.reshape(b * s1, d), g, *([mod] * (3 * parts)), w1, w2)
    return out.reshape(b, s1, d)


def _rope_tables(lc, l):
    rows = l // GRID_W
    row = jnp.repeat(jnp.arange(rows), GRID_W).astype(F32)
    col = jnp.tile(jnp.arange(GRID_W), rows).astype(F32)
    n_freq = MLA_ROPE // 4
    inv_freq = jnp.power(ROPE_BASE, -jnp.arange(n_freq, dtype=F32) / n_freq)
    ang = jnp.concatenate([row[:, None] * inv_freq, col[:, None] * inv_freq], axis=-1)
    cos = jnp.concatenate([jnp.ones((lc, MLA_ROPE // 2), F32), jnp.cos(ang)], axis=0)
    sin = jnp.concatenate([jnp.zeros((lc, MLA_ROPE // 2), F32), jnp.sin(ang)], axis=0)
    s = lc + l
    pad = jnp.zeros((s, LANES - MLA_NOPE - MLA_ROPE), F32)
    z16 = jnp.zeros((s, MLA_ROPE // 2), F32)
    zn = jnp.zeros((s, MLA_NOPE), F32)
    tc = jnp.concatenate([jnp.ones((s, MLA_NOPE), F32), cos, cos, pad], axis=1)
    ts1 = jnp.concatenate([zn, -sin, z16, pad], axis=1)
    ts2 = jnp.concatenate([zn, z16, sin, pad], axis=1)
    return tc, ts1, ts2


def _slot_cols(w, per_head):
    k = w.shape[0]
    w = w.reshape(k, MLA_HEADS, per_head)
    return jnp.pad(w, ((0, 0), (0, 0), (0, HEAD_SLOT - per_head))).reshape(k, MLA_HEADS * HEAD_SLOT)


def _block_diag(blocks):
    n = len(blocks)
    rows = []
    for i, blk in enumerate(blocks):
        rows.append(jnp.concatenate(
            [blk if j == i else jnp.zeros((blk.shape[0], blocks[j].shape[1]), blk.dtype) for j in range(n)], axis=1))
    return jnp.concatenate(rows, axis=0)


def _row(x):
    return x.reshape(1, -1).astype(F32)


def kernel(x, c, ctx, c_ctx, norm1_g, norm2_g, w_ada, b_ada, w_in, pool_w, pool_scale, mla_q_norm, mla_w_uq, mla_kv_norm, mla_w_ukv, qk_gain_q, qk_gain_k, rwkv_mu, rwkv_w0, rwkv_w2, rwkv_a0, rwkv_a2, rwkv_ka, rwkv_kk, rwkv_rk, rwkv_g2, rwkv_ln_w, rwkv_ln_b, w_br_pool, w_br_mla, w_br_rwkv, w_o, mlp_w1, mlp_w2):
    b, l, d = x.shape
    lc = ctx.shape[1]
    depth = w_in.shape[0]
    tl = _Tiles(b, lc, l, d)

    pool_width = pool_scale.shape[-1]
    q_rank = mla_q_norm.shape[-1]
    kv_rank = mla_kv_norm.shape[-1]
    rw_width = rwkv_kk.shape[-1]
    rw_in = rwkv_mu.shape[-1]

    rows = -(-(b + 1) // SUBLANES) * SUBLANES
    c_all = jnp.concatenate([c, c_ctx[None, :], jnp.zeros((rows - b - 1, d), F32)], axis=0)
    mod_all = _ada_mod(c_all, w_ada, b_ada).reshape(depth, rows, N_MOD, 1, d)
    tabs = _rope_tables(lc, l)
    seg = jnp.zeros((HEAD_SLOT, HEAD_SLOT), F32)
    seg = seg.at[:MLA_NOPE, :MLA_NOPE].set(1.0 / MLA_NOPE)
    seg = seg.at[MLA_NOPE:MLA_NOPE + MLA_ROPE, MLA_NOPE:MLA_NOPE + MLA_ROPE].set(1.0 / MLA_ROPE)
    seg = _block_diag([seg, seg]).astype(BF16)
    anti = jnp.eye(tl.tm, dtype=BF16)[::-1]

    tk = 256 if tl.s % 256 == 0 else LANES
    tq = 512
    while tq * tk > (MLA_HEADS // 2) * b * l or l % tq:
        tq //= 2
    assert tq >= 16

    x_ctx, x_lat, lat_off = ctx, x, 0
    out = None
    for i in range(depth):
        need_ctx = i < depth - 1
        off = 0 if need_ctx else tl.nct
        mod = mod_all[i]

        wi = w_in[i].astype(BF16)
        o0 = 0
        wp = wi[:, o0:o0 + pool_width]; o0 += pool_width
        wq = wi[:, o0:o0 + q_rank]; o0 += q_rank
        wkv = wi[:, o0:o0 + kv_rank + MLA_ROPE]; o0 += kv_rank + MLA_ROPE
        wkv = jnp.pad(wkv, ((0, 0), (0, LANES - MLA_ROPE)))
        wrw = wi[:, o0:o0 + rw_in]; o0 += rw_in
        wg = wi[:, o0:]

        zp, zq, zkv, zrw, gate = _in_proj(tl, x_ctx, x_lat, lat_off, _row(norm1_g[i]), mod, wp, wq, wkv, wrw, wg)

        pw_bd = _block_diag([pool_w[i, g] for g in range(pool_w.shape[1])]).astype(BF16)

        wuq = _slot_cols(mla_w_uq[i], MLA_NOPE + MLA_ROPE).astype(BF16)
        wukv = mla_w_ukv[i].reshape(kv_rank, MLA_HEADS, MLA_NOPE + MLA_V)
        wuk = _slot_cols(wukv[:, :, :MLA_NOPE].reshape(kv_rank, -1), MLA_NOPE).astype(BF16)
        wuv = wukv[:, :, MLA_NOPE:].reshape(kv_rank, MLA_HEADS * MLA_V).astype(BF16)
        zpad = jnp.zeros((LANES - MLA_NOPE - MLA_ROPE,), F32)
        gq = _row(jnp.concatenate([qk_gain_q[i], zpad]))
        gk = _row(jnp.concatenate([qk_gain_k[i, :MLA_NOPE], jnp.zeros((LANES - MLA_NOPE,), F32)]))
        gkr = _row(jnp.concatenate([qk_gain_k[i, MLA_NOPE:], jnp.zeros((LANES - MLA_ROPE,), F32)]))
        bound = _score_bound(qk_gain_q[i], qk_gain_k[i])
        spare = (jnp.arange(LANES) == MLA_NOPE + MLA_ROPE).astype(F32)[None, :]
        smalls = (_row(mla_q_norm[i]), _row(mla_kv_norm[i]), gq, gk, gkr, seg, spare, -bound * spare)
        q, k, v, o_pool = _qkv_up(tl, zq, zkv, zp, pw_bd, _row(pool_scale[i]), tabs, smalls, wuq, wuk, wuv)

        w2cat = _block_diag([rwkv_w2[i, 0], rwkv_w2[i, 1]]).astype(BF16)
        a2cat = _block_diag([rwkv_a2[i, 0], rwkv_a2[i, 1]]).astype(BF16)
        rsmalls = (rwkv_mu[i].astype(F32), _row(rwkv_kk[i]), _row(rwkv_w0[i]), w2cat, _row(rwkv_a0[i]), a2cat,
                   _row(rwkv_ka[i]), _row(rwkv_rk[i]), rwkv_g2[i].astype(BF16), anti)
        r, vv, nkk, gg, bonus, wf, kf, bf, rb, vb, nkkb, wb, kb, bb = _rwkv_prep(tl, zrw, rsmalls)
        n_steps = b * (MLA_HEADS // 2) * (l // tq) * (tl.s // tk)
        ops_k, op_v = _scan_operands((r, vv, nkk, wf, kf, bf), (rb, vb, nkkb, wb, kb, bb), n_steps)
        o_mla_l, y = _attn_scan(q, k, v, ops_k, op_v, bound, tq=tq, tk=tk, n_q=l)
        y = _scan_result(y, b, tl.s, rw_width)
        if need_ctx:
            o_mla_c = _attention(q, k, v, tq=tl.tm, q_off_tiles=l // tl.tm, n_q_tiles=tl.nct, n_keys=lc)
        else:
            o_mla_c = o_mla_l

        x1 = _merge(tl, off, y, bonus, gg, o_pool, o_mla_c, o_mla_l, gate, x_ctx, x_lat, lat_off, mod,
                    _row(rwkv_ln_w[i]), _row(rwkv_ln_b[i]),
                    w_br_pool[i].astype(BF16), w_br_mla[i].astype(BF16), w_br_rwkv[i].astype(BF16),
                    w_o[i].astype(BF16))
        xc_next = _mlp(tl, x1, off, _row(norm2_g[i]), mod, mlp_w1[i].astype(BF16), mlp_w2[i].astype(BF16))
        if need_ctx:
            x_ctx, x_lat, lat_off = xc_next, xc_next, tl.nct
        else:
            out = xc_next
    return out
```

```python
import functools
import math

import jax
import jax.numpy as jnp
from jax import lax
from jax.experimental import pallas as pl
from jax.experimental.pallas import tpu as pltpu

F32 = jnp.float32
BF16 = jnp.bfloat16

NORM_EPS = 1e-6
RWKV_GN_EPS = 64e-5
GRID_W = 64
ROPE_BASE = 10000.0
POOL_HALF_WINDOWS = (1, 2, 4, 8)
N_MOD = 6
MLA_HEADS = 8
MLA_NOPE = 64
MLA_ROPE = 32
MLA_V = 64
RWKV_HEAD = 64
DECAY_RANK = 64
AAA_RANK = 64
GATE_RANK = 128

LANES = 128
SUBLANES = 8
HEAD_SLOT = LANES
HALO = SUBLANES
VMEM_LIMIT = 56 * 1024 * 1024

LOG2E = 1.4426950408889634


def _dot(a, b):
    return jnp.dot(a, b, preferred_element_type=F32)


def _sigmoid(x):
    return 1.0 / (1.0 + jnp.exp(-x))


def _rms(x, width):
    return lax.rsqrt(jnp.sum(x * x, axis=-1, keepdims=True) * (1.0 / width) + NORM_EPS)


def _norm_mod(x, g, shift, scale):
    return (x * _rms(x, x.shape[-1]) * g) * (1.0 + scale) + shift


def _lane(shape):
    return lax.broadcasted_iota(jnp.int32, shape, 1)


def _seg64_sum(x):
    cols = []
    for c in range(x.shape[1] // LANES):
        xc = x[:, c * LANES:(c + 1) * LANES]
        lo_m = _lane(xc.shape) < 64
        lo = jnp.sum(jnp.where(lo_m, xc, 0.0), axis=-1, keepdims=True)
        hi = jnp.sum(jnp.where(lo_m, 0.0, xc), axis=-1, keepdims=True)
        cols.append(jnp.where(lo_m, lo, hi))
    return cols[0] if len(cols) == 1 else jnp.concatenate(cols, axis=1)


def _params(sem, **flags):
    return pltpu.CompilerParams(dimension_semantics=sem, vmem_limit_bytes=VMEM_LIMIT, flags=flags or None)


class _Tiles:
    def __init__(self, batch, lc, l, d):
        self.batch, self.lc, self.l, self.d = batch, lc, l, d
        self.s = lc + l
        self.tm = 256 if (lc % 256 == 0 and l % 256 == 0) else 128
        assert lc % self.tm == 0 and l % self.tm == 0
        self.nct = lc // self.tm
        self.ns = self.s // self.tm
        self.ctx_row = batch

    def grid(self, off):
        return (self.batch, self.ns - off)

    def tok(self, width, off):
        return pl.BlockSpec((None, self.tm, width), lambda b, s: (b, s + off, 0))

    def tok_split(self, width, off, lat_off):
        nct = self.nct
        ctx = pl.BlockSpec((None, self.tm, width), lambda b, s: (b, jnp.minimum(s + off, nct - 1), 0))
        lat = pl.BlockSpec((None, self.tm, width), lambda b, s: (b, jnp.maximum(s + off - nct, 0) + lat_off, 0))
        return [ctx, lat]

    def pick(self, off, ctx_ref, lat_ref):
        return jnp.where(pl.program_id(1) + off < self.nct, ctx_ref[...], lat_ref[...])

    def halo_prev(self, width, off):
        r = self.tm // HALO
        return pl.BlockSpec((None, HALO, width), lambda b, s: (b, jnp.maximum((s + off) * r - 1, 0), 0))

    def halo_next(self, width, off):
        r = self.tm // HALO
        last = self.s // HALO - 1
        return pl.BlockSpec((None, HALO, width), lambda b, s: (b, jnp.minimum((s + off + 1) * r, last), 0))

    def mod(self, j, off):
        nct, ctx_row = self.nct, self.ctx_row
        return pl.BlockSpec((None, None, 1, self.d),
                            lambda b, s: (jnp.where(s + off < nct, ctx_row, b), j, 0, 0))

    def const(self, shape):
        nd = len(shape)
        return pl.BlockSpec(shape, lambda b, s: (0,) * nd)


def _ada_kernel(c_ref, w_ref, b_ref, o_ref):
    c = c_ref[...]
    s = (c * _sigmoid(c)).astype(BF16)
    o_ref[...] = _dot(s, w_ref[...].astype(BF16)) + b_ref[...]


def _ada_mod(c_all, w_ada, b_ada):
    depth, d, n = w_ada.shape
    rows = c_all.shape[0]
    tn = 1024
    return pl.pallas_call(
        _ada_kernel,
        grid=(depth, n // tn),
        in_specs=[pl.BlockSpec((rows, d), lambda i, j: (0, 0)),
                  pl.BlockSpec((None, d, tn), lambda i, j: (i, 0, j)),
                  pl.BlockSpec((None, 1, tn), lambda i, j: (i, 0, j))],
        out_specs=pl.BlockSpec((None, rows, tn), lambda i, j: (i, 0, j)),
        out_shape=jax.ShapeDtypeStruct((depth, rows, n), F32),
        compiler_params=_params(("parallel", "parallel")),
        name="ada_mod",
    )(c_all, w_ada, b_ada.reshape(depth, 1, n))


def _in_proj_kernel(xc_ref, xl_ref, g_ref, sh_ref, sc_ref, wp_ref, wq_ref, wkv_ref, wrw_ref, wg_ref,
                    zp_ref, zq_ref, zkv_ref, zrw_ref, gate_ref, *, tl):
    h = _norm_mod(tl.pick(0, xc_ref, xl_ref), g_ref[...], sh_ref[...], sc_ref[...]).astype(BF16)
    zp_ref[...] = _dot(h, wp_ref[...])
    zq_ref[...] = _dot(h, wq_ref[...])
    zkv_ref[...] = _dot(h, wkv_ref[...])
    zrw_ref[...] = _dot(h, wrw_ref[...])
    d = h.shape[-1]
    for c in range(wg_ref.shape[1] // d):
        gate_ref[:, c * d:(c + 1) * d] = _sigmoid(_dot(h, wg_ref[:, c * d:(c + 1) * d])).astype(BF16)


def _in_proj(tl, x_ctx, x_lat, lat_off, g, mod, wp, wq, wkv, wrw, wg):
    b, d = x_ctx.shape[0], x_ctx.shape[-1]
    widths = (wp.shape[1], wq.shape[1], wkv.shape[1], wrw.shape[1], wg.shape[1])
    dts = (F32, F32, F32, F32, BF16)
    return pl.pallas_call(
        functools.partial(_in_proj_kernel, tl=tl),
        grid=tl.grid(0),
        in_specs=tl.tok_split(d, 0, lat_off) + [tl.const((1, d)), tl.mod(0, 0), tl.mod(1, 0)]
        + [tl.const(w.shape) for w in (wp, wq, wkv, wrw, wg)],
        out_specs=[tl.tok(w, 0) for w in widths],
        out_shape=[jax.ShapeDtypeStruct((b, tl.s, w), dt) for w, dt in zip(widths, dts)],
        compiler_params=_params(("parallel", "parallel")),
        name="in_proj",
    )(x_ctx, x_lat, g, mod, mod, wp, wq, wkv, wrw, wg)


def _seq_flags(tl, off):
    s_abs = pl.program_id(1) + off
    is_ctx = s_abs < tl.nct
    first = jnp.logical_or(s_abs == 0, s_abs == tl.nct)
    last = jnp.logical_or(s_abs == tl.nct - 1, s_abs == tl.ns - 1)
    seq_len = jnp.where(is_ctx, tl.lc, tl.l)
    tile_in_seq = jnp.where(is_ctx, s_abs, s_abs - tl.nct)
    return first, last, seq_len, tile_in_seq


def _pool_kernel(u_ref, up_ref, un_ref, pw_ref, ps_ref, o_ref, *, tl, off):
    first, last, seq_len, tile_in_seq = _seq_flags(tl, off)
    tm = tl.tm
    u = u_ref[...]
    prev = jnp.where(first, 0.0, up_ref[...])
    nxt = jnp.where(last, 0.0, un_ref[...])
    e = jnp.concatenate([prev, u, nxt], axis=0)
    n = tm + 2 * HALO
    w2 = e + pltpu.roll(e, 1, 0)
    w4 = pltpu.roll(w2, n - 1, 0) + pltpu.roll(w2, 1, 0)
    w8 = pltpu.roll(w4, n - 2, 0) + pltpu.roll(w4, 2, 0)
    w16 = pltpu.roll(w8, n - 4, 0) + pltpu.roll(w8, 4, 0)
    sums = [w[HALO:HALO + tm] for w in (w2, w4, w8, w16)]
    width = u.shape[1]
    group = width // len(POOL_HALF_WINDOWS)
    lane = _lane((tm, width))
    pos = tile_in_seq * tm + lax.broadcasted_iota(jnp.int32, (tm, width), 0)
    total = sums[-1]
    half = jnp.full((tm, width), POOL_HALF_WINDOWS[-1], jnp.int32)
    for gi in range(len(POOL_HALF_WINDOWS) - 2, -1, -1):
        sel = lane < (gi + 1) * group
        total = jnp.where(sel, sums[gi], total)
        half = jnp.where(sel, POOL_HALF_WINDOWS[gi], half)
    cnt = jnp.minimum(pos + half, seq_len) - jnp.maximum(pos - half, 0)
    pooled = total / cnt.astype(F32) - u
    o_ref[...] = (_dot(pooled.astype(BF16), pw_ref[...]) * ps_ref[...]).astype(BF16)


def _rope(x, tc, ts1, ts2):
    return x * tc + pltpu.roll(x, LANES - MLA_ROPE // 2, 1) * ts1 + pltpu.roll(x, MLA_ROPE // 2, 1) * ts2


def _head_inv_rms(x, seg):
    w = seg.shape[0]
    cols = [lax.rsqrt(_dot((xc * xc).astype(BF16), seg) + NORM_EPS)
            for xc in (x[:, c * w:(c + 1) * w] for c in range(x.shape[1] // w))]
    return jnp.concatenate(cols, axis=1)


def _qkv_kernel(zq_ref, zkv_ref, tc_ref, ts1_ref, ts2_ref, qng_ref, kvng_ref, gq_ref, gk_ref, gkr_ref, seg_ref,
                qpad_ref, kpad_ref, wuq_ref, wuk_ref, wuv_ref, u_ref, up_ref, un_ref, pw_ref, ps_ref,
                q_ref, k_ref, v_ref, op_ref, *, tl):
    _pool_kernel(u_ref, up_ref, un_ref, pw_ref, ps_ref, op_ref, tl=tl, off=0)
    tc, ts1, ts2 = tc_ref[...], ts1_ref[...], ts2_ref[...]
    seg = seg_ref[...]

    zq = zq_ref[...]
    qc = (zq * _rms(zq, zq.shape[-1]) * qng_ref[...]).astype(BF16)
    q = _dot(qc, wuq_ref[...])
    q = q * _head_inv_rms(q, seg)
    q_scale = LOG2E * (MLA_NOPE + MLA_ROPE) ** -0.5
    for h in range(MLA_HEADS):
        qh = _rope(q[:, h * HEAD_SLOT:(h + 1) * HEAD_SLOT] * gq_ref[...], tc, ts1, ts2)
        q_ref[:, h * HEAD_SLOT:(h + 1) * HEAD_SLOT] = (qh * q_scale + qpad_ref[...]).astype(BF16)

    zkv = zkv_ref[...]
    kv_w = kvng_ref.shape[-1]
    kvc = zkv[:, :kv_w]
    kvn = (kvc * _rms(kvc, kv_w) * kvng_ref[...]).astype(BF16)
    kr = zkv[:, kv_w:kv_w + LANES]
    kr = kr * lax.rsqrt(jnp.sum(kr * kr, axis=-1, keepdims=True) * (1.0 / MLA_ROPE) + NORM_EPS) * gkr_ref[...]
    kr = _rope(pltpu.roll(kr, MLA_NOPE, 1), tc, ts1, ts2)
    kn = _dot(kvn, wuk_ref[...])
    kn = kn * _head_inv_rms(kn, seg)
    for h in range(MLA_HEADS):
        kh = kn[:, h * HEAD_SLOT:(h + 1) * HEAD_SLOT]
        k_ref[:, h * HEAD_SLOT:(h + 1) * HEAD_SLOT] = (kh * gk_ref[...] + kr + kpad_ref[...]).astype(BF16)
    v_ref[...] = _dot(kvn, wuv_ref[...]).astype(BF16)


def _qkv_up(tl, zq, zkv, zp, pw_bd, ps, tabs, smalls, wuq, wuk, wuv, zrw, rsmalls):
    b, s, _ = zq.shape
    p_in, p_out, p_shape, p_args = _rwkv_prep_specs(tl, zrw, rsmalls)
    n_in = 5 + len(smalls) + 3 + 5

    def token_kernel(*refs):
        ins, outs = refs[:n_in + len(p_in)], refs[n_in + len(p_in):]
        _rwkv_prep_kernel(*ins[n_in:], *outs[4:], tl=tl, off=0)
        _qkv_kernel(*ins[:n_in], *outs[:4], tl=tl)

    tc, ts1, ts2 = tabs
    tab_spec = pl.BlockSpec((tl.tm, LANES), lambda bb, ss: (ss, 0))
    widths = (wuq.shape[1], wuk.shape[1], wuv.shape[1], zp.shape[-1])
    pw = zp.shape[-1]
    nct, nlt = tl.nct, tl.ns - tl.nct
    q_spec = pl.BlockSpec((None, tl.tm, widths[0]), lambda bb, ss: (bb, jnp.where(ss < nct, nlt + ss, ss - nct), 0))
    return pl.pallas_call(
        token_kernel,
        grid=tl.grid(0),
        in_specs=[tl.tok(zq.shape[-1], 0), tl.tok(zkv.shape[-1], 0), tab_spec, tab_spec, tab_spec]
        + [tl.const(a.shape) for a in smalls] + [tl.const(w.shape) for w in (wuq, wuk, wuv)]
        + [tl.tok(pw, 0), tl.halo_prev(pw, 0), tl.halo_next(pw, 0), tl.const(pw_bd.shape), tl.const(ps.shape)]
        + p_in,
        out_specs=[q_spec, tl.tok(widths[1], 0), tl.tok(widths[2], 0), tl.tok(pw, 0)] + p_out,
        out_shape=[jax.ShapeDtypeStruct((b, s, w), BF16) for w in widths] + p_shape,
        compiler_params=_params(("parallel", "parallel")),
        name="token_stage",
    )(zq, zkv, tc, ts1, ts2, *smalls, wuq, wuk, wuv, zp, zp, zp, pw_bd, ps, *p_args)


def _attn_kernel(q_ref, k_ref, v_ref, o_ref):
    v = v_ref[...]
    outs = []
    for hh in range(2):
        q = q_ref[:, hh * HEAD_SLOT:(hh + 1) * HEAD_SLOT]
        k = k_ref[:, hh * HEAD_SLOT:(hh + 1) * HEAD_SLOT]
        s = lax.dot_general(q, k, (((1,), (1,)), ((), ())), preferred_element_type=F32)
        p = jnp.exp2(s - jnp.max(s, axis=-1, keepdims=True))
        denom = jnp.sum(p, axis=-1, keepdims=True)
        outs.append(_dot(p.astype(BF16), v) / denom)
    o_ref[...] = jnp.where(_lane(outs[0].shape) < MLA_V, outs[0], outs[1]).astype(BF16)


def _attention(q, k, v, *, tq, q_off_tiles, n_q_tiles, n_keys):
    b, _, qw = q.shape
    pairs = qw // (2 * HEAD_SLOT)
    return pl.pallas_call(
        _attn_kernel,
        grid=(b, pairs, n_q_tiles),
        in_specs=[pl.BlockSpec((None, tq, 2 * HEAD_SLOT), lambda bb, hp, i: (bb, i + q_off_tiles, hp)),
                  pl.BlockSpec((None, n_keys, 2 * HEAD_SLOT), lambda bb, hp, i: (bb, 0, hp)),
                  pl.BlockSpec((None, n_keys, 2 * MLA_V), lambda bb, hp, i: (bb, 0, hp))],
        out_specs=pl.BlockSpec((None, tq, 2 * MLA_V), lambda bb, hp, i: (bb, i, hp)),
        out_shape=jax.ShapeDtypeStruct((b, n_q_tiles * tq, pairs * 2 * MLA_V), BF16),
        compiler_params=_params(("parallel", "parallel", "arbitrary")),
        name="attention",
    )(q, k, v)


def _rev_rows(x):
    n = x.shape[0]
    row = lax.broadcasted_iota(jnp.int32, x.shape, 0)
    for sh in (1, 2, 4):
        x = jnp.where((row & sh) == 0, pltpu.roll(x, n - sh, 0), pltpu.roll(x, sh, 0))
    groups = [x[g * SUBLANES:(g + 1) * SUBLANES] for g in range(n // SUBLANES)]
    return jnp.concatenate(groups[::-1], axis=0)


def _rev_tile(tl, s_abs):
    return jnp.where(s_abs < tl.nct, tl.nct - 1 - s_abs, tl.ns - 1 + tl.nct - s_abs)


def _rwkv_prep_kernel(z_ref, zp_ref, zn_ref, mu_ref, kkw_ref, w0_ref, w2_ref, a0_ref, a2_ref, ka_ref,
                      rk_ref, g2_ref, anti_ref,
                      r_o, v_o, nkk_o, gg_o, bonus_o, wf_o, kf_o, bf_o, rb_o, vb_o, nkkb_o, wb_o, kb_o, bb_o,
                      *, tl, off):
    first, last, _, _ = _seq_flags(tl, off)
    tm = tl.tm
    z = z_ref[...]
    row = lax.broadcasted_iota(jnp.int32, z.shape, 0)
    prev_row = jnp.where(first, 0.0, zp_ref[HALO - 1:HALO, :])
    next_row = jnp.where(last, 0.0, zn_ref[0:1, :])
    z_prev = jnp.where(row == 0, prev_row, pltpu.roll(z, 1, 0))
    z_next = jnp.where(row == tm - 1, next_row, pltpu.roll(z, tm - 1, 0))
    zs = z + mu_ref[0:1, :] * (z_prev - z) + mu_ref[1:2, :] * (z_next - z)

    w = kkw_ref.shape[-1]
    r, k, v = zs[:, 0:w], zs[:, w:2 * w], zs[:, 2 * w:3 * w]
    o = 3 * w
    wd = zs[:, o:o + 2 * DECAY_RANK]
    ad = zs[:, o + 2 * DECAY_RANK:o + 2 * DECAY_RANK + 2 * AAA_RANK]
    gd = zs[:, o + 2 * DECAY_RANK + 2 * AAA_RANK:]

    kk = k * kkw_ref[...]
    kk = kk * lax.rsqrt(jnp.maximum(_seg64_sum(kk * kk), 1e-24))
    u = w0_ref[...] + _dot(jnp.tanh(wd).astype(BF16), w2_ref[...])
    log_decay = -math.exp(-0.5) * _sigmoid(u)
    a = _sigmoid(a0_ref[...] + _dot(ad.astype(BF16), a2_ref[...]))
    ka = ka_ref[...]
    k_sum = jnp.zeros_like(k)
    anti = anti_ref[...]

    def put16(o_ref, t, backward):
        t = t.T.astype(BF16)
        o_ref[...] = _dot(t, anti).astype(BF16) if backward else t

    for d, (w_o, k_o, b_o) in enumerate(((wf_o, kf_o, bf_o), (wb_o, kb_o, bb_o))):
        a_d = a[:, d * w:(d + 1) * w]
        k_d = k * (1.0 + (a_d - 1.0) * ka[:, d * w:(d + 1) * w])
        put16(w_o, log_decay[:, d * w:(d + 1) * w], d)
        put16(k_o, k_d, d)
        put16(b_o, kk * a_d, d)
        k_sum = k_sum + k_d
    put16(r_o, r, 0)
    put16(rb_o, r, 1)
    put16(nkk_o, -kk, 0)
    put16(nkkb_o, -kk, 1)
    put16(v_o, v, 0)
    put16(vb_o, v, 1)
    gg_o[...] = _dot(_sigmoid(gd).astype(BF16), g2_ref[...])
    bonus_o[...] = _seg64_sum(r * (0.5 * k_sum) * rk_ref[...]) * v


def _rwkv_prep_specs(tl, zrw, smalls):
    b, s, win = zrw.shape
    w = smalls[1].shape[-1]
    fwd = pl.BlockSpec((None, w, tl.tm), lambda bb, ss: (bb, 0, ss))
    bwd = pl.BlockSpec((None, w, tl.tm), lambda bb, ss: (bb, 0, _rev_tile(tl, ss)))
    tok = tl.tok(w, 0)

    def chan(dt):
        return jax.ShapeDtypeStruct((b, w, s), dt)

    tokf = jax.ShapeDtypeStruct((b, s, w), F32)
    in_specs = [tl.tok(win, 0), tl.halo_prev(win, 0), tl.halo_next(win, 0)] + [tl.const(a.shape) for a in smalls]
    out_specs = [fwd, fwd, fwd, tok, tok, fwd, fwd, fwd] + [bwd] * 6
    out_shape = [chan(BF16), chan(BF16), chan(BF16), tokf, tokf] + [chan(BF16)] * 9
    return in_specs, out_specs, out_shape, (zrw, zrw, zrw, *smalls)


def _zero_after(x):
    bits = pltpu.bitcast(x[:SUBLANES, :LANES].astype(F32), jnp.uint32)
    return pltpu.bitcast((bits >> 16) >> 16, F32)


def _scan_step(t, a_ref, b_ref, k_ref, r_ref, v_ref, y_ref, s_ref, after):
    n_k = s_ref.shape[0]
    v = v_ref[t].astype(F32) + jnp.concatenate([after] * (v_ref.shape[1] // SUBLANES), axis=0)
    acc = [jnp.zeros_like(v), jnp.zeros_like(v)]
    for kk in range(n_k):
        acc[kk % 2] = acc[kk % 2] + s_ref[kk] * a_ref[t, pl.ds(kk, 1), :]
    sa = acc[0] + acc[1]
    yacc = [jnp.zeros_like(v), jnp.zeros_like(v)]
    for kk in range(n_k):
        sn = s_ref[kk] + sa * b_ref[t, pl.ds(kk, 1), :] + v * k_ref[t, pl.ds(kk, 1), :]
        s_ref[kk] = sn
        yacc[kk % 2] = yacc[kk % 2] + sn * r_ref[t, pl.ds(kk, 1), :]
    y_ref[t] = yacc[0] + yacc[1]


def _attn_scan_kernel(static_ref, q_ref, k_ref, v_ref, ab_ref, kr_ref, lw_ref, vv_ref, o_ref, y_ref,
                      s_ref, m_ref, acc_ref, ops_ref, *, tk):
    first = jnp.logical_and(pl.program_id(0) == 0, jnp.logical_and(pl.program_id(1) == 0, pl.program_id(2) == 0))

    @pl.when(first)
    def _():
        s_ref[...] = jnp.zeros_like(s_ref)

    def unpack_pair(ref):
        x = ref[...].astype(F32)
        swapped = pltpu.roll(x.reshape(-1, LANES), LANES // 2, 1).reshape(x.shape)
        lo = lax.broadcasted_iota(jnp.int32, x.shape, 2) < LANES // 2
        return jnp.where(lo, x, swapped), jnp.where(lo, swapped, x)

    nkv = vv_ref.shape[0]
    a, b = unpack_pair(ab_ref)
    kd, r = unpack_pair(kr_ref)
    lw = jnp.concatenate(unpack_pair(lw_ref), axis=1)
    log_w = jnp.zeros(lw.shape[1:], F32)
    for t in range(nkv):
        ops_ref[0, t] = a[t] * jnp.exp(log_w)
        log_w = log_w + lw[t]
        inv_w = jnp.exp(-log_w)
        ops_ref[1, t] = b[t] * inv_w
        ops_ref[2, t] = kd[t] * inv_w
        ops_ref[3, t] = r[t] * jnp.exp(log_w)
    ops_ref[4, 0] = jnp.exp(log_w)
    a_ref, b_ref, kk_ref, r_ref = (ops_ref.at[i] for i in range(4))

    m_ref[...] = jnp.full_like(m_ref, -1e30)
    acc_ref[...] = jnp.zeros_like(acc_ref)
    lo_half = _lane((tk, 2 * MLA_V)) < MLA_V

    def block(j, after, static_max):
        _scan_step(j, a_ref, b_ref, kk_ref, r_ref, vv_ref, y_ref, s_ref, after)
        start = j * tk
        vblk = v_ref[pl.ds(start, tk), :]
        v_ones = (jnp.where(lo_half, vblk, 1.0), jnp.where(lo_half, 1.0, vblk))
        for hh in range(2):
            q = q_ref[:, hh * HEAD_SLOT:(hh + 1) * HEAD_SLOT]
            kblk = k_ref[pl.ds(start, tk), hh * HEAD_SLOT:(hh + 1) * HEAD_SLOT]
            s = lax.dot_general(q, kblk, (((1,), (1,)), ((), ())), preferred_element_type=F32)
            if static_max:
                new = acc_ref[hh] + _dot(jnp.exp2(s).astype(BF16), v_ones[hh])
            else:
                m_old = m_ref[hh]
                m_new = jnp.maximum(m_old, jnp.max(s, axis=-1, keepdims=True))
                p = jnp.exp2(s - jnp.concatenate([m_new] * (tk // LANES), axis=1))
                new = jnp.exp2(m_old - m_new) * acc_ref[hh] + _dot(p.astype(BF16), v_ones[hh])
                m_ref[hh] = m_new
            acc_ref[hh] = new
        return _zero_after(new)

    def run(static_max):
        after = jnp.zeros((SUBLANES, LANES), F32)
        for j in range(nkv):
            after = block(j, after, static_max)

    use_static = static_ref[0] != 0
    pl.when(use_static)(functools.partial(run, True))
    pl.when(jnp.logical_not(use_static))(functools.partial(run, False))
    for kk in range(s_ref.shape[0]):
        s_ref[kk] = s_ref[kk] * ops_ref[4, 0, pl.ds(kk, 1), :]
    a0, a1 = acc_ref[0], acc_ref[1]
    o = jnp.where(_lane(a0.shape) < MLA_V, a0 / pltpu.roll(a0, MLA_V, 1), a1 / pltpu.roll(a1, MLA_V, 1))
    o_ref[...] = o.astype(BF16)


def _score_bound(gain_q, gain_k):
    def norm2(g):
        return MLA_NOPE * jnp.max(jnp.square(g[:MLA_NOPE])) + MLA_ROPE * jnp.max(jnp.square(g[MLA_NOPE:]))
    q_scale = LOG2E * (MLA_NOPE + MLA_ROPE) ** -0.5
    return 1.02 * q_scale * jnp.sqrt(norm2(gain_q) * norm2(gain_k))


STATIC_SOFTMAX_MAX_BOUND = 50.0


def _attn_scan(q, k, v, ops_k, op_v, bound, *, tq, tk, n_q):
    b, s, qw = q.shape[0], k.shape[1], q.shape[2]
    pairs = qw // (2 * HEAD_SLOT)
    nq_t, nkv = n_q // tq, s // tk
    n_k, nv = ops_k[0].shape[1], op_v.shape[1]
    assert op_v.shape[0] == b * pairs * nq_t * nkv and op_v.shape[2] == LANES

    def step_idx(bb, hp, i):
        return ((bb * pairs + hp) * nq_t + i, 0, 0)

    step_specs = [pl.BlockSpec((nkv, x.shape[1], LANES), step_idx) for x in (*ops_k, op_v)]
    vspec = step_specs[-1]

    use_static = (bound <= STATIC_SOFTMAX_MAX_BOUND).astype(jnp.int32).reshape(1)
    return pl.pallas_call(
        functools.partial(_attn_scan_kernel, tk=tk),
        grid=(b, pairs, nq_t),
        in_specs=[pl.BlockSpec(memory_space=pltpu.SMEM),
                  pl.BlockSpec((None, tq, 2 * HEAD_SLOT), lambda bb, hp, i: (bb, i, hp)),
                  pl.BlockSpec((None, s, 2 * HEAD_SLOT), lambda bb, hp, i: (bb, 0, hp)),
                  pl.BlockSpec((None, s, 2 * MLA_V), lambda bb, hp, i: (bb, 0, hp))] + step_specs,
        out_specs=[pl.BlockSpec((None, tq, 2 * MLA_V), lambda bb, hp, i: (bb, i, hp)), vspec],
        out_shape=[jax.ShapeDtypeStruct((b, n_q, pairs * 2 * MLA_V), BF16), jax.ShapeDtypeStruct(op_v.shape, F32)],
        scratch_shapes=[pltpu.VMEM((n_k, nv, LANES), F32), pltpu.VMEM((2, tq, LANES), F32),
                        pltpu.VMEM((2, tq, 2 * MLA_V), F32), pltpu.VMEM((5, nkv, n_k, LANES), F32)],
        compiler_params=_params(("arbitrary", "arbitrary", "arbitrary")),
        name="attn_scan",
    )(use_static, q, k, v, *ops_k, op_v)


def _scan_operands(fwd, bwd, n_steps):
    r, v, nkk, w_f, k_f, b_f = fwd
    r_b, v_b, nkk_b, w_b, k_b, b_b = bwd
    b, width, s = r.shape
    heads = width // RWKV_HEAD
    half = RWKV_HEAD // 2
    nlh = 2 * b * heads
    assert nlh <= LANES // 2

    def pad(x):
        return jnp.pad(x, ((0, n_steps - s), (0, 0), (0, 0), (0, LANES // 2 - nlh))).reshape(n_steps, -1, LANES)

    def k_half(x_f, x_b):
        z = jnp.stack([x_f, x_b]).reshape(2, b, heads, RWKV_HEAD, s)
        return z.transpose(4, 3, 0, 1, 2).reshape(s, RWKV_HEAD, nlh)

    def pair(lo, hi):
        return pad(jnp.stack([lo, hi], axis=2))

    lw = k_half(w_f, w_b)
    ops_k = [pair(k_half(nkk, nkk_b), k_half(b_f, b_b)), pair(k_half(k_f, k_b), k_half(r, r_b)),
             pair(lw[:, :RWKV_HEAD // 2], lw[:, RWKV_HEAD // 2:])]
    z = jnp.stack([v, v_b]).reshape(2, b, heads, 2, half, s)
    return ops_k, pad(z.transpose(5, 4, 3, 0, 1, 2).reshape(s, half, 2, nlh))


def _scan_result(y, b, s, width):
    heads = width // RWKV_HEAD
    half = RWKV_HEAD // 2
    y = y[:s].reshape(s, half, 2, LANES // 2)[..., :2 * b * heads]
    return y.reshape(s, half, 2, 2, b, heads).transpose(3, 4, 0, 5, 2, 1).reshape(2, b, s, width)


def _merge_kernel(yf_ref, yb_ref, bonus_ref, gg_ref, op_ref, omc_ref, oml_ref, gate_ref, xc_ref, xl_ref, g1_ref,
                  lnw_ref, lnb_ref, wbp_ref, wbm_ref, wbr_ref, wo_ref, o_ref, *, tl, off):
    y = yf_ref[...] + _rev_rows(yb_ref[...])
    mu = _seg64_sum(y) * (1.0 / RWKV_HEAD)
    yc = y - mu
    var = _seg64_sum(yc * yc) * (1.0 / RWKV_HEAD)
    yn = yc * lax.rsqrt(var + RWKV_GN_EPS) * lnw_ref[...] + lnb_ref[...]
    o_rw = ((yn + bonus_ref[...]) * gg_ref[...]).astype(BF16)
    x = tl.pick(off, xc_ref, xl_ref)
    d = x.shape[-1]
    m = (gate_ref[:, 0:d].astype(F32) * _dot(op_ref[...], wbp_ref[...])
         + gate_ref[:, d:2 * d].astype(F32) * _dot(tl.pick(off, omc_ref, oml_ref), wbm_ref[...])
         + gate_ref[:, 2 * d:3 * d].astype(F32) * _dot(o_rw, wbr_ref[...]))
    o_ref[...] = x + g1_ref[...] * _dot(m.astype(BF16), wo_ref[...])


def _merge(tl, off, y, bonus, gg, o_pool, o_mla_ctx, o_mla_lat, gate, x_ctx, x_lat, lat_off, mod,
           lnw, lnb, wbp, wbm, wbr, wo):
    b, d = x_ctx.shape[0], x_ctx.shape[-1]
    n_tiles = tl.ns - off
    out_spec = pl.BlockSpec((None, tl.tm, d), lambda bb, ss: (bb, ss, 0))
    rw = y.shape[-1]
    yf_spec = pl.BlockSpec((None, None, tl.tm, rw), lambda bb, ss: (0, bb, ss + off, 0))
    yb_spec = pl.BlockSpec((None, None, tl.tm, rw), lambda bb, ss: (1, bb, _rev_tile(tl, ss + off), 0))
    return pl.pallas_call(
        functools.partial(_merge_kernel, tl=tl, off=off),
        grid=tl.grid(off),
        in_specs=[yf_spec, yb_spec, tl.tok(rw, off), tl.tok(rw, off), tl.tok(o_pool.shape[-1], off)]
        + tl.tok_split(o_mla_lat.shape[-1], off, 0) + [tl.tok(gate.shape[-1], off)]
        + tl.tok_split(d, off, lat_off) + [tl.mod(2, off), tl.const(lnw.shape), tl.const(lnb.shape)]
        + [tl.const(w.shape) for w in (wbp, wbm, wbr, wo)],
        out_specs=out_spec,
        out_shape=jax.ShapeDtypeStruct((b, n_tiles * tl.tm, d), F32),
        compiler_params=_params(("parallel", "parallel")),
        name="merge",
    )(y, y, bonus, gg, o_pool, o_mla_ctx, o_mla_lat, gate, x_ctx, x_lat, mod, lnw, lnb, wbp, wbm, wbr, wo)


def _mlp_kernel(x_ref, g_ref, *rest, parts, tm):
    mods, (w1_ref, w2_ref, o_ref, h_ref, acc_ref) = rest[:3 * parts], rest[3 * parts:]
    j = pl.program_id(1)

    @pl.when(j == 0)
    def _():
        for p in range(parts):
            rows = slice(p * tm, (p + 1) * tm)
            h_ref[rows, :] = _norm_mod(x_ref[rows, :], g_ref[...], mods[3 * p][...], mods[3 * p + 1][...]).astype(BF16)
        acc_ref[...] = jnp.zeros_like(acc_ref)

    a = jnp.maximum(_dot(h_ref[...], w1_ref[...]), 0.0)
    acc_ref[...] += _dot((a * a).astype(BF16), w2_ref[...])

    @pl.when(j == pl.num_programs(1) - 1)
    def _():
        for p in range(parts):
            rows = slice(p * tm, (p + 1) * tm)
            o_ref[rows, :] = x_ref[rows, :] + mods[3 * p + 2][...] * acc_ref[rows, :]


def _mlp(tl, x1, mod_off, g, mod, w1, w2):
    b, s1, d = x1.shape
    dff = w1.shape[1]
    tf = 1024 if dff % 1024 == 0 else dff
    tm = tl.tm
    per_batch = s1 // tm
    parts = next(p for p in (4, 2, 1) if (b * per_batch) % p == 0)
    nct, ctx_row = tl.nct, tl.ctx_row

    def mod_spec(j, p):
        def idx(ii, jj):
            sub = ii * parts + p
            bb, ss = sub // per_batch, sub % per_batch
            return (jnp.where(ss + mod_off < nct, ctx_row, bb), j, 0, 0)
        return pl.BlockSpec((None, None, 1, d), idx)

    mod_specs = [mod_spec(j, p) for p in range(parts) for j in (3, 4, 5)]
    x_spec = pl.BlockSpec((parts * tm, d), lambda ii, jj: (ii, 0))
    out = pl.pallas_call(
        functools.partial(_mlp_kernel, parts=parts, tm=tm),
        grid=(b * per_batch // parts, dff // tf),
        in_specs=[x_spec, pl.BlockSpec((1, d), lambda ii, jj: (0, 0))] + mod_specs
        + [pl.BlockSpec((d, tf), lambda ii, jj: (0, jj)), pl.BlockSpec((tf, d), lambda ii, jj: (jj, 0))],
        out_specs=x_spec,
        out_shape=jax.ShapeDtypeStruct((b * s1, d), F32),
        scratch_shapes=[pltpu.VMEM((parts * tm, d), BF16), pltpu.VMEM((parts * tm, d), F32)],
        compiler_params=_params(("parallel", "arbitrary")),
        name="mlp",
    )(x1.reshape(b * s1, d), g, *([mod] * (3 * parts)), w1, w2)
    return out.reshape(b, s1, d)


def _rope_tables(lc, l):
    rows = l // GRID_W
    row = jnp.repeat(jnp.arange(rows), GRID_W).astype(F32)
    col = jnp.tile(jnp.arange(GRID_W), rows).astype(F32)
    n_freq = MLA_ROPE // 4
    inv_freq = jnp.power(ROPE_BASE, -jnp.arange(n_freq, dtype=F32) / n_freq)
    ang = jnp.concatenate([row[:, None] * inv_freq, col[:, None] * inv_freq], axis=-1)
    cos = jnp.concatenate([jnp.ones((lc, MLA_ROPE // 2), F32), jnp.cos(ang)], axis=0)
    sin = jnp.concatenate([jnp.zeros((lc, MLA_ROPE // 2), F32), jnp.sin(ang)], axis=0)
    s = lc + l
    pad = jnp.zeros((s, LANES - MLA_NOPE - MLA_ROPE), F32)
    z16 = jnp.zeros((s, MLA_ROPE // 2), F32)
    zn = jnp.zeros((s, MLA_NOPE), F32)
    tc = jnp.concatenate([jnp.ones((s, MLA_NOPE), F32), cos, cos, pad], axis=1)
    ts1 = jnp.concatenate([zn, -sin, z16, pad], axis=1)
    ts2 = jnp.concatenate([zn, z16, sin, pad], axis=1)
    return tc, ts1, ts2


def _slot_cols(w, per_head):
    k = w.shape[0]
    w = w.reshape(k, MLA_HEADS, per_head)
    return jnp.pad(w, ((0, 0), (0, 0), (0, HEAD_SLOT - per_head))).reshape(k, MLA_HEADS * HEAD_SLOT)


def _block_diag(blocks):
    n = len(blocks)
    rows = []
    for i, blk in enumerate(blocks):
        rows.append(jnp.concatenate(
            [blk if j == i else jnp.zeros((blk.shape[0], blocks[j].shape[1]), blk.dtype) for j in range(n)], axis=1))
    return jnp.concatenate(rows, axis=0)


def _row(x):
    return x.reshape(1, -1).astype(F32)


def kernel(x, c, ctx, c_ctx, norm1_g, norm2_g, w_ada, b_ada, w_in, pool_w, pool_scale, mla_q_norm, mla_w_uq, mla_kv_norm, mla_w_ukv, qk_gain_q, qk_gain_k, rwkv_mu, rwkv_w0, rwkv_w2, rwkv_a0, rwkv_a2, rwkv_ka, rwkv_kk, rwkv_rk, rwkv_g2, rwkv_ln_w, rwkv_ln_b, w_br_pool, w_br_mla, w_br_rwkv, w_o, mlp_w1, mlp_w2):
    b, l, d = x.shape
    lc = ctx.shape[1]
    depth = w_in.shape[0]
    tl = _Tiles(b, lc, l, d)

    pool_width = pool_scale.shape[-1]
    q_rank = mla_q_norm.shape[-1]
    kv_rank = mla_kv_norm.shape[-1]
    rw_width = rwkv_kk.shape[-1]
    rw_in = rwkv_mu.shape[-1]

    rows = -(-(b + 1) // SUBLANES) * SUBLANES
    c_all = jnp.concatenate([c, c_ctx[None, :], jnp.zeros((rows - b - 1, d), F32)], axis=0)
    mod_all = _ada_mod(c_all, w_ada, b_ada).reshape(depth, rows, N_MOD, 1, d)
    tabs = _rope_tables(lc, l)
    seg = jnp.zeros((HEAD_SLOT, HEAD_SLOT), F32)
    seg = seg.at[:MLA_NOPE, :MLA_NOPE].set(1.0 / MLA_NOPE)
    seg = seg.at[MLA_NOPE:MLA_NOPE + MLA_ROPE, MLA_NOPE:MLA_NOPE + MLA_ROPE].set(1.0 / MLA_ROPE)
    seg = _block_diag([seg, seg]).astype(BF16)
    anti = jnp.eye(tl.tm, dtype=BF16)[::-1]

    tk = 256 if tl.s % 256 == 0 else LANES
    tq = 512
    while tq * tk > (MLA_HEADS // 2) * b * l or l % tq:
        tq //= 2
    assert tq >= 16

    x_ctx, x_lat, lat_off = ctx, x, 0
    out = None
    for i in range(depth):
        need_ctx = i < depth - 1
        off = 0 if need_ctx else tl.nct
        mod = mod_all[i]

        wi = w_in[i].astype(BF16)
        o0 = 0
        wp = wi[:, o0:o0 + pool_width]; o0 += pool_width
        wq = wi[:, o0:o0 + q_rank]; o0 += q_rank
        wkv = wi[:, o0:o0 + kv_rank + MLA_ROPE]; o0 += kv_rank + MLA_ROPE
        wkv = jnp.pad(wkv, ((0, 0), (0, LANES - MLA_ROPE)))
        wrw = wi[:, o0:o0 + rw_in]; o0 += rw_in
        wg = wi[:, o0:]

        zp, zq, zkv, zrw, gate = _in_proj(tl, x_ctx, x_lat, lat_off, _row(norm1_g[i]), mod, wp, wq, wkv, wrw, wg)

        pw_bd = _block_diag([pool_w[i, g] for g in range(pool_w.shape[1])]).astype(BF16)

        wuq = _slot_cols(mla_w_uq[i], MLA_NOPE + MLA_ROPE).astype(BF16)
        wukv = mla_w_ukv[i].reshape(kv_rank, MLA_HEADS, MLA_NOPE + MLA_V)
        wuk = _slot_cols(wukv[:, :, :MLA_NOPE].reshape(kv_rank, -1), MLA_NOPE).astype(BF16)
        wuv = wukv[:, :, MLA_NOPE:].reshape(kv_rank, MLA_HEADS * MLA_V).astype(BF16)
        zpad = jnp.zeros((LANES - MLA_NOPE - MLA_ROPE,), F32)
        gq = _row(jnp.concatenate([qk_gain_q[i], zpad]))
        gk = _row(jnp.concatenate([qk_gain_k[i, :MLA_NOPE], jnp.zeros((LANES - MLA_NOPE,), F32)]))
        gkr = _row(jnp.concatenate([qk_gain_k[i, MLA_NOPE:], jnp.zeros((LANES - MLA_ROPE,), F32)]))
        bound = _score_bound(qk_gain_q[i], qk_gain_k[i])
        spare = (jnp.arange(LANES) == MLA_NOPE + MLA_ROPE).astype(F32)[None, :]
        smalls = (_row(mla_q_norm[i]), _row(mla_kv_norm[i]), gq, gk, gkr, seg, spare, -bound * spare)

        w2cat = _block_diag([rwkv_w2[i, 0], rwkv_w2[i, 1]]).astype(BF16)
        a2cat = _block_diag([rwkv_a2[i, 0], rwkv_a2[i, 1]]).astype(BF16)
        rsmalls = (rwkv_mu[i].astype(F32), _row(rwkv_kk[i]), _row(rwkv_w0[i]), w2cat, _row(rwkv_a0[i]), a2cat,
                   _row(rwkv_ka[i]), _row(rwkv_rk[i]), rwkv_g2[i].astype(BF16), anti)
        (q, k, v, o_pool, r, vv, nkk, gg, bonus, wf, kf, bf, rb, vb, nkkb, wb, kb, bb) = _qkv_up(
            tl, zq, zkv, zp, pw_bd, _row(pool_scale[i]), tabs, smalls, wuq, wuk, wuv, zrw, rsmalls)
        n_steps = b * (MLA_HEADS // 2) * (l // tq) * (tl.s // tk)
        ops_k, op_v = _scan_operands((r, vv, nkk, wf, kf, bf), (rb, vb, nkkb, wb, kb, bb), n_steps)
        o_mla_l, y = _attn_scan(q, k, v, ops_k, op_v, bound, tq=tq, tk=tk, n_q=l)
        y = _scan_result(y, b, tl.s, rw_width)
        if need_ctx:
            o_mla_c = _attention(q, k, v, tq=tl.tm, q_off_tiles=l // tl.tm, n_q_tiles=tl.nct, n_keys=lc)
        else:
            o_mla_c = o_mla_l

        x1 = _merge(tl, off, y, bonus, gg, o_pool, o_mla_c, o_mla_l, gate, x_ctx, x_lat, lat_off, mod,
                    _row(rwkv_ln_w[i]), _row(rwkv_ln_b[i]),
                    w_br_pool[i].astype(BF16), w_br_mla[i].astype(BF16), w_br_rwkv[i].astype(BF16),
                    w_o[i].astype(BF16))
        xc_next = _mlp(tl, x1, off, _row(norm2_g[i]), mod, mlp_w1[i].astype(BF16), mlp_w2[i].astype(BF16))
        if need_ctx:
            x_ctx, x_lat, lat_off = xc_next, xc_next, tl.nct
        else:
            out = xc_next
    return out
```
